```python
import math
import jax
import jax.numpy as jnp
from jax import lax
import numpy as np

D_MODEL = 1024
BATCH = 4
SEQ = 4096
DEPTH = 1

D_MIX = D_MODEL
D_NSA = D_MIX // 2
D_CONV = D_MIX - D_NSA
HEAD_DIM = 64
N_HEADS = D_NSA // HEAD_DIM
N_KV = 2
REP = N_HEADS // N_KV
KV_W = N_KV * HEAD_DIM
N_GATES = 3 * N_HEADS
D_IN = D_NSA + 6 * KV_W + N_GATES + 3 * D_CONV
L_CMP = 32
S_CMP = 16
L_SLC = 64
N_SEL = 16
WINDOW = 512
Q_BLOCK = 128
CMP_HIDDEN = 256
CONV_W = 3
D_FF = 2816
N_BUCKETS = 32
MAX_DIST = 128
RMS_EPS = 1e-6
NEG = -1e30
FORCE_BONUS = 1e4

kernel_name = "hybrid_nsa_shortconv_convffn_block"


def rms_norm(x, g):
    xf = x.astype(jnp.float32)
    y = xf * lax.rsqrt(jnp.mean(xf * xf, axis=-1, keepdims=True) + RMS_EPS)
    return (y * g.astype(jnp.float32)).astype(x.dtype)


def causal_dwconv(x, w):
    return lax.conv_general_dilated(
        x, w[:, None, :].astype(x.dtype), window_strides=(1,),
        padding=[(CONV_W - 1, 0)], dimension_numbers=('NWC', 'WIO', 'NWC'),
        feature_group_count=x.shape[-1])


def t5_bucket(dist):
    max_exact = N_BUCKETS // 2
    d = jnp.maximum(dist, 0)
    df = jnp.maximum(d, 1).astype(jnp.float32)
    large = max_exact + (jnp.log(df / max_exact) / math.log(MAX_DIST / max_exact)
                         * (N_BUCKETS - max_exact)).astype(jnp.int32)
    return jnp.where(d < max_exact, d, jnp.minimum(large, N_BUCKETS - 1))


def masked_softmax(logits, mask):
    z = jnp.where(mask, logits.astype(jnp.float32), NEG)
    p = jax.nn.softmax(z, axis=-1)
    return p * jnp.any(mask, axis=-1, keepdims=True)


def compress_kv(kv, pe, w1, w2):
    B, S, G, dh = kv.shape
    n_cmp = (S - L_CMP) // S_CMP + 1
    idx = jnp.arange(n_cmp)[:, None] * S_CMP + jnp.arange(L_CMP)[None, :]
    blk = kv[:, idx] + pe[None, None, :, None, :]
    blk = blk.transpose(0, 3, 1, 2, 4).reshape(B, G, n_cmp, L_CMP * dh)
    return jax.nn.gelu(blk @ w1, approximate=True) @ w2


def nsa_attention(q, k_c, v_c, k_s, v_s, k_w, v_w, gate_logits,
                  pe_k, pe_v, wk1, wk2, wv1, wv2, rel_bias):
    B, S = q.shape[0], q.shape[1]
    n_cmp = (S - L_CMP) // S_CMP + 1
    n_slc = S // L_SLC
    n_sel = min(N_SEL, n_slc)
    nqb = S // Q_BLOCK
    scale = HEAD_DIM ** -0.5
    t = jnp.arange(S)
    tab = rel_bias.reshape(N_KV, REP, N_BUCKETS)
    qg = q.reshape(B, S, N_KV, REP, HEAD_DIM).transpose(0, 2, 3, 1, 4)

    kc = compress_kv(k_c, pe_k, wk1, wk2)
    vc = compress_kv(v_c, pe_v, wv1, wv2)
    c_start = jnp.arange(n_cmp) * S_CMP
    dist_c = t[:, None] - (c_start + L_CMP - 1)[None, :]
    logits_c = (jnp.einsum('bgrsd,bgnd->bgrsn', qg, kc) * scale
                + tab[:, :, t5_bucket(dist_c)])
    p_c = masked_softmax(logits_c, dist_c >= 0)
    o_c = jnp.einsum('bgrsn,bgnd->bgrsd', p_c.astype(vc.dtype), vc)

    s_start = jnp.arange(n_slc) * L_SLC
    overlap = jnp.clip(jnp.minimum(c_start[:, None] + L_CMP, s_start[None, :] + L_SLC)
                       - jnp.maximum(c_start[:, None], s_start[None, :]), 0, None)
    overlap = overlap.astype(jnp.float32) / L_CMP
    imp = jnp.einsum('bgrsn,nj->bgsj', p_c, overlap)
    j = jnp.arange(n_slc)[None, :]
    cur = (t // L_SLC)[:, None]
    forced = (j == 0) | (j == cur) | (j == cur - 1)
    score = jnp.where(j <= cur, imp + jnp.where(forced, FORCE_BONUS, 0.0), NEG)
    top_val, top_idx = lax.top_k(score, n_sel)
    top_ok = top_val > 0.5 * NEG

    k_sb = k_s.reshape(B, n_slc, L_SLC, N_KV, HEAD_DIM).transpose(0, 3, 1, 2, 4)
    v_sb = v_s.reshape(B, n_slc, L_SLC, N_KV, HEAD_DIM).transpose(0, 3, 1, 2, 4)
    k_wp = jnp.pad(k_w, ((0, 0), (WINDOW, 0), (0, 0), (0, 0)))
    v_wp = jnp.pad(v_w, ((0, 0), (WINDOW, 0), (0, 0), (0, 0)))
    bi = jnp.arange(B)[:, None, None, None]
    gi = jnp.arange(N_KV)[None, :, None, None]

    def block_fn(args):
        n, q_b, idx_b, ok_b = args
        t_b = n * Q_BLOCK + jnp.arange(Q_BLOCK)
        ks = k_sb[bi, gi, idx_b].reshape(B, N_KV, Q_BLOCK, n_sel * L_SLC, HEAD_DIM)
        vs = v_sb[bi, gi, idx_b].reshape(B, N_KV, Q_BLOCK, n_sel * L_SLC, HEAD_DIM)
        pos = (idx_b[..., None] * L_SLC + jnp.arange(L_SLC)).reshape(
            B, N_KV, Q_BLOCK, n_sel * L_SLC)
        dist_s = t_b[None, None, :, None] - pos
        m_s = (dist_s >= 0) & jnp.repeat(ok_b, L_SLC, axis=-1)
        bias_s = jnp.moveaxis(tab[gi, :, t5_bucket(dist_s)], -1, 2)
        logits_s = jnp.einsum('bgrqd,bgqkd->bgrqk', q_b, ks) * scale + bias_s
        p_s = masked_softmax(logits_s, m_s[:, :, None])
        o_s = jnp.einsum('bgrqk,bgqkd->bgrqd', p_s.astype(vs.dtype), vs)
        kw = lax.dynamic_slice_in_dim(k_wp, n * Q_BLOCK, Q_BLOCK + WINDOW, axis=1)
        vw = lax.dynamic_slice_in_dim(v_wp, n * Q_BLOCK, Q_BLOCK + WINDOW, axis=1)
        s_w = n * Q_BLOCK - WINDOW + jnp.arange(Q_BLOCK + WINDOW)
        dist_w = t_b[:, None] - s_w[None, :]
        m_w = (dist_w >= 0) & (dist_w < WINDOW) & (s_w[None, :] >= 0)
        logits_w = (jnp.einsum('bgrqd,bkgd->bgrqk', q_b, kw) * scale
                    + tab[:, :, t5_bucket(dist_w)])
        p_w = masked_softmax(logits_w, m_w)
        o_w = jnp.einsum('bgrqk,bkgd->bgrqd', p_w.astype(vw.dtype), vw)
        return o_s, o_w

    q_blocks = jnp.moveaxis(qg.reshape(B, N_KV, REP, nqb, Q_BLOCK, HEAD_DIM), 3, 0)
    idx_blocks = jnp.moveaxis(top_idx.reshape(B, N_KV, nqb, Q_BLOCK, n_sel), 2, 0)
    ok_blocks = jnp.moveaxis(top_ok.reshape(B, N_KV, nqb, Q_BLOCK, n_sel), 2, 0)
    o_s, o_w = lax.map(block_fn, (jnp.arange(nqb), q_blocks, idx_blocks, ok_blocks))
    o_s = jnp.moveaxis(o_s, 0, 3).reshape(B, N_KV, REP, S, HEAD_DIM)
    o_w = jnp.moveaxis(o_w, 0, 3).reshape(B, N_KV, REP, S, HEAD_DIM)

    g = jax.nn.sigmoid(gate_logits.astype(jnp.float32)).reshape(B, S, N_KV, REP, 3)
    g = g.transpose(0, 2, 3, 1, 4).astype(q.dtype)
    o = g[..., 0:1] * o_c + g[..., 1:2] * o_s + g[..., 2:3] * o_w
    return o.transpose(0, 3, 1, 2, 4).reshape(B, S, D_NSA)


def short_conv_mixer(b_gate, c_gate, x_t, conv_w):
    return b_gate * causal_dwconv(c_gate * x_t, conv_w)


def split_points():
    sizes = [D_NSA] + [KV_W] * 6 + [N_GATES, D_CONV, D_CONV]
    pts, acc = [], 0
    for s in sizes:
        acc += s
        pts.append(acc)
    return pts


def setup_inputs(seed: int = 0) -> dict:
    key = jax.random.key(seed)
    ks = jax.random.split(key, 20)

    def nrm(k, shape, scale):
        return jax.random.normal(k, shape, jnp.float32) * scale

    return {
        'x': nrm(ks[0], (BATCH, SEQ, D_MODEL), 1.0),
        'norm_mix_pre': 1.0 + nrm(ks[1], (DEPTH, D_MODEL), 0.05),
        'norm_mix_post': 1.0 + nrm(ks[2], (DEPTH, D_MODEL), 0.05),
        'norm_ffn_pre': 1.0 + nrm(ks[3], (DEPTH, D_MODEL), 0.05),
        'norm_ffn_post': 1.0 + nrm(ks[4], (DEPTH, D_MODEL), 0.05),
        'w_in': nrm(ks[5], (DEPTH, D_MODEL, D_IN), D_MODEL ** -0.5),
        'pe_cmp_k': nrm(ks[6], (DEPTH, L_CMP, HEAD_DIM), 0.5),
        'pe_cmp_v': nrm(ks[7], (DEPTH, L_CMP, HEAD_DIM), 0.5),
        'w_cmp_k1': nrm(ks[8], (DEPTH, L_CMP * HEAD_DIM, CMP_HIDDEN), (L_CMP * HEAD_DIM) ** -0.5),
        'w_cmp_k2': nrm(ks[9], (DEPTH, CMP_HIDDEN, HEAD_DIM), CMP_HIDDEN ** -0.5),
        'w_cmp_v1': nrm(ks[10], (DEPTH, L_CMP * HEAD_DIM, CMP_HIDDEN), (L_CMP * HEAD_DIM) ** -0.5),
        'w_cmp_v2': nrm(ks[11], (DEPTH, CMP_HIDDEN, HEAD_DIM), CMP_HIDDEN ** -0.5),
        'rel_bias': nrm(ks[12], (N_HEADS, N_BUCKETS), 0.5),
        'conv_mix_w': nrm(ks[13], (DEPTH, CONV_W, D_CONV), CONV_W ** -0.5),
        'w_out': nrm(ks[14], (DEPTH, D_MIX, D_MODEL), D_MIX ** -0.5),
        'w_ffn_up': nrm(ks[15], (DEPTH, D_MODEL, 2 * D_FF), D_MODEL ** -0.5),
        'ffn_conv_w': nrm(ks[16], (DEPTH, CONV_W, 2 * D_FF), CONV_W ** -0.5),
        'w_ffn_down': nrm(ks[17], (DEPTH, D_FF, D_MODEL), D_FF ** -0.5),
    }


def reference(x, norm_mix_pre, norm_mix_post, norm_ffn_pre, norm_ffn_post, w_in,
              pe_cmp_k, pe_cmp_v, w_cmp_k1, w_cmp_k2, w_cmp_v1, w_cmp_v2, rel_bias,
              conv_mix_w, w_out, w_ffn_up, ffn_conv_w, w_ffn_down):
    B, S, _ = x.shape
    for l in range(DEPTH):
        h = rms_norm(x, norm_mix_pre[l])
        u = h @ w_in[l]
        q, k_c, v_c, k_s, v_s, k_w, v_w, gl, b_g, c_g, x_t = jnp.split(
            u, split_points(), axis=-1)
        kv = lambda a: a.reshape(B, S, N_KV, HEAD_DIM)
        o_nsa = nsa_attention(q.reshape(B, S, N_HEADS, HEAD_DIM),
                              kv(k_c), kv(v_c), kv(k_s), kv(v_s), kv(k_w), kv(v_w), gl,
                              pe_cmp_k[l], pe_cmp_v[l], w_cmp_k1[l], w_cmp_k2[l],
                              w_cmp_v1[l], w_cmp_v2[l], rel_bias)
        o_conv = short_conv_mixer(b_g, c_g, x_t, conv_mix_w[l])
        y = jnp.concatenate([o_nsa, o_conv], axis=-1) @ w_out[l]
        x = x + rms_norm(y, norm_mix_post[l])
        h = rms_norm(x, norm_ffn_pre[l])
        gu = causal_dwconv(h @ w_ffn_up[l], ffn_conv_w[l])
        g, up = jnp.split(gu, 2, axis=-1)
        y = (jax.nn.gelu(g, approximate=True) * up) @ w_ffn_down[l]
        x = x + rms_norm(y, norm_ffn_post[l])
    return x
```

```python
import functools
import math

import numpy as np
import jax
import jax.numpy as jnp
from jax import lax
from jax.experimental import pallas as pl
from jax.experimental.pallas import tpu as pltpu

D_MODEL = 1024
D_NSA = 512
D_CONV = 512
HEAD_DIM = 64
N_HEADS = 8
N_KV = 2
REP = 4
KV_W = 128
N_GATES = 24
L_CMP = 32
S_CMP = 16
L_SLC = 64
N_SEL = 16
WINDOW = 512
CMP_HIDDEN = 256
D_FF = 2816
N_BUCKETS = 32
MAX_DIST = 128
RMS_EPS = 1e-6
MASK_NEG = -1e30
FORCE_BONUS = 1e4

TQ = 256
TK = 256
N_CMP_PAD = 256
TM_IN = 512
TM_OUT = 512
TM_FFN = 512
FF_CHUNK = 256
VMEM_LIMIT = 56 * 1024 * 1024

_HI = lax.Precision.HIGHEST
_NT = (((1,), (1,)), ((), ()))


def _bucket_np(d):
    max_exact = N_BUCKETS // 2
    d = np.maximum(d, 0)
    df = np.maximum(d, 1).astype(np.float32)
    large = max_exact + (np.log(df / max_exact) / math.log(MAX_DIST / max_exact)
                         * (N_BUCKETS - max_exact)).astype(np.int32)
    return np.where(d < max_exact, d, np.minimum(large, N_BUCKETS - 1)).astype(np.int32)


def _bucket_or_mask(d):
    return np.where(d >= 0, _bucket_np(d), -1).astype(np.int32)


def _rms(x, g):
    return x * lax.rsqrt(jnp.mean(x * x, axis=-1, keepdims=True) + RMS_EPS) * g


def _gelu(x):
    return jax.nn.gelu(x, approximate=True)


def _bias_tables_kernel(rb_ref, idx_c_ref, idx_n_ref, tab_c_ref, tab_n_ref):
    h = pl.program_id(0)
    last = rb_ref[h, N_BUCKETS - 1]
    idx_c = idx_c_ref[...]
    idx_n = idx_n_ref[...]
    val_c = jnp.zeros(idx_c.shape, jnp.float32)
    val_n = jnp.zeros(idx_n.shape, jnp.float32)
    for b in range(N_BUCKETS - 1):
        delta = rb_ref[h, b] - last
        val_c = jnp.where(idx_c == b, delta, val_c)
        val_n = jnp.where(idx_n == b, delta, val_n)
    tab_c_ref[0] = val_c
    tab_n_ref[0] = jnp.where(idx_n < 0, MASK_NEG, val_n)


def _bias_tables(rel_bias):
    u = np.arange(512)[:, None] - 240
    row = np.arange(TQ)[None, :]
    d_c = row - S_CMP * u - (L_CMP - 1)
    idx_c = np.where(d_c >= 0, _bucket_np(d_c), N_BUCKETS - 1).astype(np.int32)
    key = np.arange(TK)[:, None]
    idx_n = np.stack([_bucket_or_mask(dd + row - key) for dd in (0, TK)])
    return pl.pallas_call(
        _bias_tables_kernel,
        grid=(N_HEADS,),
        in_specs=[pl.BlockSpec(memory_space=pltpu.SMEM),
                  pl.BlockSpec((512, TQ), lambda h: (0, 0)),
                  pl.BlockSpec((2, TK, TQ), lambda h: (0, 0, 0))],
        out_specs=[pl.BlockSpec((1, 512, TQ), lambda h: (h, 0, 0)),
                   pl.BlockSpec((1, 2, TK, TQ), lambda h: (h, 0, 0, 0))],
        out_shape=[jax.ShapeDtypeStruct((N_HEADS, 512, TQ), jnp.float32),
                   jax.ShapeDtypeStruct((N_HEADS, 2, TK, TQ), jnp.float32)],
        name="bias_tables",
    )(rel_bias, jnp.asarray(idx_c), jnp.asarray(idx_n))


def _in_proj_kernel(x_ref, g_ref, wq_hi_ref, wq_bf_ref, wkc_ref, wnat_ref, wvt_ref, wgt_ref,
                    cw_ref,
                    qft_ref, qt_ref, kc_ref, vc_ref, ks_ref, kw_ref, vst_ref, vwt_ref, gt_ref,
                    oconv_ref, carry_ref):
    i = pl.program_id(1)
    tm = x_ref.shape[1]
    h = _rms(x_ref[0], g_ref[...])
    hb = h.astype(jnp.bfloat16)

    qft_ref[0] = lax.dot_general(wq_hi_ref[...], h, _NT, precision=_HI,
                                 preferred_element_type=jnp.float32)
    kc_ref[0] = jnp.dot(h, wkc_ref[...], precision=_HI, preferred_element_type=jnp.float32)

    qt_ref[0] = lax.dot_general(wq_bf_ref[...], hb, _NT,
                                preferred_element_type=jnp.float32).astype(jnp.bfloat16)
    vt = lax.dot_general(wvt_ref[...], hb, _NT, preferred_element_type=jnp.float32)
    for t in range(tm // TK):
        vst_ref[0, t] = vt[0:KV_W, t * TK:(t + 1) * TK].astype(jnp.bfloat16)
        vwt_ref[0, t] = vt[KV_W:2 * KV_W, t * TK:(t + 1) * TK].astype(jnp.bfloat16)
    gl = lax.dot_general(wgt_ref[...], hb, _NT, preferred_element_type=jnp.float32)
    gt_ref[0] = jax.nn.sigmoid(gl)

    vc_ref[0] = jnp.dot(hb, wnat_ref[:, 0:128], preferred_element_type=jnp.float32)
    k_s = jnp.dot(hb, wnat_ref[:, 128:256], preferred_element_type=jnp.float32)
    blk = (i * tm + lax.broadcasted_iota(jnp.int32, (tm, KV_W), 0)) // L_SLC
    col = lax.broadcasted_iota(jnp.int32, (tm, KV_W), 1)
    onehot = jnp.where(blk == col, 1.0, 0.0)
    ks_ref[0, :, 0:KV_W] = k_s.astype(jnp.bfloat16)
    ks_ref[0, :, KV_W:2 * KV_W] = onehot.astype(jnp.bfloat16)
    kw_ref[0] = jnp.dot(hb, wnat_ref[:, 256:384],
                        preferred_element_type=jnp.float32).astype(jnp.bfloat16)

    bg = jnp.dot(hb, wnat_ref[:, 384:896], preferred_element_type=jnp.float32)
    cg = jnp.dot(hb, wnat_ref[:, 896:1408], preferred_element_type=jnp.float32)
    xt = jnp.dot(hb, wnat_ref[:, 1408:1920], preferred_element_type=jnp.float32)
    z = cg * xt

    @pl.when(i == 0)
    def _():
        carry_ref[...] = jnp.zeros_like(carry_ref)

    prev1 = carry_ref[7:8, :]
    prev2 = carry_ref[6:7, :]
    row = lax.broadcasted_iota(jnp.int32, z.shape, 0)
    z1 = jnp.where(row == 0, prev1, pltpu.roll(z, 1, 0))
    z2 = jnp.where(row == 0, prev2, jnp.where(row == 1, prev1, pltpu.roll(z, 2, 0)))
    y = cw_ref[0:1, :] * z2 + cw_ref[1:2, :] * z1 + cw_ref[2:3, :] * z
    oconv_ref[0] = (bg * y).astype(jnp.bfloat16)
    carry_ref[...] = z[tm - 8:tm, :]


def _in_proj(x, g_pre, w_in, conv_w):
    B, S, D = x.shape
    tm = TM_IN
    scale = HEAD_DIM ** -0.5
    wq = w_in[:, 0:512] * scale
    wq_hi = wq.T
    wq_bf = wq_hi.astype(jnp.bfloat16)
    wkc = w_in[:, 512:640]
    wnat = jnp.concatenate([w_in[:, 640:768], w_in[:, 768:896], w_in[:, 1024:1152],
                            w_in[:, 1304:2840]], axis=1).astype(jnp.bfloat16)
    wvt = jnp.concatenate([w_in[:, 896:1024], w_in[:, 1152:1280]], axis=1).T.astype(jnp.bfloat16)
    wgt = jnp.pad(w_in[:, 1280:1304], ((0, 0), (0, 8))).T.astype(jnp.bfloat16)
    nt = S // TK
    const = lambda b, i: (0, 0)
    outs = pl.pallas_call(
        _in_proj_kernel,
        grid=(B, S // tm),
        in_specs=[pl.BlockSpec((1, tm, D), lambda b, i: (b, i, 0)),
                  pl.BlockSpec((1, D), const),
                  pl.BlockSpec((512, D), const),
                  pl.BlockSpec((512, D), const),
                  pl.BlockSpec((D, 128), const),
                  pl.BlockSpec((D, 1920), const),
                  pl.BlockSpec((256, D), const),
                  pl.BlockSpec((32, D), const),
                  pl.BlockSpec((3, D_CONV), const)],
        out_specs=[pl.BlockSpec((1, 512, tm), lambda b, i: (b, 0, i)),
                   pl.BlockSpec((1, 512, tm), lambda b, i: (b, 0, i)),
                   pl.BlockSpec((1, tm, 128), lambda b, i: (b, i, 0)),
                   pl.BlockSpec((1, tm, 128), lambda b, i: (b, i, 0)),
                   pl.BlockSpec((1, tm, 256), lambda b, i: (b, i, 0)),
                   pl.BlockSpec((1, tm, 128), lambda b, i: (b, i, 0)),
                   pl.BlockSpec((1, tm // TK, 128, TK), lambda b, i: (b, i, 0, 0)),
                   pl.BlockSpec((1, tm // TK, 128, TK), lambda b, i: (b, i, 0, 0)),
                   pl.BlockSpec((1, 32, tm), lambda b, i: (b, 0, i)),
                   pl.BlockSpec((1, tm, D_CONV), lambda b, i: (b, i, 0))],
        out_shape=[jax.ShapeDtypeStruct((B, 512, S), jnp.float32),
                   jax.ShapeDtypeStruct((B, 512, S), jnp.bfloat16),
                   jax.ShapeDtypeStruct((B, S, 128), jnp.float32),
                   jax.ShapeDtypeStruct((B, S, 128), jnp.float32),
                   jax.ShapeDtypeStruct((B, S, 256), jnp.bfloat16),
                   jax.ShapeDtypeStruct((B, S, 128), jnp.bfloat16),
                   jax.ShapeDtypeStruct((B, nt, 128, TK), jnp.bfloat16),
                   jax.ShapeDtypeStruct((B, nt, 128, TK), jnp.bfloat16),
                   jax.ShapeDtypeStruct((B, 32, S), jnp.float32),
                   jax.ShapeDtypeStruct((B, S, D_CONV), jnp.bfloat16)],
        scratch_shapes=[pltpu.VMEM((8, D_CONV), jnp.float32)],
        compiler_params=pltpu.CompilerParams(
            dimension_semantics=("arbitrary", "arbitrary"), vmem_limit_bytes=VMEM_LIMIT),
        name="in_proj",
    )(x, g_pre.reshape(1, D), wq_hi, wq_bf, wkc, wnat, wvt, wgt, conv_w)
    return outs


def _compress_kernel(ck_ref, cv_ref, pek_ref, pev_ref, wk1_ref, wk2_ref, wv1_ref, wv2t_ref,
                     kc_ref, vct_ref):
    half = S_CMP * HEAD_DIM
    ck = ck_ref[0, 0]
    a = jnp.dot(ck + pek_ref[0:1, :], wk1_ref[0:half, :], precision=_HI,
                preferred_element_type=jnp.float32)
    b = jnp.dot(ck + pek_ref[1:2, :], wk1_ref[half:2 * half, :], precision=_HI,
                preferred_element_type=jnp.float32)
    hid = _gelu(a + pltpu.roll(b, N_CMP_PAD - 1, 0))
    kc_ref[0, 0] = jnp.dot(hid, wk2_ref[...], precision=_HI, preferred_element_type=jnp.float32)

    cv = cv_ref[0, 0]
    av = jnp.dot((cv + pev_ref[0:1, :]).astype(jnp.bfloat16), wv1_ref[0:half, :],
                 preferred_element_type=jnp.float32)
    bv = jnp.dot((cv + pev_ref[1:2, :]).astype(jnp.bfloat16), wv1_ref[half:2 * half, :],
                 preferred_element_type=jnp.float32)
    hv = _gelu(av + pltpu.roll(bv, N_CMP_PAD - 1, 0)).astype(jnp.bfloat16)
    vct_ref[0, 0] = lax.dot_general(wv2t_ref[...], hv, _NT,
                                    preferred_element_type=jnp.float32)


def _compress(kc_in, vc_in, pe_k, pe_v, wk1, wk2, wv1, wv2):
    B, S, _ = kc_in.shape
    n_str = S // S_CMP
    half = S_CMP * HEAD_DIM

    def strides(a):
        return a.reshape(B, n_str, S_CMP, N_KV, HEAD_DIM).transpose(0, 3, 1, 2, 4).reshape(
            B, N_KV, n_str, half)

    const2 = lambda b, g: (0, 0)
    return pl.pallas_call(
        _compress_kernel,
        grid=(B, N_KV),
        in_specs=[pl.BlockSpec((1, 1, n_str, half), lambda b, g: (b, g, 0, 0)),
                  pl.BlockSpec((1, 1, n_str, half), lambda b, g: (b, g, 0, 0)),
                  pl.BlockSpec((2, half), const2),
                  pl.BlockSpec((2, half), const2),
                  pl.BlockSpec((2 * half, CMP_HIDDEN), const2),
                  pl.BlockSpec((CMP_HIDDEN, HEAD_DIM), const2),
                  pl.BlockSpec((2 * half, CMP_HIDDEN), const2),
                  pl.BlockSpec((HEAD_DIM, CMP_HIDDEN), const2)],
        out_specs=[pl.BlockSpec((1, 1, n_str, HEAD_DIM), lambda b, g: (b, g, 0, 0)),
                   pl.BlockSpec((1, 1, HEAD_DIM, n_str), lambda b, g: (b, g, 0, 0))],
        out_shape=[jax.ShapeDtypeStruct((B, N_KV, n_str, HEAD_DIM), jnp.float32),
                   jax.ShapeDtypeStruct((B, N_KV, HEAD_DIM, n_str), jnp.float32)],
        compiler_params=pltpu.CompilerParams(
            dimension_semantics=("arbitrary", "arbitrary"), vmem_limit_bytes=VMEM_LIMIT),
        name="compress",
    )(strides(kc_in), strides(vc_in), pe_k.reshape(2, half), pe_v.reshape(2, half),
      wk1, wk2, wv1.astype(jnp.bfloat16), wv2.T.astype(jnp.bfloat16))


def _cmp_select_kernel(qt_ref, kc_ref, vct_ref, tab_ref, ovl_ref, oct_ref, sel_ref):
    i = pl.program_id(2)
    kc = kc_ref[0, 0]
    vct = vct_ref[0, 0]
    n_idx = lax.broadcasted_iota(jnp.int32, (N_CMP_PAD, TQ), 0)
    t_idx = i * TQ + lax.broadcasted_iota(jnp.int32, (N_CMP_PAD, TQ), 1)
    valid = t_idx >= n_idx * S_CMP + (L_CMP - 1)
    off = pl.multiple_of(240 - 16 * i, 16)
    p_sum = jnp.zeros((N_CMP_PAD, TQ), jnp.float32)
    for r in range(REP):
        qh = qt_ref[0, r * HEAD_DIM:(r + 1) * HEAD_DIM, :]
        s = jnp.dot(kc, qh, precision=_HI, preferred_element_type=jnp.float32)
        s = s + tab_ref[r, pl.ds(off, N_CMP_PAD), :]
        s = jnp.where(valid, s, MASK_NEG)
        m = jnp.max(s, axis=0, keepdims=True)
        p = jnp.where(valid, jnp.exp(s - m), 0.0)
        l = jnp.sum(p, axis=0, keepdims=True)
        p = p / jnp.where(l > 0.0, l, 1.0)
        oct_ref[0, r * HEAD_DIM:(r + 1) * HEAD_DIM, :] = jnp.dot(
            vct, p, preferred_element_type=jnp.float32)
        p_sum = p_sum + p
    imp = jnp.dot(ovl_ref[...], p_sum, precision=_HI, preferred_element_type=jnp.float32)
    n_blk = imp.shape[0]
    j_idx = lax.broadcasted_iota(jnp.int32, (n_blk, TQ), 0)
    cur = (i * TQ + lax.broadcasted_iota(jnp.int32, (n_blk, TQ), 1)) // L_SLC
    forced = (j_idx == 0) | (j_idx == cur) | (j_idx == cur - 1)
    allowed = j_idx <= cur
    score = jnp.where(allowed, imp + jnp.where(forced, FORCE_BONUS, 0.0), MASK_NEG)
    rank = jnp.zeros((n_blk, TQ), jnp.float32)
    for jp in range(n_blk):
        sj = score[jp:jp + 1, :]
        before = (sj > score) | ((sj == score) & (j_idx > jp))
        rank = rank + jnp.where(before, 1.0, 0.0)
    sel = allowed & (rank < float(N_SEL))
    neg = jnp.where(sel, 0.0, MASK_NEG)
    sel_ref[0, 0, 0:n_blk, :] = neg.astype(jnp.bfloat16)
    sel_ref[0, 0, n_blk:2 * n_blk, :] = jnp.zeros((n_blk, TQ), jnp.bfloat16)


def _overlap_t(S):
    n_slc = S // L_SLC
    c_start = np.arange(N_CMP_PAD) * S_CMP
    s_start = np.arange(n_slc) * L_SLC
    ov = np.clip(np.minimum(c_start[:, None] + L_CMP, s_start[None, :] + L_SLC)
                 - np.maximum(c_start[:, None], s_start[None, :]), 0, None).astype(np.float32) / L_CMP
    ov[(S - L_CMP) // S_CMP + 1:, :] = 0.0
    return jnp.asarray(ov.T)


def _cmp_select(qft, kc, vct, tab_c):
    B, _, S = qft.shape
    ni = S // TQ
    return pl.pallas_call(
        _cmp_select_kernel,
        grid=(B, N_KV, ni),
        in_specs=[pl.BlockSpec((1, REP * HEAD_DIM, TQ), lambda b, g, i: (b, g, i)),
                  pl.BlockSpec((1, 1, N_CMP_PAD, HEAD_DIM), lambda b, g, i: (b, g, 0, 0)),
                  pl.BlockSpec((1, 1, HEAD_DIM, N_CMP_PAD), lambda b, g, i: (b, g, 0, 0)),
                  pl.BlockSpec((REP, 512, TQ), lambda b, g, i: (g, 0, 0)),
                  pl.BlockSpec((S // L_SLC, N_CMP_PAD), lambda b, g, i: (0, 0))],
        out_specs=[pl.BlockSpec((1, REP * HEAD_DIM, TQ), lambda b, g, i: (b, g, i)),
                   pl.BlockSpec((1, 1, 128, TQ), lambda b, g, i: (b, g, 0, i))],
        out_shape=[jax.ShapeDtypeStruct((B, D_NSA, S), jnp.float32),
                   jax.ShapeDtypeStruct((B, N_KV, 128, S), jnp.bfloat16)],
        compiler_params=pltpu.CompilerParams(
            dimension_semantics=("arbitrary", "arbitrary", "arbitrary"),
            vmem_limit_bytes=VMEM_LIMIT),
        name="cmp_select",
    )(qft, kc, vct, tab_c, _overlap_t(S))


def _softmax_step(state, s, v_t):
    m, l, acc = state
    m_new = jnp.maximum(m, jnp.max(s, axis=0, keepdims=True))
    alpha = jnp.exp(m - m_new)
    p = jnp.exp(s - m_new)
    l = alpha * l + jnp.sum(p, axis=0, keepdims=True)
    acc = alpha * acc + jnp.dot(v_t, p.astype(jnp.bfloat16), preferred_element_type=jnp.float32)
    return m_new, l, acc


def _sparse_attn_kernel(qt_ref, sel_ref, ks_ref, kw_ref, vst_ref, vwt_ref, near_ref, edge_ref,
                        gt_ref, oct_ref, o_ref):
    i = pl.program_id(1)
    zeros64 = jnp.zeros((HEAD_DIM, TQ), jnp.bfloat16)
    off1 = jnp.where(i >= 1, 0.0, MASK_NEG)
    off2 = jnp.where(i >= 2, 0.0, MASK_NEG)
    j1 = jnp.maximum(i - 1, 0)
    j2 = jnp.maximum(i - 2, 0)
    init = (jnp.full((1, TQ), MASK_NEG, jnp.float32), jnp.zeros((1, TQ), jnp.float32),
            jnp.zeros((HEAD_DIM, TQ), jnp.float32))
    for pair in range(N_HEADS // 2):
        outs = []
        for hh in range(2):
            h = 2 * pair + hh
            g = h // REP
            qh = qt_ref[0, h * HEAD_DIM:(h + 1) * HEAD_DIM, :]
            q_w = jnp.concatenate([qh, zeros64] if g == 0 else [zeros64, qh], axis=0)
            q_s = jnp.concatenate([q_w, sel_ref[0, g]], axis=0)
            vrow = slice(g * HEAD_DIM, (g + 1) * HEAD_DIM)

            def logits_s(j):
                k = ks_ref[0, pl.ds(pl.multiple_of(j * TK, TK), TK), :]
                return jnp.dot(k, q_s, preferred_element_type=jnp.float32)

            def logits_w(j):
                k = kw_ref[0, pl.ds(pl.multiple_of(j * TK, TK), TK), :]
                return jnp.dot(k, q_w, preferred_element_type=jnp.float32)

            st = _softmax_step(init, logits_s(i) + near_ref[h, 0], vst_ref[0, i, vrow, :])
            st = _softmax_step(st, logits_s(j1) + (near_ref[h, 1] + off1), vst_ref[0, j1, vrow, :])

            def far_body(j, st):
                return _softmax_step(st, logits_s(j), vst_ref[0, j, vrow, :])

            m, l, acc = lax.fori_loop(0, j1, far_body, st)
            o_s = acc / l
            st = _softmax_step(init, logits_w(i) + near_ref[h, 0], vwt_ref[0, i, vrow, :])
            st = _softmax_step(st, logits_w(j1) + (near_ref[h, 1] + off1), vwt_ref[0, j1, vrow, :])
            m, l, acc = _softmax_step(st, logits_w(j2) + (edge_ref[...] + off2),
                                      vwt_ref[0, j2, vrow, :])
            o_w = acc / l
            o_c = oct_ref[0, h * HEAD_DIM:(h + 1) * HEAD_DIM, :]
            o = (gt_ref[0, 3 * h:3 * h + 1, :] * o_c + gt_ref[0, 3 * h + 1:3 * h + 2, :] * o_s
                 + gt_ref[0, 3 * h + 2:3 * h + 3, :] * o_w)
            outs.append(o)
        o_pair = jnp.concatenate(outs, axis=0)
        o_ref[0, :, pair * 128:(pair + 1) * 128] = o_pair.T.astype(jnp.bfloat16)


def _sparse_attn(qt, sel, ks, kw, vst, vwt, near, gt, oct):
    B, _, S = qt.shape
    ni = S // TQ
    nt = S // TK
    key = np.arange(TK)[:, None]
    row = np.arange(TQ)[None, :]
    edge = jnp.asarray(np.where(2 * TK + row - key < WINDOW, 0.0, MASK_NEG).astype(np.float32))
    return pl.pallas_call(
        _sparse_attn_kernel,
        grid=(B, ni),
        in_specs=[pl.BlockSpec((1, D_NSA, TQ), lambda b, i: (b, 0, i)),
                  pl.BlockSpec((1, N_KV, 128, TQ), lambda b, i: (b, 0, 0, i)),
                  pl.BlockSpec((1, S, 256), lambda b, i: (b, 0, 0)),
                  pl.BlockSpec((1, S, 128), lambda b, i: (b, 0, 0)),
                  pl.BlockSpec((1, nt, 128, TK), lambda b, i: (b, 0, 0, 0)),
                  pl.BlockSpec((1, nt, 128, TK), lambda b, i: (b, 0, 0, 0)),
                  pl.BlockSpec((N_HEADS, 2, TK, TQ), lambda b, i: (0, 0, 0, 0)),
                  pl.BlockSpec((TK, TQ), lambda b, i: (0, 0)),
                  pl.BlockSpec((1, 32, TQ), lambda b, i: (b, 0, i)),
                  pl.BlockSpec((1, D_NSA, TQ), lambda b, i: (b, 0, i))],
        out_specs=pl.BlockSpec((1, TQ, D_NSA), lambda b, i: (b, i, 0)),
        out_shape=jax.ShapeDtypeStruct((B, S, D_NSA), jnp.bfloat16),
        compiler_params=pltpu.CompilerParams(
            dimension_semantics=("arbitrary", "arbitrary"), vmem_limit_bytes=VMEM_LIMIT),
        name="sparse_attn",
    )(qt, sel, ks, kw, vst, vwt, near, edge, gt, oct)


def _out_proj_kernel(on_ref, oc_ref, x_ref, w_ref, gpost_ref, gffn_ref, x1_ref, h2_ref):
    y = jnp.dot(on_ref[...], w_ref[0:D_NSA, :], preferred_element_type=jnp.float32)
    y = y + jnp.dot(oc_ref[...], w_ref[D_NSA:D_NSA + D_CONV, :], preferred_element_type=jnp.float32)
    x1 = x_ref[...] + _rms(y, gpost_ref[...])
    x1_ref[...] = x1
    h2_ref[...] = _rms(x1, gffn_ref[...]).astype(jnp.bfloat16)


def _out_proj(o_nsa, o_conv, x, w_out, g_post, g_ffn):
    T, D = x.shape
    tm = TM_OUT
    const = lambda i: (0, 0)
    return pl.pallas_call(
        _out_proj_kernel,
        grid=(T // tm,),
        in_specs=[pl.BlockSpec((tm, D_NSA), lambda i: (i, 0)),
                  pl.BlockSpec((tm, D_CONV), lambda i: (i, 0)),
                  pl.BlockSpec((tm, D), lambda i: (i, 0)),
                  pl.BlockSpec((D, D), const),
                  pl.BlockSpec((1, D), const),
                  pl.BlockSpec((1, D), const)],
        out_specs=[pl.BlockSpec((tm, D), lambda i: (i, 0)),
                   pl.BlockSpec((tm, D), lambda i: (i, 0))],
        out_shape=[jax.ShapeDtypeStruct((T, D), jnp.float32),
                   jax.ShapeDtypeStruct((T, D), jnp.bfloat16)],
        compiler_params=pltpu.CompilerParams(
            dimension_semantics=("arbitrary",), vmem_limit_bytes=VMEM_LIMIT),
        name="out_proj",
    )(o_nsa, o_conv, x, w_out.astype(jnp.bfloat16), g_post.reshape(1, D), g_ffn.reshape(1, D))


def _conv_ffn_kernel(h_ref, x1_ref, wup_ref, cw_ref, wdn_ref, g_ref, o_ref, carry_ref, act_ref):
    i = pl.program_id(1)
    tm = h_ref.shape[1]
    hb = h_ref[0]

    @pl.when(i == 0)
    def _():
        carry_ref[...] = jnp.zeros_like(carry_ref)

    row = lax.broadcasted_iota(jnp.int32, (tm, FF_CHUNK), 0)
    is0 = row == 0
    is1 = row == 1

    def conv(z, c0):
        cols = slice(c0, c0 + FF_CHUNK)
        prev1 = carry_ref[7:8, cols]
        prev2 = carry_ref[6:7, cols]
        z1 = jnp.where(is0, prev1, pltpu.roll(z, 1, 0))
        z2 = jnp.where(is0, prev2, jnp.where(is1, prev1, pltpu.roll(z, 2, 0)))
        carry_ref[:, cols] = z[tm - 8:tm, :]
        return cw_ref[0:1, cols] * z2 + cw_ref[1:2, cols] * z1 + cw_ref[2:3, cols] * z

    for c in range(D_FF // FF_CHUNK):
        g0 = c * FF_CHUNK
        u0 = D_FF + c * FF_CHUNK
        zg = jnp.dot(hb, wup_ref[:, g0:g0 + FF_CHUNK], preferred_element_type=jnp.float32)
        zu = jnp.dot(hb, wup_ref[:, u0:u0 + FF_CHUNK], preferred_element_type=jnp.float32)
        act = _gelu(conv(zg, g0)) * conv(zu, u0)
        act_ref[:, g0:g0 + FF_CHUNK] = act.astype(jnp.bfloat16)
    y = jnp.dot(act_ref[...], wdn_ref[...], preferred_element_type=jnp.float32)
    o_ref[0] = x1_ref[0] + _rms(y, g_ref[...])


def _conv_ffn(h2, x1, w_up, conv_w, w_down, g_post):
    B, S, D = x1.shape
    tm = TM_FFN
    const = lambda b, i: (0, 0)
    single = pl.Buffered(1)
    return pl.pallas_call(
        _conv_ffn_kernel,
        grid=(B, S // tm),
        in_specs=[pl.BlockSpec((1, tm, D), lambda b, i: (b, i, 0)),
                  pl.BlockSpec((1, tm, D), lambda b, i: (b, i, 0)),
                  pl.BlockSpec((D, 2 * D_FF), const, pipeline_mode=single),
                  pl.BlockSpec((3, 2 * D_FF), const, pipeline_mode=single),
                  pl.BlockSpec((D_FF, D), const, pipeline_mode=single),
                  pl.BlockSpec((1, D), const, pipeline_mode=single)],
        out_specs=pl.BlockSpec((1, tm, D), lambda b, i: (b, i, 0)),
        out_shape=jax.ShapeDtypeStruct((B, S, D), jnp.float32),
        scratch_shapes=[pltpu.VMEM((8, 2 * D_FF), jnp.float32),
                        pltpu.VMEM((tm, D_FF), jnp.bfloat16)],
        compiler_params=pltpu.CompilerParams(
            dimension_semantics=("arbitrary", "arbitrary"), vmem_limit_bytes=VMEM_LIMIT),
        name="conv_ffn",
    )(h2, x1, w_up.astype(jnp.bfloat16), conv_w, w_down.astype(jnp.bfloat16), g_post.reshape(1, D))


def kernel(x, norm_mix_pre, norm_mix_post, norm_ffn_pre, norm_ffn_post, w_in, pe_cmp_k, pe_cmp_v,
           w_cmp_k1, w_cmp_k2, w_cmp_v1, w_cmp_v2, rel_bias, conv_mix_w, w_out, w_ffn_up,
           ffn_conv_w, w_ffn_down):
    B, S, D = x.shape
    assert (S, D) == (4096, D_MODEL) and norm_mix_pre.shape[0] == 1
    tab_c, near = _bias_tables(rel_bias)
    for l in range(norm_mix_pre.shape[0]):
        qft, qt, kc_in, vc_in, ks, kw, vst, vwt, gt, o_conv = _in_proj(
            x, norm_mix_pre[l], w_in[l], conv_mix_w[l])
        kc, vct = _compress(kc_in, vc_in, pe_cmp_k[l], pe_cmp_v[l], w_cmp_k1[l], w_cmp_k2[l],
                            w_cmp_v1[l], w_cmp_v2[l])
        oct, sel = _cmp_select(qft, kc, vct, tab_c)
        o_nsa = _sparse_attn(qt, sel, ks, kw, vst, vwt, near, gt, oct)
        x1, h2 = _out_proj(o_nsa.reshape(B * S, D_NSA), o_conv.reshape(B * S, D_CONV),
                           x.reshape(B * S, D), w_out[l], norm_mix_post[l], norm_ffn_pre[l])
        x = _conv_ffn(h2.reshape(B, S, D), x1.reshape(B, S, D), w_ffn_up[l], ffn_conv_w[l],
                      w_ffn_down[l], norm_ffn_post[l])
    return x
```

```python
import functools
import math

import numpy as np
import jax
import jax.numpy as jnp
from jax import lax
from jax.experimental import pallas as pl
from jax.experimental.pallas import tpu as pltpu

D_MODEL = 1024
D_NSA = 512
D_CONV = 512
HEAD_DIM = 64
N_HEADS = 8
N_KV = 2
REP = 4
KV_W = 128
N_GATES = 24
L_CMP = 32
S_CMP = 16
L_SLC = 64
N_SEL = 16
WINDOW = 512
CMP_HIDDEN = 256
D_FF = 2816
N_BUCKETS = 32
MAX_DIST = 128
RMS_EPS = 1e-6
MASK_NEG = -1e30
FORCE_BONUS = 1e4
LOG2E = math.log2(math.e)

TQ = 256
TK = 256
V_ROWS = 80
N_CMP_PAD = 256
TM_IN = 512
TM_OUT = 512
TM_FFN = 512
FF_CHUNK = 256
VMEM_LIMIT = 56 * 1024 * 1024

_HI = lax.Precision.HIGHEST
_NT = (((1,), (1,)), ((), ()))


def _bucket_np(d):
    max_exact = N_BUCKETS // 2
    d = np.maximum(d, 0)
    df = np.maximum(d, 1).astype(np.float32)
    large = max_exact + (np.log(df / max_exact) / math.log(MAX_DIST / max_exact)
                         * (N_BUCKETS - max_exact)).astype(np.int32)
    return np.where(d < max_exact, d, np.minimum(large, N_BUCKETS - 1)).astype(np.int32)


def _bucket_or_mask(d):
    return np.where(d >= 0, _bucket_np(d), -1).astype(np.int32)


def _rms(x, g):
    return x * lax.rsqrt(jnp.mean(x * x, axis=-1, keepdims=True) + RMS_EPS) * g


def _gelu(x):
    return jax.nn.gelu(x, approximate=True)


def _bias_tables_kernel(rb_ref, idx_c_ref, idx_n_ref, tab_c_ref, tab_n_ref):
    h = pl.program_id(0)
    last = rb_ref[h, N_BUCKETS - 1]
    idx_c = idx_c_ref[...]
    idx_n = idx_n_ref[...]
    val_c = jnp.zeros(idx_c.shape, jnp.float32)
    val_n = jnp.zeros(idx_n.shape, jnp.float32)
    for b in range(N_BUCKETS - 1):
        delta = rb_ref[h, b] - last
        val_c = jnp.where(idx_c == b, delta, val_c)
        val_n = jnp.where(idx_n == b, delta, val_n)
    tab_c_ref[0] = val_c
    tab_n_ref[0] = jnp.where(idx_n < 0, MASK_NEG, val_n * LOG2E)


def _bias_tables(rel_bias):
    u = np.arange(512)[:, None] - 240
    row = np.arange(TQ)[None, :]
    d_c = row - S_CMP * u - (L_CMP - 1)
    idx_c = np.where(d_c >= 0, _bucket_np(d_c), N_BUCKETS - 1).astype(np.int32)
    key = np.arange(TK)[:, None]
    idx_n = np.stack([_bucket_or_mask(dd + row - key) for dd in (0, TK)])
    return pl.pallas_call(
        _bias_tables_kernel,
        grid=(N_HEADS,),
        in_specs=[pl.BlockSpec(memory_space=pltpu.SMEM),
                  pl.BlockSpec((512, TQ), lambda h: (0, 0)),
                  pl.BlockSpec((2, TK, TQ), lambda h: (0, 0, 0))],
        out_specs=[pl.BlockSpec((1, 512, TQ), lambda h: (h, 0, 0)),
                   pl.BlockSpec((1, 2, TK, TQ), lambda h: (h, 0, 0, 0))],
        out_shape=[jax.ShapeDtypeStruct((N_HEADS, 512, TQ), jnp.float32),
                   jax.ShapeDtypeStruct((N_HEADS, 2, TK, TQ), jnp.float32)],
        name="bias_tables",
    )(rel_bias, jnp.asarray(idx_c), jnp.asarray(idx_n))


def _in_proj_kernel(x_ref, g_ref, wq_hi_ref, wq_bf_ref, wkc_ref, wnat_ref, wvt_ref, wgt_ref,
                    cw_ref,
                    qft_ref, qt_ref, kc_ref, vc_ref, ks_ref, kw_ref, vst_ref, vwt_ref, gt_ref,
                    oconv_ref, carry_ref):
    i = pl.program_id(1)
    tm = x_ref.shape[1]
    h = _rms(x_ref[0], g_ref[...])
    hb = h.astype(jnp.bfloat16)

    qft_ref[0] = lax.dot_general(wq_hi_ref[...], h, _NT, precision=_HI,
                                 preferred_element_type=jnp.float32)
    kc_ref[0] = jnp.dot(h, wkc_ref[...], precision=_HI, preferred_element_type=jnp.float32)

    qt_ref[0] = lax.dot_general(wq_bf_ref[...], hb, _NT,
                                preferred_element_type=jnp.float32).astype(jnp.bfloat16)
    vt = lax.dot_general(wvt_ref[...], hb, _NT, preferred_element_type=jnp.float32)
    ones = jnp.ones((V_ROWS - HEAD_DIM, TK), jnp.bfloat16)
    for t in range(tm // TK):
        for g in range(N_KV):
            for out_ref, r0 in ((vst_ref, g * HEAD_DIM), (vwt_ref, KV_W + g * HEAD_DIM)):
                out_ref[0, t, g, 0:HEAD_DIM, :] = vt[r0:r0 + HEAD_DIM,
                                                      t * TK:(t + 1) * TK].astype(jnp.bfloat16)
                out_ref[0, t, g, HEAD_DIM:V_ROWS, :] = ones
    gl = lax.dot_general(wgt_ref[...], hb, _NT, preferred_element_type=jnp.float32)
    gt_ref[0] = jax.nn.sigmoid(gl)

    vc_ref[0] = jnp.dot(hb, wnat_ref[:, 0:128], preferred_element_type=jnp.float32)
    k_s = jnp.dot(hb, wnat_ref[:, 128:256], preferred_element_type=jnp.float32)
    blk = (i * tm + lax.broadcasted_iota(jnp.int32, (tm, KV_W), 0)) // L_SLC
    col = lax.broadcasted_iota(jnp.int32, (tm, KV_W), 1)
    onehot = jnp.where(blk == col, 1.0, 0.0)
    ks_ref[0, :, 0:KV_W] = k_s.astype(jnp.bfloat16)
    ks_ref[0, :, KV_W:2 * KV_W] = onehot.astype(jnp.bfloat16)
    kw_ref[0] = jnp.dot(hb, wnat_ref[:, 256:384],
                        preferred_element_type=jnp.float32).astype(jnp.bfloat16)

    bg = jnp.dot(hb, wnat_ref[:, 384:896], preferred_element_type=jnp.float32)
    cg = jnp.dot(hb, wnat_ref[:, 896:1408], preferred_element_type=jnp.float32)
    xt = jnp.dot(hb, wnat_ref[:, 1408:1920], preferred_element_type=jnp.float32)
    z = cg * xt

    @pl.when(i == 0)
    def _():
        carry_ref[...] = jnp.zeros_like(carry_ref)

    prev1 = carry_ref[7:8, :]
    prev2 = carry_ref[6:7, :]
    row = lax.broadcasted_iota(jnp.int32, z.shape, 0)
    z1 = jnp.where(row == 0, prev1, pltpu.roll(z, 1, 0))
    z2 = jnp.where(row == 0, prev2, jnp.where(row == 1, prev1, pltpu.roll(z, 2, 0)))
    y = cw_ref[0:1, :] * z2 + cw_ref[1:2, :] * z1 + cw_ref[2:3, :] * z
    oconv_ref[0] = (bg * y).astype(jnp.bfloat16)
    carry_ref[...] = z[tm - 8:tm, :]


def _in_proj(x, g_pre, w_in, conv_w):
    B, S, D = x.shape
    tm = TM_IN
    scale = HEAD_DIM ** -0.5
    wq = w_in[:, 0:512] * scale
    wq_hi = wq.T
    wq_bf = (wq_hi * LOG2E).astype(jnp.bfloat16)
    wkc = w_in[:, 512:640]
    wnat = jnp.concatenate([w_in[:, 640:768], w_in[:, 768:896], w_in[:, 1024:1152],
                            w_in[:, 1304:2840]], axis=1).astype(jnp.bfloat16)
    wvt = jnp.concatenate([w_in[:, 896:1024], w_in[:, 1152:1280]], axis=1).T.astype(jnp.bfloat16)
    wgt = jnp.pad(w_in[:, 1280:1304], ((0, 0), (0, 8))).T.astype(jnp.bfloat16)
    nt = S // TK
    const = lambda b, i: (0, 0)
    outs = pl.pallas_call(
        _in_proj_kernel,
        grid=(B, S // tm),
        in_specs=[pl.BlockSpec((1, tm, D), lambda b, i: (b, i, 0)),
                  pl.BlockSpec((1, D), const),
                  pl.BlockSpec((512, D), const),
                  pl.BlockSpec((512, D), const),
                  pl.BlockSpec((D, 128), const),
                  pl.BlockSpec((D, 1920), const),
                  pl.BlockSpec((256, D), const),
                  pl.BlockSpec((32, D), const),
                  pl.BlockSpec((3, D_CONV), const)],
        out_specs=[pl.BlockSpec((1, 512, tm), lambda b, i: (b, 0, i)),
                   pl.BlockSpec((1, 512, tm), lambda b, i: (b, 0, i)),
                   pl.BlockSpec((1, tm, 128), lambda b, i: (b, i, 0)),
                   pl.BlockSpec((1, tm, 128), lambda b, i: (b, i, 0)),
                   pl.BlockSpec((1, tm, 256), lambda b, i: (b, i, 0)),
                   pl.BlockSpec((1, tm, 128), lambda b, i: (b, i, 0)),
                   pl.BlockSpec((1, tm // TK, N_KV, V_ROWS, TK), lambda b, i: (b, i, 0, 0, 0)),
                   pl.BlockSpec((1, tm // TK, N_KV, V_ROWS, TK), lambda b, i: (b, i, 0, 0, 0)),
                   pl.BlockSpec((1, 32, tm), lambda b, i: (b, 0, i)),
                   pl.BlockSpec((1, tm, D_CONV), lambda b, i: (b, i, 0))],
        out_shape=[jax.ShapeDtypeStruct((B, 512, S), jnp.float32),
                   jax.ShapeDtypeStruct((B, 512, S), jnp.bfloat16),
                   jax.ShapeDtypeStruct((B, S, 128), jnp.float32),
                   jax.ShapeDtypeStruct((B, S, 128), jnp.float32),
                   jax.ShapeDtypeStruct((B, S, 256), jnp.bfloat16),
                   jax.ShapeDtypeStruct((B, S, 128), jnp.bfloat16),
                   jax.ShapeDtypeStruct((B, nt, N_KV, V_ROWS, TK), jnp.bfloat16),
                   jax.ShapeDtypeStruct((B, nt, N_KV, V_ROWS, TK), jnp.bfloat16),
                   jax.ShapeDtypeStruct((B, 32, S), jnp.float32),
                   jax.ShapeDtypeStruct((B, S, D_CONV), jnp.bfloat16)],
        scratch_shapes=[pltpu.VMEM((8, D_CONV), jnp.float32)],
        compiler_params=pltpu.CompilerParams(
            dimension_semantics=("arbitrary", "arbitrary"), vmem_limit_bytes=VMEM_LIMIT),
        name="in_proj",
    )(x, g_pre.reshape(1, D), wq_hi, wq_bf, wkc, wnat, wvt, wgt, conv_w)
    return outs


def _compress_kernel(ck_ref, cv_ref, pek_ref, pev_ref, wk1_ref, wk2_ref, wv1_ref, wv2t_ref,
                     kc_ref, vct_ref):
    half = S_CMP * HEAD_DIM
    ck = ck_ref[0, 0]
    a = jnp.dot(ck + pek_ref[0:1, :], wk1_ref[0:half, :], precision=_HI,
                preferred_element_type=jnp.float32)
    b = jnp.dot(ck + pek_ref[1:2, :], wk1_ref[half:2 * half, :], precision=_HI,
                preferred_element_type=jnp.float32)
    hid = _gelu(a + pltpu.roll(b, N_CMP_PAD - 1, 0))
    kc_ref[0, 0] = jnp.dot(hid, wk2_ref[...], precision=_HI, preferred_element_type=jnp.float32)

    cv = cv_ref[0, 0]
    av = jnp.dot((cv + pev_ref[0:1, :]).astype(jnp.bfloat16), wv1_ref[0:half, :],
                 preferred_element_type=jnp.float32)
    bv = jnp.dot((cv + pev_ref[1:2, :]).astype(jnp.bfloat16), wv1_ref[half:2 * half, :],
                 preferred_element_type=jnp.float32)
    hv = _gelu(av + pltpu.roll(bv, N_CMP_PAD - 1, 0)).astype(jnp.bfloat16)
    vct_ref[0, 0] = lax.dot_general(wv2t_ref[...], hv, _NT,
                                    preferred_element_type=jnp.float32)


def _compress(kc_in, vc_in, pe_k, pe_v, wk1, wk2, wv1, wv2):
    B, S, _ = kc_in.shape
    n_str = S // S_CMP
    half = S_CMP * HEAD_DIM

    def strides(a):
        return a.reshape(B, n_str, S_CMP, N_KV, HEAD_DIM).transpose(0, 3, 1, 2, 4).reshape(
            B, N_KV, n_str, half)

    const2 = lambda b, g: (0, 0)
    return pl.pallas_call(
        _compress_kernel,
        grid=(B, N_KV),
        in_specs=[pl.BlockSpec((1, 1, n_str, half), lambda b, g: (b, g, 0, 0)),
                  pl.BlockSpec((1, 1, n_str, half), lambda b, g: (b, g, 0, 0)),
                  pl.BlockSpec((2, half), const2),
                  pl.BlockSpec((2, half), const2),
                  pl.BlockSpec((2 * half, CMP_HIDDEN), const2),
                  pl.BlockSpec((CMP_HIDDEN, HEAD_DIM), const2),
                  pl.BlockSpec((2 * half, CMP_HIDDEN), const2),
                  pl.BlockSpec((HEAD_DIM, CMP_HIDDEN), const2)],
        out_specs=[pl.BlockSpec((1, 1, n_str, HEAD_DIM), lambda b, g: (b, g, 0, 0)),
                   pl.BlockSpec((1, 1, HEAD_DIM, n_str), lambda b, g: (b, g, 0, 0))],
        out_shape=[jax.ShapeDtypeStruct((B, N_KV, n_str, HEAD_DIM), jnp.float32),
                   jax.ShapeDtypeStruct((B, N_KV, HEAD_DIM, n_str), jnp.float32)],
        compiler_params=pltpu.CompilerParams(
            dimension_semantics=("arbitrary", "arbitrary"), vmem_limit_bytes=VMEM_LIMIT),
        name="compress",
    )(strides(kc_in), strides(vc_in), pe_k.reshape(2, half), pe_v.reshape(2, half),
      wk1, wk2, wv1.astype(jnp.bfloat16), wv2.T.astype(jnp.bfloat16))


def _cmp_select_kernel(qt_ref, kc_ref, vct_ref, tab_ref, ovl_ref, oct_ref, sel_ref):
    i = pl.program_id(2)
    kc = kc_ref[0, 0]
    vct = vct_ref[0, 0]
    n_idx = lax.broadcasted_iota(jnp.int32, (N_CMP_PAD, TQ), 0)
    t_idx = i * TQ + lax.broadcasted_iota(jnp.int32, (N_CMP_PAD, TQ), 1)
    valid = t_idx >= n_idx * S_CMP + (L_CMP - 1)
    off = pl.multiple_of(240 - 16 * i, 16)
    p_sum = jnp.zeros((N_CMP_PAD, TQ), jnp.float32)
    for r in range(REP):
        qh = qt_ref[0, r * HEAD_DIM:(r + 1) * HEAD_DIM, :]
        s = jnp.dot(kc, qh, precision=_HI, preferred_element_type=jnp.float32)
        s = s + tab_ref[r, pl.ds(off, N_CMP_PAD), :]
        s = jnp.where(valid, s, MASK_NEG)
        m = jnp.max(s, axis=0, keepdims=True)
        p = jnp.where(valid, jnp.exp(s - m), 0.0)
        l = jnp.sum(p, axis=0, keepdims=True)
        p = p / jnp.where(l > 0.0, l, 1.0)
        oct_ref[0, r * HEAD_DIM:(r + 1) * HEAD_DIM, :] = jnp.dot(
            vct, p, preferred_element_type=jnp.float32)
        p_sum = p_sum + p
    imp = jnp.dot(ovl_ref[...], p_sum, precision=_HI, preferred_element_type=jnp.float32)
    n_blk = imp.shape[0]
    j_idx = lax.broadcasted_iota(jnp.int32, (n_blk, TQ), 0)
    cur = (i * TQ + lax.broadcasted_iota(jnp.int32, (n_blk, TQ), 1)) // L_SLC
    forced = (j_idx == 0) | (j_idx == cur) | (j_idx == cur - 1)
    allowed = j_idx <= cur
    score = jnp.where(allowed, imp + jnp.where(forced, FORCE_BONUS, 0.0), MASK_NEG)
    rank = jnp.zeros((n_blk, TQ), jnp.float32)
    for jp in range(n_blk):
        sj = score[jp:jp + 1, :]
        before = (sj > score) | ((sj == score) & (j_idx > jp))
        rank = rank + jnp.where(before, 1.0, 0.0)
    sel = allowed & (rank < float(N_SEL))
    neg = jnp.where(sel, 0.0, MASK_NEG)
    sel_ref[0, 0, 0:n_blk, :] = neg.astype(jnp.bfloat16)
    sel_ref[0, 0, n_blk:2 * n_blk, :] = jnp.zeros((n_blk, TQ), jnp.bfloat16)


def _overlap_t(S):
    n_slc = S // L_SLC
    c_start = np.arange(N_CMP_PAD) * S_CMP
    s_start = np.arange(n_slc) * L_SLC
    ov = np.clip(np.minimum(c_start[:, None] + L_CMP, s_start[None, :] + L_SLC)
                 - np.maximum(c_start[:, None], s_start[None, :]), 0, None).astype(np.float32) / L_CMP
    ov[(S - L_CMP) // S_CMP + 1:, :] = 0.0
    return jnp.asarray(ov.T)


def _cmp_select(qft, kc, vct, tab_c):
    B, _, S = qft.shape
    ni = S // TQ
    return pl.pallas_call(
        _cmp_select_kernel,
        grid=(B, N_KV, ni),
        in_specs=[pl.BlockSpec((1, REP * HEAD_DIM, TQ), lambda b, g, i: (b, g, i)),
                  pl.BlockSpec((1, 1, N_CMP_PAD, HEAD_DIM), lambda b, g, i: (b, g, 0, 0)),
                  pl.BlockSpec((1, 1, HEAD_DIM, N_CMP_PAD), lambda b, g, i: (b, g, 0, 0)),
                  pl.BlockSpec((REP, 512, TQ), lambda b, g, i: (g, 0, 0)),
                  pl.BlockSpec((S // L_SLC, N_CMP_PAD), lambda b, g, i: (0, 0))],
        out_specs=[pl.BlockSpec((1, REP * HEAD_DIM, TQ), lambda b, g, i: (b, g, i)),
                   pl.BlockSpec((1, 1, 128, TQ), lambda b, g, i: (b, g, 0, i))],
        out_shape=[jax.ShapeDtypeStruct((B, D_NSA, S), jnp.float32),
                   jax.ShapeDtypeStruct((B, N_KV, 128, S), jnp.bfloat16)],
        compiler_params=pltpu.CompilerParams(
            dimension_semantics=("arbitrary", "arbitrary", "arbitrary"),
            vmem_limit_bytes=VMEM_LIMIT),
        name="cmp_select",
    )(qft, kc, vct, tab_c, _overlap_t(S))


def _sparse_attn_kernel(qt_ref, sel_ref, ks_ref, kw_ref, vst_ref, vwt_ref, near_ref, edge_ref,
                        gt_ref, oct_ref, o_ref, m_ref, acc_ref, mw_ref, accw_ref, s_ref):
    i = pl.program_id(1)
    zeros64 = jnp.zeros((HEAD_DIM, TQ), jnp.bfloat16)
    off1 = jnp.where(i >= 1, 0.0, MASK_NEG)
    off2 = jnp.where(i >= 2, 0.0, MASK_NEG)
    j1 = jnp.maximum(i - 1, 0)
    j2 = jnp.maximum(i - 2, 0)
    dot = functools.partial(jnp.dot, preferred_element_type=jnp.float32)

    def q_window(h):
        qh = qt_ref[0, h * HEAD_DIM:(h + 1) * HEAD_DIM, :]
        return jnp.concatenate([qh, zeros64] if h // REP == 0 else [zeros64, qh], axis=0)

    def q_selected(h):
        return jnp.concatenate([q_window(h), sel_ref[0, h // REP]], axis=0)

    def key_tile_step(k, q_of, bias_of, v_ref, j, state):
        ms_ref, as_ref = state
        tile_max = []
        for h in range(N_HEADS):
            s = dot(k, q_of(h))
            if bias_of is not None:
                s = s + bias_of(h)
            s_ref[h] = s
            tile_max.append(jnp.max(s, axis=0, keepdims=True))
        for h in range(N_HEADS):
            m = ms_ref[h:h + 1, :]
            m_new = jnp.maximum(m, tile_max[h])
            alpha = jnp.exp2(m - m_new)
            p = jnp.exp2(s_ref[h] - m_new).astype(jnp.bfloat16)
            as_ref[h] = alpha * as_ref[h] + dot(v_ref[0, j, h // REP], p)
            ms_ref[h:h + 1, :] = m_new

    def k_tile(ref, j):
        return ref[0, pl.ds(pl.multiple_of(j * TK, TK), TK), :]

    for ref in (m_ref, mw_ref):
        ref[...] = jnp.full(ref.shape, MASK_NEG, jnp.float32)
    for ref in (acc_ref, accw_ref):
        ref[...] = jnp.zeros(ref.shape, jnp.float32)
    sel_state = (m_ref, acc_ref)
    win_state = (mw_ref, accw_ref)
    near0 = lambda h: near_ref[h, 0]
    near1 = lambda h: near_ref[h, 1] + off1
    edge = lambda h: edge_ref[...] + off2

    key_tile_step(k_tile(ks_ref, i), q_selected, near0, vst_ref, i, sel_state)
    key_tile_step(k_tile(kw_ref, i), q_window, near0, vwt_ref, i, win_state)
    key_tile_step(k_tile(ks_ref, j1), q_selected, near1, vst_ref, j1, sel_state)
    key_tile_step(k_tile(kw_ref, j1), q_window, near1, vwt_ref, j1, win_state)
    key_tile_step(k_tile(kw_ref, j2), q_window, edge, vwt_ref, j2, win_state)

    def far_body(j, carry):
        key_tile_step(k_tile(ks_ref, j), q_selected, None, vst_ref, j, sel_state)
        return carry

    lax.fori_loop(0, j1, far_body, 0)

    for pair in range(N_HEADS // 2):
        outs = []
        for h in (2 * pair, 2 * pair + 1):
            rows = slice(h * HEAD_DIM, (h + 1) * HEAD_DIM)
            o_s = acc_ref[h, 0:HEAD_DIM, :] / acc_ref[h, HEAD_DIM:HEAD_DIM + 1, :]
            o_w = accw_ref[h, 0:HEAD_DIM, :] / accw_ref[h, HEAD_DIM:HEAD_DIM + 1, :]
            outs.append(gt_ref[0, 3 * h:3 * h + 1, :] * oct_ref[0, rows, :]
                        + gt_ref[0, 3 * h + 1:3 * h + 2, :] * o_s
                        + gt_ref[0, 3 * h + 2:3 * h + 3, :] * o_w)
        o_pair = jnp.concatenate(outs, axis=0)
        o_ref[0, :, pair * 128:(pair + 1) * 128] = o_pair.T.astype(jnp.bfloat16)


def _sparse_attn(qt, sel, ks, kw, vst, vwt, near, gt, oct):
    B, _, S = qt.shape
    ni = S // TQ
    nt = S // TK
    key = np.arange(TK)[:, None]
    row = np.arange(TQ)[None, :]
    edge = jnp.asarray(np.where(2 * TK + row - key < WINDOW, 0.0, MASK_NEG).astype(np.float32))
    return pl.pallas_call(
        _sparse_attn_kernel,
        grid=(B, ni),
        in_specs=[pl.BlockSpec((1, D_NSA, TQ), lambda b, i: (b, 0, i)),
                  pl.BlockSpec((1, N_KV, 128, TQ), lambda b, i: (b, 0, 0, i)),
                  pl.BlockSpec((1, S, 256), lambda b, i: (b, 0, 0)),
                  pl.BlockSpec((1, S, 128), lambda b, i: (b, 0, 0)),
                  pl.BlockSpec((1, nt, N_KV, V_ROWS, TK), lambda b, i: (b, 0, 0, 0, 0)),
                  pl.BlockSpec((1, nt, N_KV, V_ROWS, TK), lambda b, i: (b, 0, 0, 0, 0)),
                  pl.BlockSpec((N_HEADS, 2, TK, TQ), lambda b, i: (0, 0, 0, 0)),
                  pl.BlockSpec((TK, TQ), lambda b, i: (0, 0)),
                  pl.BlockSpec((1, 32, TQ), lambda b, i: (b, 0, i)),
                  pl.BlockSpec((1, D_NSA, TQ), lambda b, i: (b, 0, i))],
        out_specs=pl.BlockSpec((1, TQ, D_NSA), lambda b, i: (b, i, 0)),
        out_shape=jax.ShapeDtypeStruct((B, S, D_NSA), jnp.bfloat16),
        scratch_shapes=[pltpu.VMEM((N_HEADS, TQ), jnp.float32),
                        pltpu.VMEM((N_HEADS, V_ROWS, TQ), jnp.float32),
                        pltpu.VMEM((N_HEADS, TQ), jnp.float32),
                        pltpu.VMEM((N_HEADS, V_ROWS, TQ), jnp.float32),
                        pltpu.VMEM((N_HEADS, TK, TQ), jnp.float32)],
        compiler_params=pltpu.CompilerParams(
            dimension_semantics=("arbitrary", "arbitrary"), vmem_limit_bytes=VMEM_LIMIT),
        name="sparse_attn",
    )(qt, sel, ks, kw, vst, vwt, near, edge, gt, oct)


def _out_proj_kernel(on_ref, oc_ref, x_ref, w_ref, gpost_ref, gffn_ref, x1_ref, h2_ref):
    y = jnp.dot(on_ref[...], w_ref[0:D_NSA, :], preferred_element_type=jnp.float32)
    y = y + jnp.dot(oc_ref[...], w_ref[D_NSA:D_NSA + D_CONV, :], preferred_element_type=jnp.float32)
    x1 = x_ref[...] + _rms(y, gpost_ref[...])
    x1_ref[...] = x1
    h2_ref[...] = _rms(x1, gffn_ref[...]).astype(jnp.bfloat16)


def _out_proj(o_nsa, o_conv, x, w_out, g_post, g_ffn):
    T, D = x.shape
    tm = TM_OUT
    const = lambda i: (0, 0)
    return pl.pallas_call(
        _out_proj_kernel,
        grid=(T // tm,),
        in_specs=[pl.BlockSpec((tm, D_NSA), lambda i: (i, 0)),
                  pl.BlockSpec((tm, D_CONV), lambda i: (i, 0)),
                  pl.BlockSpec((tm, D), lambda i: (i, 0)),
                  pl.BlockSpec((D, D), const),
                  pl.BlockSpec((1, D), const),
                  pl.BlockSpec((1, D), const)],
        out_specs=[pl.BlockSpec((tm, D), lambda i: (i, 0)),
                   pl.BlockSpec((tm, D), lambda i: (i, 0))],
        out_shape=[jax.ShapeDtypeStruct((T, D), jnp.float32),
                   jax.ShapeDtypeStruct((T, D), jnp.bfloat16)],
        compiler_params=pltpu.CompilerParams(
            dimension_semantics=("arbitrary",), vmem_limit_bytes=VMEM_LIMIT),
        name="out_proj",
    )(o_nsa, o_conv, x, w_out.astype(jnp.bfloat16), g_post.reshape(1, D), g_ffn.reshape(1, D))


def _conv_ffn_kernel(h_ref, x1_ref, wup_ref, cw_ref, wdn_ref, g_ref, o_ref, carry_ref, act_ref):
    i = pl.program_id(1)
    tm = h_ref.shape[1]
    hb = h_ref[0]

    @pl.when(i == 0)
    def _():
        carry_ref[...] = jnp.zeros_like(carry_ref)

    row = lax.broadcasted_iota(jnp.int32, (tm, FF_CHUNK), 0)
    is0 = row == 0
    is1 = row == 1

    def conv(z, c0):
        cols = slice(c0, c0 + FF_CHUNK)
        prev1 = carry_ref[7:8, cols]
        prev2 = carry_ref[6:7, cols]
        z1 = jnp.where(is0, prev1, pltpu.roll(z, 1, 0))
        z2 = jnp.where(is0, prev2, jnp.where(is1, prev1, pltpu.roll(z, 2, 0)))
        carry_ref[:, cols] = z[tm - 8:tm, :]
        return cw_ref[0:1, cols] * z2 + cw_ref[1:2, cols] * z1 + cw_ref[2:3, cols] * z

    for c in range(D_FF // FF_CHUNK):
        g0 = c * FF_CHUNK
        u0 = D_FF + c * FF_CHUNK
        zg = jnp.dot(hb, wup_ref[:, g0:g0 + FF_CHUNK], preferred_element_type=jnp.float32)
        zu = jnp.dot(hb, wup_ref[:, u0:u0 + FF_CHUNK], preferred_element_type=jnp.float32)
        act = _gelu(conv(zg, g0)) * conv(zu, u0)
        act_ref[:, g0:g0 + FF_CHUNK] = act.astype(jnp.bfloat16)
    y = jnp.dot(act_ref[...], wdn_ref[...], preferred_element_type=jnp.float32)
    o_ref[0] = x1_ref[0] + _rms(y, g_ref[...])


def _conv_ffn(h2, x1, w_up, conv_w, w_down, g_post):
    B, S, D = x1.shape
    tm = TM_FFN
    const = lambda b, i: (0, 0)
    single = pl.Buffered(1)
    return pl.pallas_call(
        _conv_ffn_kernel,
        grid=(B, S // tm),
        in_specs=[pl.BlockSpec((1, tm, D), lambda b, i: (b, i, 0)),
                  pl.BlockSpec((1, tm, D), lambda b, i: (b, i, 0)),
                  pl.BlockSpec((D, 2 * D_FF), const, pipeline_mode=single),
                  pl.BlockSpec((3, 2 * D_FF), const, pipeline_mode=single),
                  pl.BlockSpec((D_FF, D), const, pipeline_mode=single),
                  pl.BlockSpec((1, D), const, pipeline_mode=single)],
        out_specs=pl.BlockSpec((1, tm, D), lambda b, i: (b, i, 0)),
        out_shape=jax.ShapeDtypeStruct((B, S, D), jnp.float32),
        scratch_shapes=[pltpu.VMEM((8, 2 * D_FF), jnp.float32),
                        pltpu.VMEM((tm, D_FF), jnp.bfloat16)],
        compiler_params=pltpu.CompilerParams(
            dimension_semantics=("arbitrary", "arbitrary"), vmem_limit_bytes=VMEM_LIMIT),
        name="conv_ffn",
    )(h2, x1, w_up.astype(jnp.bfloat16), conv_w, w_down.astype(jnp.bfloat16), g_post.reshape(1, D))


def kernel(x, norm_mix_pre, norm_mix_post, norm_ffn_pre, norm_ffn_post, w_in, pe_cmp_k, pe_cmp_v,
           w_cmp_k1, w_cmp_k2, w_cmp_v1, w_cmp_v2, rel_bias, conv_mix_w, w_out, w_ffn_up,
           ffn_conv_w, w_ffn_down):
    B, S, D = x.shape
    assert (S, D) == (4096, D_MODEL) and norm_mix_pre.shape[0] == 1
    tab_c, near = _bias_tables(rel_bias)
    for l in range(norm_mix_pre.shape[0]):
        qft, qt, kc_in, vc_in, ks, kw, vst, vwt, gt, o_conv = _in_proj(
            x, norm_mix_pre[l], w_in[l], conv_mix_w[l])
        kc, vct = _compress(kc_in, vc_in, pe_cmp_k[l], pe_cmp_v[l], w_cmp_k1[l], w_cmp_k2[l],
                            w_cmp_v1[l], w_cmp_v2[l])
        oct, sel = _cmp_select(qft, kc, vct, tab_c)
        o_nsa = _sparse_attn(qt, sel, ks, kw, vst, vwt, near, gt, oct)
        x1, h2 = _out_proj(o_nsa.reshape(B * S, D_NSA), o_conv.reshape(B * S, D_CONV),
                           x.reshape(B * S, D), w_out[l], norm_mix_post[l], norm_ffn_pre[l])
        x = _conv_ffn(h2.reshape(B, S, D), x1.reshape(B, S, D), w_ffn_up[l], ffn_conv_w[l],
                      w_ffn_down[l], norm_ffn_post[l])
    return x
```

```python
import functools
import math

import numpy as np
import jax
import jax.numpy as jnp
from jax import lax
from jax.experimental import pallas as pl
from jax.experimental.pallas import tpu as pltpu

D_MODEL = 1024
D_NSA = 512
D_CONV = 512
HEAD_DIM = 64
N_HEADS = 8
N_KV = 2
REP = 4
KV_W = 128
N_GATES = 24
L_CMP = 32
S_CMP = 16
L_SLC = 64
N_SEL = 16
WINDOW = 512
CMP_HIDDEN = 256
D_FF = 2816
N_BUCKETS = 32
MAX_DIST = 128
RMS_EPS = 1e-6
MASK_NEG = -1e30
FORCE_BONUS = 1e4
LOG2E = math.log2(math.e)

TQ = 256
TK = 256
RANK_UNROLL = 4
V_ROWS = 80
N_CMP_PAD = 256
TM_IN = 512
TM_FFN = 512
FF_CHUNK = 256
VMEM_LIMIT = 56 * 1024 * 1024

_NT = (((1,), (1,)), ((), ()))


def _bucket_np(d):
    max_exact = N_BUCKETS // 2
    d = np.maximum(d, 0)
    df = np.maximum(d, 1).astype(np.float32)
    large = max_exact + (np.log(df / max_exact) / math.log(MAX_DIST / max_exact)
                         * (N_BUCKETS - max_exact)).astype(np.int32)
    return np.where(d < max_exact, d, np.minimum(large, N_BUCKETS - 1)).astype(np.int32)


def _bucket_or_mask(d):
    return np.where(d >= 0, _bucket_np(d), -1).astype(np.int32)


def _rms(x, g):
    return x * lax.rsqrt(jnp.mean(x * x, axis=-1, keepdims=True) + RMS_EPS) * g


def _gelu(x):
    return jax.nn.gelu(x, approximate=True)


def _bias_tables_kernel(rb_ref, idx_c_ref, idx_n_ref, tab_c_ref, tab_n_ref):
    h = pl.program_id(0)
    last = rb_ref[h, N_BUCKETS - 1]
    idx_c = idx_c_ref[...]
    idx_n = idx_n_ref[...]
    val_c = jnp.zeros(idx_c.shape, jnp.float32)
    val_n = jnp.zeros(idx_n.shape, jnp.float32)
    for b in range(N_BUCKETS - 1):
        delta = rb_ref[h, b] - last
        val_c = jnp.where(idx_c == b, delta, val_c)
        val_n = jnp.where(idx_n == b, delta, val_n)
    tab_c_ref[0] = val_c * LOG2E
    tab_n_ref[0] = jnp.where(idx_n < 0, MASK_NEG, val_n * LOG2E)


def _bias_tables(rel_bias):
    u = np.arange(512)[:, None] - 240
    row = np.arange(TQ)[None, :]
    d_c = row - S_CMP * u - (L_CMP - 1)
    idx_c = np.where(d_c >= 0, _bucket_np(d_c), N_BUCKETS - 1).astype(np.int32)
    key = np.arange(TK)[:, None]
    idx_n = np.stack([_bucket_or_mask(dd + row - key) for dd in (0, TK)])
    return pl.pallas_call(
        _bias_tables_kernel,
        grid=(N_HEADS,),
        in_specs=[pl.BlockSpec(memory_space=pltpu.SMEM),
                  pl.BlockSpec((512, TQ), lambda h: (0, 0)),
                  pl.BlockSpec((2, TK, TQ), lambda h: (0, 0, 0))],
        out_specs=[pl.BlockSpec((1, 512, TQ), lambda h: (h, 0, 0)),
                   pl.BlockSpec((1, 2, TK, TQ), lambda h: (h, 0, 0, 0))],
        out_shape=[jax.ShapeDtypeStruct((N_HEADS, 512, TQ), jnp.float32),
                   jax.ShapeDtypeStruct((N_HEADS, 2, TK, TQ), jnp.float32)],
        name="bias_tables",
    )(rel_bias, jnp.asarray(idx_c), jnp.asarray(idx_n))


def _in_proj_kernel(x_ref, g_ref, wq_ref, wnat_ref, wvt_ref, wgt_ref, cw_ref,
                    qt_ref, kc_ref, vc_ref, ks_ref, kw_ref, vst_ref, vwt_ref, gt_ref,
                    oconv_ref, carry_ref, stage_ref):
    i = pl.program_id(1)
    tm = x_ref.shape[1]
    hb = _rms(x_ref[0], g_ref[...]).astype(jnp.bfloat16)

    qt_ref[0] = lax.dot_general(wq_ref[...], hb, _NT,
                                preferred_element_type=jnp.float32).astype(jnp.bfloat16)
    vt = lax.dot_general(wvt_ref[...], hb, _NT, preferred_element_type=jnp.float32)
    ones = jnp.ones((V_ROWS - HEAD_DIM, TK), jnp.bfloat16)
    for t in range(tm // TK):
        for g in range(N_KV):
            for out_ref, r0 in ((vst_ref, g * HEAD_DIM), (vwt_ref, KV_W + g * HEAD_DIM)):
                out_ref[0, t, g, 0:HEAD_DIM, :] = vt[r0:r0 + HEAD_DIM,
                                                      t * TK:(t + 1) * TK].astype(jnp.bfloat16)
                out_ref[0, t, g, HEAD_DIM:V_ROWS, :] = ones
    gl = lax.dot_general(wgt_ref[...], hb, _NT, preferred_element_type=jnp.float32)
    gt_ref[0] = jax.nn.sigmoid(gl)

    for out_ref, c0 in ((kc_ref, 0), (vc_ref, KV_W)):
        stage_ref[...] = jnp.dot(hb, wnat_ref[:, c0:c0 + KV_W], preferred_element_type=jnp.float32)
        for l in range(S_CMP):
            out_ref[0, :, l * KV_W:(l + 1) * KV_W] = stage_ref[pl.ds(l, tm // S_CMP, stride=S_CMP), :]
    k_s = jnp.dot(hb, wnat_ref[:, 256:384], preferred_element_type=jnp.float32)
    blk = (i * tm + lax.broadcasted_iota(jnp.int32, (tm, KV_W), 0)) // L_SLC
    col = lax.broadcasted_iota(jnp.int32, (tm, KV_W), 1)
    onehot = jnp.where(blk == col, 1.0, 0.0)
    ks_ref[0, :, 0:KV_W] = k_s.astype(jnp.bfloat16)
    ks_ref[0, :, KV_W:2 * KV_W] = onehot.astype(jnp.bfloat16)
    kw_ref[0] = jnp.dot(hb, wnat_ref[:, 384:512],
                        preferred_element_type=jnp.float32).astype(jnp.bfloat16)

    bg = jnp.dot(hb, wnat_ref[:, 512:1024], preferred_element_type=jnp.float32)
    cg = jnp.dot(hb, wnat_ref[:, 1024:1536], preferred_element_type=jnp.float32)
    xt = jnp.dot(hb, wnat_ref[:, 1536:2048], preferred_element_type=jnp.float32)
    z = cg * xt

    @pl.when(i == 0)
    def _():
        carry_ref[...] = jnp.zeros_like(carry_ref)

    prev1 = carry_ref[7:8, :]
    prev2 = carry_ref[6:7, :]
    row = lax.broadcasted_iota(jnp.int32, z.shape, 0)
    z1 = jnp.where(row == 0, prev1, pltpu.roll(z, 1, 0))
    z2 = jnp.where(row == 0, prev2, jnp.where(row == 1, prev1, pltpu.roll(z, 2, 0)))
    y = cw_ref[0:1, :] * z2 + cw_ref[1:2, :] * z1 + cw_ref[2:3, :] * z
    oconv_ref[0] = (bg * y).astype(jnp.bfloat16)
    carry_ref[...] = z[tm - 8:tm, :]


def _in_proj(x, g_pre, w_in, conv_w):
    B, S, D = x.shape
    tm = TM_IN
    wq = (w_in[:, 0:512].T * (HEAD_DIM ** -0.5 * LOG2E)).astype(jnp.bfloat16)
    wnat = jnp.concatenate([w_in[:, 512:896], w_in[:, 1024:1152],
                            w_in[:, 1304:2840]], axis=1).astype(jnp.bfloat16)
    wvt = jnp.concatenate([w_in[:, 896:1024], w_in[:, 1152:1280]], axis=1).T.astype(jnp.bfloat16)
    wgt = jnp.pad(w_in[:, 1280:1304], ((0, 0), (0, 8))).T.astype(jnp.bfloat16)
    nt = S // TK
    const = lambda b, i: (0, 0)
    outs = pl.pallas_call(
        _in_proj_kernel,
        grid=(B, S // tm),
        in_specs=[pl.BlockSpec((1, tm, D), lambda b, i: (b, i, 0)),
                  pl.BlockSpec((1, D), const),
                  pl.BlockSpec((512, D), const),
                  pl.BlockSpec((D, 2048), const),
                  pl.BlockSpec((256, D), const),
                  pl.BlockSpec((32, D), const),
                  pl.BlockSpec((3, D_CONV), const)],
        out_specs=[pl.BlockSpec((1, 512, tm), lambda b, i: (b, 0, i)),
                   pl.BlockSpec((1, tm // S_CMP, S_CMP * KV_W), lambda b, i: (b, i, 0)),
                   pl.BlockSpec((1, tm // S_CMP, S_CMP * KV_W), lambda b, i: (b, i, 0)),
                   pl.BlockSpec((1, tm, 256), lambda b, i: (b, i, 0)),
                   pl.BlockSpec((1, tm, 128), lambda b, i: (b, i, 0)),
                   pl.BlockSpec((1, tm // TK, N_KV, V_ROWS, TK), lambda b, i: (b, i, 0, 0, 0)),
                   pl.BlockSpec((1, tm // TK, N_KV, V_ROWS, TK), lambda b, i: (b, i, 0, 0, 0)),
                   pl.BlockSpec((1, 32, tm), lambda b, i: (b, 0, i)),
                   pl.BlockSpec((1, tm, D_CONV), lambda b, i: (b, i, 0))],
        out_shape=[jax.ShapeDtypeStruct((B, 512, S), jnp.bfloat16),
                   jax.ShapeDtypeStruct((B, S // S_CMP, S_CMP * KV_W), jnp.float32),
                   jax.ShapeDtypeStruct((B, S // S_CMP, S_CMP * KV_W), jnp.float32),
                   jax.ShapeDtypeStruct((B, S, 256), jnp.bfloat16),
                   jax.ShapeDtypeStruct((B, S, 128), jnp.bfloat16),
                   jax.ShapeDtypeStruct((B, nt, N_KV, V_ROWS, TK), jnp.bfloat16),
                   jax.ShapeDtypeStruct((B, nt, N_KV, V_ROWS, TK), jnp.bfloat16),
                   jax.ShapeDtypeStruct((B, 32, S), jnp.float32),
                   jax.ShapeDtypeStruct((B, S, D_CONV), jnp.bfloat16)],
        scratch_shapes=[pltpu.VMEM((8, D_CONV), jnp.float32),
                        pltpu.VMEM((tm, KV_W), jnp.float32)],
        compiler_params=pltpu.CompilerParams(
            dimension_semantics=("arbitrary", "arbitrary"), vmem_limit_bytes=VMEM_LIMIT),
        name="in_proj",
    )(x, g_pre.reshape(1, D), wq, wnat, wvt, wgt, conv_w)
    return outs


def _compress_kernel(ck_ref, cv_ref, pek_ref, pev_ref, wk1_ref, wk2_ref, wv1_ref, wv2t_ref,
                     kc_ref, vct_ref):
    def hidden(c_ref, pe_ref, w1_ref):
        c = c_ref[0]
        a = jnp.dot((c + pe_ref[0:1, :]).astype(jnp.bfloat16), w1_ref[0],
                    preferred_element_type=jnp.float32)
        b = jnp.dot((c + pe_ref[1:2, :]).astype(jnp.bfloat16), w1_ref[1],
                    preferred_element_type=jnp.float32)
        return _gelu(a + pltpu.roll(b, N_CMP_PAD - 1, 0)).astype(jnp.bfloat16)

    hk = hidden(ck_ref, pek_ref, wk1_ref)
    hv = hidden(cv_ref, pev_ref, wv1_ref)
    for g in range(N_KV):
        cols = slice(g * CMP_HIDDEN, (g + 1) * CMP_HIDDEN)
        kc_ref[0, g] = jnp.dot(hk[:, cols], wk2_ref[...],
                               preferred_element_type=jnp.float32).astype(jnp.bfloat16)
        vct_ref[0, g] = lax.dot_general(wv2t_ref[...], hv[:, cols], _NT,
                                        preferred_element_type=jnp.float32).astype(jnp.bfloat16)


def _compress(kc_in, vc_in, pe_k, pe_v, wk1, wk2, wv1, wv2):
    B, n_str, width = kc_in.shape

    def both_groups(w1):
        w = w1.reshape(2, S_CMP, HEAD_DIM, CMP_HIDDEN)
        w = jnp.einsum('aldc,gh->algdhc', w, jnp.eye(N_KV, dtype=w.dtype))
        return w.reshape(2, width, N_KV * CMP_HIDDEN).astype(jnp.bfloat16)

    def pe_rows(pe):
        return jnp.broadcast_to(pe.reshape(2, S_CMP, 1, HEAD_DIM),
                                (2, S_CMP, N_KV, HEAD_DIM)).reshape(2, width)

    const2 = lambda b: (0, 0)
    const3 = lambda b: (0, 0, 0)
    return pl.pallas_call(
        _compress_kernel,
        grid=(B,),
        in_specs=[pl.BlockSpec((1, n_str, width), lambda b: (b, 0, 0)),
                  pl.BlockSpec((1, n_str, width), lambda b: (b, 0, 0)),
                  pl.BlockSpec((2, width), const2),
                  pl.BlockSpec((2, width), const2),
                  pl.BlockSpec((2, width, N_KV * CMP_HIDDEN), const3),
                  pl.BlockSpec((CMP_HIDDEN, HEAD_DIM), const2),
                  pl.BlockSpec((2, width, N_KV * CMP_HIDDEN), const3),
                  pl.BlockSpec((HEAD_DIM, CMP_HIDDEN), const2)],
        out_specs=[pl.BlockSpec((1, N_KV, n_str, HEAD_DIM), lambda b: (b, 0, 0, 0)),
                   pl.BlockSpec((1, N_KV, HEAD_DIM, n_str), lambda b: (b, 0, 0, 0))],
        out_shape=[jax.ShapeDtypeStruct((B, N_KV, n_str, HEAD_DIM), jnp.bfloat16),
                   jax.ShapeDtypeStruct((B, N_KV, HEAD_DIM, n_str), jnp.bfloat16)],
        compiler_params=pltpu.CompilerParams(
            dimension_semantics=("arbitrary",), vmem_limit_bytes=VMEM_LIMIT),
        name="compress",
    )(kc_in, vc_in, pe_rows(pe_k), pe_rows(pe_v), both_groups(wk1), wk2.astype(jnp.bfloat16),
      both_groups(wv1), wv2.T.astype(jnp.bfloat16))


def _cmp_select_kernel(qt_ref, kc_ref, vct_ref, tab_ref, ovl_ref, oct_ref, sel_ref, score_ref):
    i = pl.program_id(2)
    kc = kc_ref[0, 0]
    vct = vct_ref[0, 0]
    n_idx = lax.broadcasted_iota(jnp.int32, (N_CMP_PAD, TQ), 0)
    t_idx = i * TQ + lax.broadcasted_iota(jnp.int32, (N_CMP_PAD, TQ), 1)
    valid = t_idx >= n_idx * S_CMP + (L_CMP - 1)
    off = pl.multiple_of(240 - 16 * i, 16)
    p_sum = jnp.zeros((N_CMP_PAD, TQ), jnp.float32)
    for r in range(REP):
        qh = qt_ref[0, r * HEAD_DIM:(r + 1) * HEAD_DIM, :]
        s = jnp.dot(kc, qh, preferred_element_type=jnp.float32)
        s = jnp.where(valid, s + tab_ref[r, pl.ds(off, N_CMP_PAD), :], MASK_NEG)
        m = jnp.max(s, axis=0, keepdims=True)
        p = jnp.where(valid, jnp.exp2(s - m), 0.0)
        l = jnp.sum(p, axis=0, keepdims=True)
        p = p / jnp.where(l > 0.0, l, 1.0)
        oct_ref[0, r * HEAD_DIM:(r + 1) * HEAD_DIM, :] = jnp.dot(
            vct, p.astype(jnp.bfloat16), preferred_element_type=jnp.float32)
        p_sum = p_sum + p
    imp = jnp.dot(ovl_ref[...], p_sum.astype(jnp.bfloat16),
                  preferred_element_type=jnp.float32)
    n_blk = imp.shape[0]
    j_idx = lax.broadcasted_iota(jnp.int32, (n_blk, TQ), 0)
    cur = (i * TQ + lax.broadcasted_iota(jnp.int32, (n_blk, TQ), 1)) // L_SLC
    forced = (j_idx == 0) | (j_idx == cur) | (j_idx == cur - 1)
    allowed = j_idx <= cur
    score = jnp.where(allowed, imp + jnp.where(forced, FORCE_BONUS, 0.0), MASK_NEG)
    score_ref[...] = score

    def count_before(c, rank):
        for u in range(RANK_UNROLL):
            jp = c * RANK_UNROLL + u
            sj = score_ref[pl.ds(jp, 1), :]
            before = (sj > score) | ((sj == score) & (j_idx > jp))
            rank = rank + jnp.where(before, 1.0, 0.0)
        return rank

    n_live = (i + 1) * (TQ // L_SLC)
    rank = lax.fori_loop(0, n_live // RANK_UNROLL, count_before,
                         jnp.zeros((n_blk, TQ), jnp.float32))
    sel = allowed & (rank < float(N_SEL))
    neg = jnp.where(sel, 0.0, MASK_NEG)
    sel_ref[0, 0, 0:n_blk, :] = neg.astype(jnp.bfloat16)
    sel_ref[0, 0, n_blk:2 * n_blk, :] = jnp.zeros((n_blk, TQ), jnp.bfloat16)


def _overlap_t(S):
    n_slc = S // L_SLC
    c_start = np.arange(N_CMP_PAD) * S_CMP
    s_start = np.arange(n_slc) * L_SLC
    ov = np.clip(np.minimum(c_start[:, None] + L_CMP, s_start[None, :] + L_SLC)
                 - np.maximum(c_start[:, None], s_start[None, :]), 0, None).astype(np.float32) / L_CMP
    ov[(S - L_CMP) // S_CMP + 1:, :] = 0.0
    return jnp.asarray(ov.T, dtype=jnp.bfloat16)


def _cmp_select(qt, kc, vct, tab_c):
    B, _, S = qt.shape
    ni = S // TQ
    return pl.pallas_call(
        _cmp_select_kernel,
        grid=(B, N_KV, ni),
        in_specs=[pl.BlockSpec((1, REP * HEAD_DIM, TQ), lambda b, g, i: (b, g, i)),
                  pl.BlockSpec((1, 1, N_CMP_PAD, HEAD_DIM), lambda b, g, i: (b, g, 0, 0)),
                  pl.BlockSpec((1, 1, HEAD_DIM, N_CMP_PAD), lambda b, g, i: (b, g, 0, 0)),
                  pl.BlockSpec((REP, 512, TQ), lambda b, g, i: (g, 0, 0)),
                  pl.BlockSpec((S // L_SLC, N_CMP_PAD), lambda b, g, i: (0, 0))],
        out_specs=[pl.BlockSpec((1, REP * HEAD_DIM, TQ), lambda b, g, i: (b, g, i)),
                   pl.BlockSpec((1, 1, 128, TQ), lambda b, g, i: (b, g, 0, i))],
        out_shape=[jax.ShapeDtypeStruct((B, D_NSA, S), jnp.float32),
                   jax.ShapeDtypeStruct((B, N_KV, 128, S), jnp.bfloat16)],
        scratch_shapes=[pltpu.VMEM((S // L_SLC, TQ), jnp.float32)],
        compiler_params=pltpu.CompilerParams(
            dimension_semantics=("arbitrary", "arbitrary", "arbitrary"),
            vmem_limit_bytes=VMEM_LIMIT),
        name="cmp_select",
    )(qt, kc, vct, tab_c, _overlap_t(S))


def _sparse_attn_kernel(qt_ref, sel_ref, ks_ref, kw_ref, vst_ref, vwt_ref, near_ref, edge_ref,
                        gt_ref, oct_ref, o_ref, m_ref, acc_ref, mw_ref, accw_ref, s_ref):
    i = pl.program_id(1)
    zeros64 = jnp.zeros((HEAD_DIM, TQ), jnp.bfloat16)
    off1 = jnp.where(i >= 1, 0.0, MASK_NEG)
    off2 = jnp.where(i >= 2, 0.0, MASK_NEG)
    j1 = jnp.maximum(i - 1, 0)
    j2 = jnp.maximum(i - 2, 0)
    dot = functools.partial(jnp.dot, preferred_element_type=jnp.float32)

    def q_window(h):
        qh = qt_ref[0, h * HEAD_DIM:(h + 1) * HEAD_DIM, :]
        return jnp.concatenate([qh, zeros64] if h // REP == 0 else [zeros64, qh], axis=0)

    def q_selected(h):
        return jnp.concatenate([q_window(h), sel_ref[0, h // REP]], axis=0)

    def key_tile_step(k, q_of, bias_of, v_ref, j, state):
        ms_ref, as_ref = state
        tile_max = []
        for h in range(N_HEADS):
            s = dot(k, q_of(h))
            if bias_of is not None:
                s = s + bias_of(h)
            s_ref[h] = s
            tile_max.append(jnp.max(s, axis=0, keepdims=True))
        for h in range(N_HEADS):
            m = ms_ref[h:h + 1, :]
            m_new = jnp.maximum(m, tile_max[h])
            alpha = jnp.exp2(m - m_new)
            p = jnp.exp2(s_ref[h] - m_new).astype(jnp.bfloat16)
            as_ref[h] = alpha * as_ref[h] + dot(v_ref[0, j, h // REP], p)
            ms_ref[h:h + 1, :] = m_new

    def k_tile(ref, j):
        return ref[0, pl.ds(pl.multiple_of(j * TK, TK), TK), :]

    for ref in (m_ref, mw_ref):
        ref[...] = jnp.full(ref.shape, MASK_NEG, jnp.float32)
    for ref in (acc_ref, accw_ref):
        ref[...] = jnp.zeros(ref.shape, jnp.float32)
    sel_state = (m_ref, acc_ref)
    win_state = (mw_ref, accw_ref)
    near0 = lambda h: near_ref[h, 0]
    near1 = lambda h: near_ref[h, 1] + off1
    edge = lambda h: edge_ref[...] + off2

    key_tile_step(k_tile(ks_ref, i), q_selected, near0, vst_ref, i, sel_state)
    key_tile_step(k_tile(kw_ref, i), q_window, near0, vwt_ref, i, win_state)
    key_tile_step(k_tile(ks_ref, j1), q_selected, near1, vst_ref, j1, sel_state)
    key_tile_step(k_tile(kw_ref, j1), q_window, near1, vwt_ref, j1, win_state)
    key_tile_step(k_tile(kw_ref, j2), q_window, edge, vwt_ref, j2, win_state)

    def far_body(j, carry):
        key_tile_step(k_tile(ks_ref, j), q_selected, None, vst_ref, j, sel_state)
        return carry

    lax.fori_loop(0, j1, far_body, 0)

    for pair in range(N_HEADS // 2):
        outs = []
        for h in (2 * pair, 2 * pair + 1):
            rows = slice(h * HEAD_DIM, (h + 1) * HEAD_DIM)
            o_s = acc_ref[h, 0:HEAD_DIM, :] / acc_ref[h, HEAD_DIM:HEAD_DIM + 1, :]
            o_w = accw_ref[h, 0:HEAD_DIM, :] / accw_ref[h, HEAD_DIM:HEAD_DIM + 1, :]
            outs.append(gt_ref[0, 3 * h:3 * h + 1, :] * oct_ref[0, rows, :]
                        + gt_ref[0, 3 * h + 1:3 * h + 2, :] * o_s
                        + gt_ref[0, 3 * h + 2:3 * h + 3, :] * o_w)
        o_pair = jnp.concatenate(outs, axis=0)
        o_ref[0, :, pair * 128:(pair + 1) * 128] = o_pair.T.astype(jnp.bfloat16)


def _sparse_attn(qt, sel, ks, kw, vst, vwt, near, gt, oct):
    B, _, S = qt.shape
    ni = S // TQ
    nt = S // TK
    key = np.arange(TK)[:, None]
    row = np.arange(TQ)[None, :]
    edge = jnp.asarray(np.where(2 * TK + row - key < WINDOW, 0.0, MASK_NEG).astype(np.float32))
    return pl.pallas_call(
        _sparse_attn_kernel,
        grid=(B, ni),
        in_specs=[pl.BlockSpec((1, D_NSA, TQ), lambda b, i: (b, 0, i)),
                  pl.BlockSpec((1, N_KV, 128, TQ), lambda b, i: (b, 0, 0, i)),
                  pl.BlockSpec((1, S, 256), lambda b, i: (b, 0, 0)),
                  pl.BlockSpec((1, S, 128), lambda b, i: (b, 0, 0)),
                  pl.BlockSpec((1, nt, N_KV, V_ROWS, TK), lambda b, i: (b, 0, 0, 0, 0)),
                  pl.BlockSpec((1, nt, N_KV, V_ROWS, TK), lambda b, i: (b, 0, 0, 0, 0)),
                  pl.BlockSpec((N_HEADS, 2, TK, TQ), lambda b, i: (0, 0, 0, 0)),
                  pl.BlockSpec((TK, TQ), lambda b, i: (0, 0)),
                  pl.BlockSpec((1, 32, TQ), lambda b, i: (b, 0, i)),
                  pl.BlockSpec((1, D_NSA, TQ), lambda b, i: (b, 0, i))],
        out_specs=pl.BlockSpec((1, TQ, D_NSA), lambda b, i: (b, i, 0)),
        out_shape=jax.ShapeDtypeStruct((B, S, D_NSA), jnp.bfloat16),
        scratch_shapes=[pltpu.VMEM((N_HEADS, TQ), jnp.float32),
                        pltpu.VMEM((N_HEADS, V_ROWS, TQ), jnp.float32),
                        pltpu.VMEM((N_HEADS, TQ), jnp.float32),
                        pltpu.VMEM((N_HEADS, V_ROWS, TQ), jnp.float32),
                        pltpu.VMEM((N_HEADS, TK, TQ), jnp.float32)],
        compiler_params=pltpu.CompilerParams(
            dimension_semantics=("arbitrary", "arbitrary"), vmem_limit_bytes=VMEM_LIMIT),
        name="sparse_attn",
    )(qt, sel, ks, kw, vst, vwt, near, edge, gt, oct)


def _out_ffn_kernel(on_ref, oc_ref, x_ref, wo_ref, gpost_ref, gffn_ref, wup_ref, cw_ref, wdn_ref,
                    g_ref, o_ref, carry_ref, act_ref):
    i = pl.program_id(1)
    tm = x_ref.shape[1]
    y = jnp.dot(on_ref[0], wo_ref[0:D_NSA, :], preferred_element_type=jnp.float32)
    y = y + jnp.dot(oc_ref[0], wo_ref[D_NSA:D_NSA + D_CONV, :], preferred_element_type=jnp.float32)
    x1 = x_ref[0] + _rms(y, gpost_ref[...])
    hb = _rms(x1, gffn_ref[...]).astype(jnp.bfloat16)

    @pl.when(i == 0)
    def _():
        carry_ref[...] = jnp.zeros_like(carry_ref)

    row = lax.broadcasted_iota(jnp.int32, (tm, FF_CHUNK), 0)
    is0 = row == 0
    is1 = row == 1

    def conv(z, c0):
        cols = slice(c0, c0 + FF_CHUNK)
        prev1 = carry_ref[7:8, cols]
        prev2 = carry_ref[6:7, cols]
        z1 = jnp.where(is0, prev1, pltpu.roll(z, 1, 0))
        z2 = jnp.where(is0, prev2, jnp.where(is1, prev1, pltpu.roll(z, 2, 0)))
        carry_ref[:, cols] = z[tm - 8:tm, :]
        return cw_ref[0:1, cols] * z2 + cw_ref[1:2, cols] * z1 + cw_ref[2:3, cols] * z

    for c in range(D_FF // FF_CHUNK):
        g0 = c * FF_CHUNK
        u0 = D_FF + c * FF_CHUNK
        zg = jnp.dot(hb, wup_ref[:, g0:g0 + FF_CHUNK], preferred_element_type=jnp.float32)
        zu = jnp.dot(hb, wup_ref[:, u0:u0 + FF_CHUNK], preferred_element_type=jnp.float32)
        act = _gelu(conv(zg, g0)) * conv(zu, u0)
        act_ref[:, g0:g0 + FF_CHUNK] = act.astype(jnp.bfloat16)
    y2 = jnp.dot(act_ref[...], wdn_ref[...], preferred_element_type=jnp.float32)
    o_ref[0] = x1 + _rms(y2, g_ref[...])


def _out_ffn(o_nsa, o_conv, x, w_out, g_post, g_ffn, w_up, conv_w, w_down, g_ffn_post):
    B, S, D = x.shape
    tm = TM_FFN
    const = lambda b, i: (0, 0)
    single = pl.Buffered(1)
    row_tile = lambda width: pl.BlockSpec((1, tm, width), lambda b, i: (b, i, 0))
    weight = lambda shape: pl.BlockSpec(shape, const, pipeline_mode=single)
    return pl.pallas_call(
        _out_ffn_kernel,
        grid=(B, S // tm),
        in_specs=[row_tile(D_NSA), row_tile(D_CONV), row_tile(D),
                  weight((D, D)), weight((1, D)), weight((1, D)),
                  weight((D, 2 * D_FF)), weight((3, 2 * D_FF)), weight((D_FF, D)), weight((1, D))],
        out_specs=row_tile(D),
        out_shape=jax.ShapeDtypeStruct((B, S, D), jnp.float32),
        scratch_shapes=[pltpu.VMEM((8, 2 * D_FF), jnp.float32),
                        pltpu.VMEM((tm, D_FF), jnp.bfloat16)],
        compiler_params=pltpu.CompilerParams(
            dimension_semantics=("arbitrary", "arbitrary"), vmem_limit_bytes=VMEM_LIMIT),
        name="out_ffn",
    )(o_nsa, o_conv, x, w_out.astype(jnp.bfloat16), g_post.reshape(1, D), g_ffn.reshape(1, D),
      w_up.astype(jnp.bfloat16), conv_w, w_down.astype(jnp.bfloat16), g_ffn_post.reshape(1, D))


def kernel(x, norm_mix_pre, norm_mix_post, norm_ffn_pre, norm_ffn_post, w_in, pe_cmp_k, pe_cmp_v,
           w_cmp_k1, w_cmp_k2, w_cmp_v1, w_cmp_v2, rel_bias, conv_mix_w, w_out, w_ffn_up,
           ffn_conv_w, w_ffn_down):
    B, S, D = x.shape
    assert (S, D) == (4096, D_MODEL) and norm_mix_pre.shape[0] == 1
    tab_c, near = _bias_tables(rel_bias)
    for l in range(norm_mix_pre.shape[0]):
        qt, kc_in, vc_in, ks, kw, vst, vwt, gt, o_conv = _in_proj(
            x, norm_mix_pre[l], w_in[l], conv_mix_w[l])
        kc, vct = _compress(kc_in, vc_in, pe_cmp_k[l], pe_cmp_v[l], w_cmp_k1[l], w_cmp_k2[l],
                            w_cmp_v1[l], w_cmp_v2[l])
        oct, sel = _cmp_select(qt, kc, vct, tab_c)
        o_nsa = _sparse_attn(qt, sel, ks, kw, vst, vwt, near, gt, oct)
        x = _out_ffn(o_nsa, o_conv, x, w_out[l], norm_mix_post[l], norm_ffn_pre[l],
                     w_ffn_up[l], ffn_conv_w[l], w_ffn_down[l], norm_ffn_post[l])
    return x
```

```python
import functools
import math

import numpy as np
import jax
import jax.numpy as jnp
from jax import lax
from jax.experimental import pallas as pl
from jax.experimental.pallas import tpu as pltpu

D_MODEL = 1024
D_NSA = 512
D_CONV = 512
HEAD_DIM = 64
N_HEADS = 8
N_KV = 2
REP = 4
KV_W = 128
N_GATES = 24
L_CMP = 32
S_CMP = 16
L_SLC = 64
N_SEL = 16
WINDOW = 512
CMP_HIDDEN = 256
D_FF = 2816
N_BUCKETS = 32
MAX_DIST = 128
RMS_EPS = 1e-6
MASK_NEG = -1e30
FORCE_BONUS = 1e4
LOG2E = math.log2(math.e)

TQ = 256
TK = 256
V_ROWS = 80
C_ROWS = V_ROWS + 64
N_CMP_PAD = 256
TM_IN = 512
TM_FFN = 512
FF_CHUNK = 256
VMEM_LIMIT = 56 * 1024 * 1024

_NT = (((1,), (1,)), ((), ()))


def _bucket_np(d):
    max_exact = N_BUCKETS // 2
    d = np.maximum(d, 0)
    df = np.maximum(d, 1).astype(np.float32)
    large = max_exact + (np.log(df / max_exact) / math.log(MAX_DIST / max_exact)
                         * (N_BUCKETS - max_exact)).astype(np.int32)
    return np.where(d < max_exact, d, np.minimum(large, N_BUCKETS - 1)).astype(np.int32)


def _bucket_or_mask(d):
    return np.where(d >= 0, _bucket_np(d), -1).astype(np.int32)


def _rms(x, g):
    return x * lax.rsqrt(jnp.mean(x * x, axis=-1, keepdims=True) + RMS_EPS) * g


def _gelu(x):
    return jax.nn.gelu(x, approximate=True)


def _bias_tables_kernel(rb_ref, idx_c_ref, idx_n_ref, tab_c_ref, tab_n_ref):
    h = pl.program_id(0)
    last = rb_ref[h, N_BUCKETS - 1]
    idx_c = idx_c_ref[...]
    idx_n = idx_n_ref[...]
    val_c = jnp.zeros(idx_c.shape, jnp.float32)
    val_n = jnp.zeros(idx_n.shape, jnp.float32)
    for b in range(N_BUCKETS - 1):
        delta = rb_ref[h, b] - last
        val_c = jnp.where(idx_c == b, delta, val_c)
        val_n = jnp.where(idx_n == b, delta, val_n)
    tab_c_ref[0] = jnp.where(idx_c < 0, MASK_NEG, val_c * LOG2E)
    tab_n_ref[0] = jnp.where(idx_n < 0, MASK_NEG, val_n * LOG2E)


def _bias_tables(rel_bias):
    u = np.arange(512)[:, None] - 240
    row = np.arange(TQ)[None, :]
    d_c = row - S_CMP * u - (L_CMP - 1)
    idx_c = _bucket_or_mask(d_c)
    key = np.arange(TK)[:, None]
    idx_n = np.stack([_bucket_or_mask(dd + row - key) for dd in (0, TK)])
    return pl.pallas_call(
        _bias_tables_kernel,
        grid=(N_HEADS,),
        in_specs=[pl.BlockSpec(memory_space=pltpu.SMEM),
                  pl.BlockSpec((512, TQ), lambda h: (0, 0)),
                  pl.BlockSpec((2, TK, TQ), lambda h: (0, 0, 0))],
        out_specs=[pl.BlockSpec((1, 512, TQ), lambda h: (h, 0, 0)),
                   pl.BlockSpec((1, 2, TK, TQ), lambda h: (h, 0, 0, 0))],
        out_shape=[jax.ShapeDtypeStruct((N_HEADS, 512, TQ), jnp.float32),
                   jax.ShapeDtypeStruct((N_HEADS, 2, TK, TQ), jnp.float32)],
        name="bias_tables",
    )(rel_bias, jnp.asarray(idx_c), jnp.asarray(idx_n))


def _in_proj_kernel(x_ref, g_ref, wq_ref, wnat_ref, wvt_ref, wgt_ref, cw_ref,
                    qt_ref, kc_ref, vc_ref, ks_ref, kw_ref, vst_ref, vwt_ref, gt_ref,
                    oconv_ref, carry_ref, stage_ref):
    i = pl.program_id(1)
    tm = x_ref.shape[1]
    hb = _rms(x_ref[0], g_ref[...]).astype(jnp.bfloat16)

    qt_ref[0] = lax.dot_general(wq_ref[...], hb, _NT,
                                preferred_element_type=jnp.float32).astype(jnp.bfloat16)
    vt = lax.dot_general(wvt_ref[...], hb, _NT, preferred_element_type=jnp.float32)
    ones = jnp.ones((V_ROWS - HEAD_DIM, TK), jnp.bfloat16)
    for t in range(tm // TK):
        for g in range(N_KV):
            for out_ref, r0 in ((vst_ref, g * HEAD_DIM), (vwt_ref, KV_W + g * HEAD_DIM)):
                out_ref[0, t, g, 0:HEAD_DIM, :] = vt[r0:r0 + HEAD_DIM,
                                                      t * TK:(t + 1) * TK].astype(jnp.bfloat16)
                out_ref[0, t, g, HEAD_DIM:V_ROWS, :] = ones
    gl = lax.dot_general(wgt_ref[...], hb, _NT, preferred_element_type=jnp.float32)
    gt_ref[0] = jax.nn.sigmoid(gl)

    for out_ref, c0 in ((kc_ref, 0), (vc_ref, KV_W)):
        stage_ref[...] = jnp.dot(hb, wnat_ref[:, c0:c0 + KV_W], preferred_element_type=jnp.float32)
        for l in range(S_CMP):
            out_ref[0, :, l * KV_W:(l + 1) * KV_W] = stage_ref[pl.ds(l, tm // S_CMP, stride=S_CMP), :]
    k_s = jnp.dot(hb, wnat_ref[:, 256:384], preferred_element_type=jnp.float32)
    blk = (i * tm + lax.broadcasted_iota(jnp.int32, (tm, KV_W), 0)) // L_SLC
    col = lax.broadcasted_iota(jnp.int32, (tm, KV_W), 1)
    onehot = jnp.where(blk == col, 1.0, 0.0)
    ks_ref[0, :, 0:KV_W] = k_s.astype(jnp.bfloat16)
    ks_ref[0, :, KV_W:2 * KV_W] = onehot.astype(jnp.bfloat16)
    kw_ref[0] = jnp.dot(hb, wnat_ref[:, 384:512],
                        preferred_element_type=jnp.float32).astype(jnp.bfloat16)

    bg = jnp.dot(hb, wnat_ref[:, 512:1024], preferred_element_type=jnp.float32)
    cg = jnp.dot(hb, wnat_ref[:, 1024:1536], preferred_element_type=jnp.float32)
    xt = jnp.dot(hb, wnat_ref[:, 1536:2048], preferred_element_type=jnp.float32)
    z = cg * xt

    @pl.when(i == 0)
    def _():
        carry_ref[...] = jnp.zeros_like(carry_ref)

    prev1 = carry_ref[7:8, :]
    prev2 = carry_ref[6:7, :]
    row = lax.broadcasted_iota(jnp.int32, z.shape, 0)
    z1 = jnp.where(row == 0, prev1, pltpu.roll(z, 1, 0))
    z2 = jnp.where(row == 0, prev2, jnp.where(row == 1, prev1, pltpu.roll(z, 2, 0)))
    y = cw_ref[0:1, :] * z2 + cw_ref[1:2, :] * z1 + cw_ref[2:3, :] * z
    oconv_ref[0] = (bg * y).astype(jnp.bfloat16)
    carry_ref[...] = z[tm - 8:tm, :]


def _in_proj(x, g_pre, w_in, conv_w):
    B, S, D = x.shape
    tm = TM_IN
    wq = (w_in[:, 0:512].T * (HEAD_DIM ** -0.5 * LOG2E)).astype(jnp.bfloat16)
    wnat = jnp.concatenate([w_in[:, 512:896], w_in[:, 1024:1152],
                            w_in[:, 1304:2840]], axis=1).astype(jnp.bfloat16)
    wvt = jnp.concatenate([w_in[:, 896:1024], w_in[:, 1152:1280]], axis=1).T.astype(jnp.bfloat16)
    wgt = jnp.pad(w_in[:, 1280:1304], ((0, 0), (0, 8))).T.astype(jnp.bfloat16)
    nt = S // TK
    const = lambda b, i: (0, 0)
    outs = pl.pallas_call(
        _in_proj_kernel,
        grid=(B, S // tm),
        in_specs=[pl.BlockSpec((1, tm, D), lambda b, i: (b, i, 0)),
                  pl.BlockSpec((1, D), const),
                  pl.BlockSpec((512, D), const),
                  pl.BlockSpec((D, 2048), const),
                  pl.BlockSpec((256, D), const),
                  pl.BlockSpec((32, D), const),
                  pl.BlockSpec((3, D_CONV), const)],
        out_specs=[pl.BlockSpec((1, 512, tm), lambda b, i: (b, 0, i)),
                   pl.BlockSpec((1, tm // S_CMP, S_CMP * KV_W), lambda b, i: (b, i, 0)),
                   pl.BlockSpec((1, tm // S_CMP, S_CMP * KV_W), lambda b, i: (b, i, 0)),
                   pl.BlockSpec((1, tm, 256), lambda b, i: (b, i, 0)),
                   pl.BlockSpec((1, tm, 128), lambda b, i: (b, i, 0)),
                   pl.BlockSpec((1, tm // TK, N_KV, V_ROWS, TK), lambda b, i: (b, i, 0, 0, 0)),
                   pl.BlockSpec((1, tm // TK, N_KV, V_ROWS, TK), lambda b, i: (b, i, 0, 0, 0)),
                   pl.BlockSpec((1, 32, tm), lambda b, i: (b, 0, i)),
                   pl.BlockSpec((1, tm, D_CONV), lambda b, i: (b, i, 0))],
        out_shape=[jax.ShapeDtypeStruct((B, 512, S), jnp.bfloat16),
                   jax.ShapeDtypeStruct((B, S // S_CMP, S_CMP * KV_W), jnp.float32),
                   jax.ShapeDtypeStruct((B, S // S_CMP, S_CMP * KV_W), jnp.float32),
                   jax.ShapeDtypeStruct((B, S, 256), jnp.bfloat16),
                   jax.ShapeDtypeStruct((B, S, 128), jnp.bfloat16),
                   jax.ShapeDtypeStruct((B, nt, N_KV, V_ROWS, TK), jnp.bfloat16),
                   jax.ShapeDtypeStruct((B, nt, N_KV, V_ROWS, TK), jnp.bfloat16),
                   jax.ShapeDtypeStruct((B, 32, S), jnp.float32),
                   jax.ShapeDtypeStruct((B, S, D_CONV), jnp.bfloat16)],
        scratch_shapes=[pltpu.VMEM((8, D_CONV), jnp.float32),
                        pltpu.VMEM((tm, KV_W), jnp.float32)],
        compiler_params=pltpu.CompilerParams(
            dimension_semantics=("arbitrary", "arbitrary"), vmem_limit_bytes=VMEM_LIMIT),
        name="in_proj",
    )(x, g_pre.reshape(1, D), wq, wnat, wvt, wgt, conv_w)
    return outs


def _compress_kernel(ck_ref, cv_ref, pek_ref, pev_ref, wk1_ref, wk2_ref, wv1_ref, wv2t_ref,
                     ovl_ref, kc_ref, vct_ref):
    def hidden(c_ref, pe_ref, w1_ref):
        c = c_ref[0]
        a = jnp.dot((c + pe_ref[0:1, :]).astype(jnp.bfloat16), w1_ref[0],
                    preferred_element_type=jnp.float32)
        b = jnp.dot((c + pe_ref[1:2, :]).astype(jnp.bfloat16), w1_ref[1],
                    preferred_element_type=jnp.float32)
        return _gelu(a + pltpu.roll(b, N_CMP_PAD - 1, 0)).astype(jnp.bfloat16)

    hk = hidden(ck_ref, pek_ref, wk1_ref)
    hv = hidden(cv_ref, pev_ref, wv1_ref)
    for g in range(N_KV):
        cols = slice(g * CMP_HIDDEN, (g + 1) * CMP_HIDDEN)
        kc_ref[0, g] = jnp.dot(hk[:, cols], wk2_ref[...],
                               preferred_element_type=jnp.float32).astype(jnp.bfloat16)
        vct_ref[0, g, 0:HEAD_DIM, :] = lax.dot_general(
            wv2t_ref[...], hv[:, cols], _NT, preferred_element_type=jnp.float32).astype(jnp.bfloat16)
        vct_ref[0, g, HEAD_DIM:V_ROWS, :] = jnp.ones((V_ROWS - HEAD_DIM, N_CMP_PAD), jnp.bfloat16)
        vct_ref[0, g, V_ROWS:C_ROWS, :] = ovl_ref[...]


def _compress(kc_in, vc_in, pe_k, pe_v, wk1, wk2, wv1, wv2):
    B, n_str, width = kc_in.shape

    def both_groups(w1):
        w = w1.reshape(2, S_CMP, HEAD_DIM, CMP_HIDDEN)
        w = jnp.einsum('aldc,gh->algdhc', w, jnp.eye(N_KV, dtype=w.dtype))
        return w.reshape(2, width, N_KV * CMP_HIDDEN).astype(jnp.bfloat16)

    def pe_rows(pe):
        return jnp.broadcast_to(pe.reshape(2, S_CMP, 1, HEAD_DIM),
                                (2, S_CMP, N_KV, HEAD_DIM)).reshape(2, width)

    const2 = lambda b: (0, 0)
    const3 = lambda b: (0, 0, 0)
    return pl.pallas_call(
        _compress_kernel,
        grid=(B,),
        in_specs=[pl.BlockSpec((1, n_str, width), lambda b: (b, 0, 0)),
                  pl.BlockSpec((1, n_str, width), lambda b: (b, 0, 0)),
                  pl.BlockSpec((2, width), const2),
                  pl.BlockSpec((2, width), const2),
                  pl.BlockSpec((2, width, N_KV * CMP_HIDDEN), const3),
                  pl.BlockSpec((CMP_HIDDEN, HEAD_DIM), const2),
                  pl.BlockSpec((2, width, N_KV * CMP_HIDDEN), const3),
                  pl.BlockSpec((HEAD_DIM, CMP_HIDDEN), const2),
                  pl.BlockSpec((C_ROWS - V_ROWS, n_str), const2)],
        out_specs=[pl.BlockSpec((1, N_KV, n_str, HEAD_DIM), lambda b: (b, 0, 0, 0)),
                   pl.BlockSpec((1, N_KV, C_ROWS, n_str), lambda b: (b, 0, 0, 0))],
        out_shape=[jax.ShapeDtypeStruct((B, N_KV, n_str, HEAD_DIM), jnp.bfloat16),
                   jax.ShapeDtypeStruct((B, N_KV, C_ROWS, n_str), jnp.bfloat16)],
        compiler_params=pltpu.CompilerParams(
            dimension_semantics=("arbitrary",), vmem_limit_bytes=VMEM_LIMIT),
        name="compress",
    )(kc_in, vc_in, pe_rows(pe_k), pe_rows(pe_v), both_groups(wk1), wk2.astype(jnp.bfloat16),
      both_groups(wv1), wv2.T.astype(jnp.bfloat16), _overlap_t(n_str * S_CMP))


def _overlap_t(S):
    n_slc = S // L_SLC
    c_start = np.arange(N_CMP_PAD) * S_CMP
    s_start = np.arange(n_slc) * L_SLC
    ov = np.clip(np.minimum(c_start[:, None] + L_CMP, s_start[None, :] + L_SLC)
                 - np.maximum(c_start[:, None], s_start[None, :]), 0, None).astype(np.float32) / L_CMP
    ov[(S - L_CMP) // S_CMP + 1:, :] = 0.0
    return jnp.asarray(ov.T, dtype=jnp.bfloat16)


def _select_blocks(score, allowed, tri):
    w = score
    cnt = jnp.zeros((1, TQ), jnp.float32)
    thr = jnp.zeros((1, TQ), jnp.float32)
    n_gt = jnp.zeros((1, TQ), jnp.float32)
    for _ in range(N_SEL):
        mx = jnp.max(w, axis=0, keepdims=True)
        eq = w == mx
        c = jnp.sum(jnp.where(eq, 1.0, 0.0), axis=0, keepdims=True)
        cross = (cnt < float(N_SEL)) & (cnt + c >= float(N_SEL))
        thr = jnp.where(cross, mx, thr)
        n_gt = jnp.where(cross, cnt, n_gt)
        cnt = cnt + c
        w = jnp.where(eq, -jnp.inf, w)
    at_thr = score == thr
    earlier = jnp.dot(tri, jnp.where(at_thr, 1.0, 0.0).astype(jnp.bfloat16),
                      preferred_element_type=jnp.float32)
    return allowed & ((score > thr) | (at_thr & (earlier + n_gt < float(N_SEL))))


def _nsa_attn_kernel(qt_ref, kc_ref, vc_ref, tabc_ref, tri_ref, ks_ref, kw_ref, vst_ref, vwt_ref,
                     near_ref, edge_ref, gt_ref, o_ref,
                     m_ref, acc_ref, mw_ref, accw_ref, s_ref, tmax_ref, sel_ref, oc_ref):
    i = pl.program_id(1)
    zeros64 = jnp.zeros((HEAD_DIM, TQ), jnp.bfloat16)
    off1 = jnp.where(i >= 1, 0.0, MASK_NEG)
    off2 = jnp.where(i >= 2, 0.0, MASK_NEG)
    j1 = jnp.maximum(i - 1, 0)
    j2 = jnp.maximum(i - 2, 0)
    dot = functools.partial(jnp.dot, preferred_element_type=jnp.float32)

    def q_window(h):
        qh = qt_ref[0, h * HEAD_DIM:(h + 1) * HEAD_DIM, :]
        return jnp.concatenate([qh, zeros64] if h // REP == 0 else [zeros64, qh], axis=0)

    def q_selected(h):
        return jnp.concatenate([q_window(h), sel_ref[h // REP]], axis=0)

    def stage_logits(slot, k_ref, j, q_of, bias_of=None):
        k = k_ref[0, pl.ds(pl.multiple_of(j * TK, TK), TK), :]
        for h in range(N_HEADS):
            s = dot(k, q_of(h))
            if bias_of is not None:
                s = s + bias_of(h)
            s_ref[slot, h] = s
            tmax_ref[slot, h:h + 1, :] = jnp.max(s, axis=0, keepdims=True)

    def accumulate(slot, v_ref, j, state):
        ms_ref, as_ref = state
        for h in range(N_HEADS):
            m = ms_ref[h:h + 1, :]
            m_new = jnp.maximum(m, tmax_ref[slot, h:h + 1, :])
            alpha = jnp.exp2(m - m_new)
            p = jnp.exp2(s_ref[slot, h] - m_new).astype(jnp.bfloat16)
            as_ref[h] = alpha * as_ref[h] + dot(v_ref[0, j, h // REP], p)
            ms_ref[h:h + 1, :] = m_new

    for ref in (m_ref, mw_ref):
        ref[...] = jnp.full(ref.shape, MASK_NEG, jnp.float32)
    for ref in (acc_ref, accw_ref):
        ref[...] = jnp.zeros(ref.shape, jnp.float32)
    sel_state = (m_ref, acc_ref)
    win_state = (mw_ref, accw_ref)
    near0 = lambda h: near_ref[h, 0]
    near1 = lambda h: near_ref[h, 1] + off1
    edge = lambda h: edge_ref[...] + off2

    off_c = pl.multiple_of(240 - 16 * i, 16)
    for h in range(N_HEADS):
        s = (dot(kc_ref[0, h // REP], qt_ref[0, h * HEAD_DIM:(h + 1) * HEAD_DIM, :])
             + tabc_ref[h, pl.ds(off_c, N_CMP_PAD), :])
        s_ref[0, h] = s
        tmax_ref[0, h:h + 1, :] = jnp.max(s, axis=0, keepdims=True)
    stage_logits(1, kw_ref, i, q_window, near0)

    t_row = i * TQ + lax.broadcasted_iota(jnp.int32, (1, TQ), 1)
    any_cmp = jnp.where(t_row >= L_CMP - 1, 1.0, 0.0)
    n_blk = tri_ref.shape[0]
    j_idx = lax.broadcasted_iota(jnp.int32, (n_blk, TQ), 0)
    cur = (i * TQ + lax.broadcasted_iota(jnp.int32, (n_blk, TQ), 1)) // L_SLC
    forced = (j_idx == 0) | (j_idx == cur) | (j_idx == cur - 1)
    allowed = j_idx <= cur
    bonus = jnp.where(forced, FORCE_BONUS, 0.0)
    for g in range(N_KV):
        imp = jnp.zeros((n_blk, TQ), jnp.float32)
        for h in range(g * REP, (g + 1) * REP):
            p = jnp.exp2(s_ref[0, h] - tmax_ref[0, h:h + 1, :]).astype(jnp.bfloat16)
            a = dot(vc_ref[0, g], p)
            inv = any_cmp / a[HEAD_DIM:HEAD_DIM + 1, :]
            oc_ref[h * HEAD_DIM:(h + 1) * HEAD_DIM, :] = a[0:HEAD_DIM, :] * inv
            imp = imp + a[V_ROWS:V_ROWS + n_blk, :] * inv
        score = jnp.where(allowed, imp + bonus, MASK_NEG)
        sel = _select_blocks(score, allowed, tri_ref[...])
        sel_ref[g, 0:n_blk, :] = jnp.where(sel, 0.0, MASK_NEG).astype(jnp.bfloat16)
        sel_ref[g, n_blk:2 * n_blk, :] = jnp.zeros((n_blk, TQ), jnp.bfloat16)

    accumulate(1, vwt_ref, i, win_state)
    stage_logits(0, ks_ref, i, q_selected, near0)
    stage_logits(1, kw_ref, j1, q_window, near1)
    accumulate(0, vst_ref, i, sel_state)
    stage_logits(0, ks_ref, j1, q_selected, near1)
    accumulate(1, vwt_ref, j1, win_state)
    stage_logits(1, kw_ref, j2, q_window, edge)
    accumulate(0, vst_ref, j1, sel_state)

    n_far = j1
    last = jnp.maximum(n_far - 1, 0)
    stage_logits(0, ks_ref, 0, q_selected)
    accumulate(1, vwt_ref, j2, win_state)

    def far_pair(c, carry):
        t = 2 * c
        stage_logits(1, ks_ref, t + 1, q_selected)
        accumulate(0, vst_ref, t, sel_state)
        stage_logits(0, ks_ref, jnp.minimum(t + 2, last), q_selected)
        accumulate(1, vst_ref, t + 1, sel_state)
        return carry

    lax.fori_loop(0, n_far // 2, far_pair, 0)

    @pl.when(n_far % 2 == 1)
    def _():
        accumulate(0, vst_ref, last, sel_state)

    for pair in range(N_HEADS // 2):
        outs = []
        for h in (2 * pair, 2 * pair + 1):
            rows = slice(h * HEAD_DIM, (h + 1) * HEAD_DIM)
            o_s = acc_ref[h, 0:HEAD_DIM, :] / acc_ref[h, HEAD_DIM:HEAD_DIM + 1, :]
            o_w = accw_ref[h, 0:HEAD_DIM, :] / accw_ref[h, HEAD_DIM:HEAD_DIM + 1, :]
            outs.append(gt_ref[0, 3 * h:3 * h + 1, :] * oc_ref[rows, :]
                        + gt_ref[0, 3 * h + 1:3 * h + 2, :] * o_s
                        + gt_ref[0, 3 * h + 2:3 * h + 3, :] * o_w)
        o_pair = jnp.concatenate(outs, axis=0)
        o_ref[0, :, pair * 128:(pair + 1) * 128] = o_pair.T.astype(jnp.bfloat16)


def _nsa_attn(qt, kc, vc, tab_c, ks, kw, vst, vwt, near, gt):
    B, _, S = qt.shape
    ni = S // TQ
    nt = S // TK
    n_blk = S // L_SLC
    key = np.arange(TK)[:, None]
    row = np.arange(TQ)[None, :]
    edge = jnp.asarray(np.where(2 * TK + row - key < WINDOW, 0.0, MASK_NEG).astype(np.float32))
    tri = jnp.asarray(np.tril(np.ones((n_blk, n_blk), np.float32), -1), dtype=jnp.bfloat16)
    per_batch = lambda shape: pl.BlockSpec((1,) + shape, lambda b, i: (b,) + (0,) * len(shape))
    const = lambda shape: pl.BlockSpec(shape, lambda b, i: (0,) * len(shape),
                                       pipeline_mode=pl.Buffered(1))
    return pl.pallas_call(
        _nsa_attn_kernel,
        grid=(B, ni),
        in_specs=[pl.BlockSpec((1, D_NSA, TQ), lambda b, i: (b, 0, i)),
                  per_batch((N_KV, N_CMP_PAD, HEAD_DIM)),
                  per_batch((N_KV, C_ROWS, N_CMP_PAD)),
                  const((N_HEADS, 512, TQ)),
                  const((n_blk, n_blk)),
                  per_batch((S, 256)),
                  per_batch((S, 128)),
                  per_batch((nt, N_KV, V_ROWS, TK)),
                  per_batch((nt, N_KV, V_ROWS, TK)),
                  const((N_HEADS, 2, TK, TQ)),
                  const((TK, TQ)),
                  pl.BlockSpec((1, 32, TQ), lambda b, i: (b, 0, i))],
        out_specs=pl.BlockSpec((1, TQ, D_NSA), lambda b, i: (b, i, 0)),
        out_shape=jax.ShapeDtypeStruct((B, S, D_NSA), jnp.bfloat16),
        scratch_shapes=[pltpu.VMEM((N_HEADS, TQ), jnp.float32),
                        pltpu.VMEM((N_HEADS, V_ROWS, TQ), jnp.float32),
                        pltpu.VMEM((N_HEADS, TQ), jnp.float32),
                        pltpu.VMEM((N_HEADS, V_ROWS, TQ), jnp.float32),
                        pltpu.VMEM((2, N_HEADS, TK, TQ), jnp.float32),
                        pltpu.VMEM((2, N_HEADS, TQ), jnp.float32),
                        pltpu.VMEM((N_KV, 128, TQ), jnp.bfloat16),
                        pltpu.VMEM((D_NSA, TQ), jnp.float32)],
        compiler_params=pltpu.CompilerParams(
            dimension_semantics=("arbitrary", "arbitrary"), vmem_limit_bytes=VMEM_LIMIT),
        name="nsa_attn",
    )(qt, kc, vc, tab_c, tri, ks, kw, vst, vwt, near, edge, gt)


def _out_ffn_kernel(on_ref, oc_ref, x_ref, wo_ref, gpost_ref, gffn_ref, wup_ref, cw_ref, wdn_ref,
                    g_ref, o_ref, carry_ref, act_ref):
    i = pl.program_id(1)
    tm = x_ref.shape[1]
    y = jnp.dot(on_ref[0], wo_ref[0:D_NSA, :], preferred_element_type=jnp.float32)
    y = y + jnp.dot(oc_ref[0], wo_ref[D_NSA:D_NSA + D_CONV, :], preferred_element_type=jnp.float32)
    x1 = x_ref[0] + _rms(y, gpost_ref[...])
    hb = _rms(x1, gffn_ref[...]).astype(jnp.bfloat16)

    @pl.when(i == 0)
    def _():
        carry_ref[...] = jnp.zeros_like(carry_ref)

    row = lax.broadcasted_iota(jnp.int32, (tm, FF_CHUNK), 0)
    is0 = row == 0
    is1 = row == 1

    def conv(z, c0):
        cols = slice(c0, c0 + FF_CHUNK)
        prev1 = carry_ref[7:8, cols]
        prev2 = carry_ref[6:7, cols]
        z1 = jnp.where(is0, prev1, pltpu.roll(z, 1, 0))
        z2 = jnp.where(is0, prev2, jnp.where(is1, prev1, pltpu.roll(z, 2, 0)))
        carry_ref[:, cols] = z[tm - 8:tm, :]
        return cw_ref[0:1, cols] * z2 + cw_ref[1:2, cols] * z1 + cw_ref[2:3, cols] * z

    for c in range(D_FF // FF_CHUNK):
        g0 = c * FF_CHUNK
        u0 = D_FF + c * FF_CHUNK
        zg = jnp.dot(hb, wup_ref[:, g0:g0 + FF_CHUNK], preferred_element_type=jnp.float32)
        zu = jnp.dot(hb, wup_ref[:, u0:u0 + FF_CHUNK], preferred_element_type=jnp.float32)
        act = _gelu(conv(zg, g0)) * conv(zu, u0)
        act_ref[:, g0:g0 + FF_CHUNK] = act.astype(jnp.bfloat16)
    y2 = jnp.dot(act_ref[...], wdn_ref[...], preferred_element_type=jnp.float32)
    o_ref[0] = x1 + _rms(y2, g_ref[...])


def _out_ffn(o_nsa, o_conv, x, w_out, g_post, g_ffn, w_up, conv_w, w_down, g_ffn_post):
    B, S, D = x.shape
    tm = TM_FFN
    const = lambda b, i: (0, 0)
    single = pl.Buffered(1)
    row_tile = lambda width: pl.BlockSpec((1, tm, width), lambda b, i: (b, i, 0))
    weight = lambda shape: pl.BlockSpec(shape, const, pipeline_mode=single)
    return pl.pallas_call(
        _out_ffn_kernel,
        grid=(B, S // tm),
        in_specs=[row_tile(D_NSA), row_tile(D_CONV), row_tile(D),
                  weight((D, D)), weight((1, D)), weight((1, D)),
                  weight((D, 2 * D_FF)), weight((3, 2 * D_FF)), weight((D_FF, D)), weight((1, D))],
        out_specs=row_tile(D),
        out_shape=jax.ShapeDtypeStruct((B, S, D), jnp.float32),
        scratch_shapes=[pltpu.VMEM((8, 2 * D_FF), jnp.float32),
                        pltpu.VMEM((tm, D_FF), jnp.bfloat16)],
        compiler_params=pltpu.CompilerParams(
            dimension_semantics=("arbitrary", "arbitrary"), vmem_limit_bytes=VMEM_LIMIT),
        name="out_ffn",
    )(o_nsa, o_conv, x, w_out.astype(jnp.bfloat16), g_post.reshape(1, D), g_ffn.reshape(1, D),
      w_up.astype(jnp.bfloat16), conv_w, w_down.astype(jnp.bfloat16), g_ffn_post.reshape(1, D))


def kernel(x, norm_mix_pre, norm_mix_post, norm_ffn_pre, norm_ffn_post, w_in, pe_cmp_k, pe_cmp_v,
           w_cmp_k1, w_cmp_k2, w_cmp_v1, w_cmp_v2, rel_bias, conv_mix_w, w_out, w_ffn_up,
           ffn_conv_w, w_ffn_down):
    B, S, D = x.shape
    assert (S, D) == (4096, D_MODEL) and norm_mix_pre.shape[0] == 1
    tab_c, near = _bias_tables(rel_bias)
    for l in range(norm_mix_pre.shape[0]):
        qt, kc_in, vc_in, ks, kw, vst, vwt, gt, o_conv = _in_proj(
            x, norm_mix_pre[l], w_in[l], conv_mix_w[l])
        kc, vc = _compress(kc_in, vc_in, pe_cmp_k[l], pe_cmp_v[l], w_cmp_k1[l], w_cmp_k2[l],
                           w_cmp_v1[l], w_cmp_v2[l])
        o_nsa = _nsa_attn(qt, kc, vc, tab_c, ks, kw, vst, vwt, near, gt)
        x = _out_ffn(o_nsa, o_conv, x, w_out[l], norm_mix_post[l], norm_ffn_pre[l],
                     w_ffn_up[l], ffn_conv_w[l], w_ffn_down[l], norm_ffn_post[l])
    return x
```

```python
import functools
import math

import numpy as np
import jax
import jax.numpy as jnp
from jax import lax
from jax.experimental import pallas as pl
from jax.experimental.pallas import tpu as pltpu

D_MODEL = 1024
D_NSA = 512
D_CONV = 512
HEAD_DIM = 64
N_HEADS = 8
N_KV = 2
REP = 4
KV_W = 128
N_GATES = 24
L_CMP = 32
S_CMP = 16
L_SLC = 64
N_SEL = 16
WINDOW = 512
CMP_HIDDEN = 256
D_FF = 2816
N_BUCKETS = 32
MAX_DIST = 128
RMS_EPS = 1e-6
MASK_NEG = -1e30
FORCE_BONUS = 1e4
LOG2E = math.log2(math.e)

TQ = 256
TK = 256
V_ROWS = 80
C_ROWS = V_ROWS + 64
N_CMP_PAD = 256
TM_IN = 1024
TM_FFN = 1024
FF_CHUNK = 256
VMEM_LIMIT = 56 * 1024 * 1024

_NT = (((1,), (1,)), ((), ()))


def _bucket_np(d):
    max_exact = N_BUCKETS // 2
    d = np.maximum(d, 0)
    df = np.maximum(d, 1).astype(np.float32)
    large = max_exact + (np.log(df / max_exact) / math.log(MAX_DIST / max_exact)
                         * (N_BUCKETS - max_exact)).astype(np.int32)
    return np.where(d < max_exact, d, np.minimum(large, N_BUCKETS - 1)).astype(np.int32)


def _bucket_or_mask(d):
    return np.where(d >= 0, _bucket_np(d), -1).astype(np.int32)


def _rms(x, g):
    return x * lax.rsqrt(jnp.mean(x * x, axis=-1, keepdims=True) + RMS_EPS) * g


def _gelu(x):
    return jax.nn.gelu(x, approximate=True)


def _band_box(idx):
    rows, cols = np.nonzero((idx >= 0) & (idx < N_BUCKETS - 1))
    r0, r1, c0, c1 = int(rows.min()), int(rows.max()) + 1, int(cols.min()), int(cols.max()) + 1
    return r0 // 8 * 8, -(-r1 // 8) * 8, c0 // 128 * 128, -(-c1 // 128) * 128


def _bias_tables_kernel(boxes, rb_ref, idx_c_ref, idx_n_ref, tab_c_ref, tab_n_ref):
    h = pl.program_id(0)
    last = rb_ref[h, N_BUCKETS - 1]

    def fill(idx, out, box):
        out[...] = jnp.where(idx < 0, MASK_NEG, 0.0)
        r0, r1, c0, c1 = box
        sub = idx[r0:r1, c0:c1]
        val = jnp.zeros(sub.shape, jnp.float32)
        for b in range(N_BUCKETS - 1):
            val = jnp.where(sub == b, rb_ref[h, b] - last, val)
        out[r0:r1, c0:c1] = jnp.where(sub < 0, MASK_NEG, val * LOG2E)

    fill(idx_c_ref[...], tab_c_ref.at[0], boxes[0])
    for k in range(idx_n_ref.shape[0]):
        fill(idx_n_ref[k], tab_n_ref.at[0, k], boxes[1 + k])


def _bias_tables(rel_bias):
    u = np.arange(512)[:, None] - 240
    row = np.arange(TQ)[None, :]
    d_c = row - S_CMP * u - (L_CMP - 1)
    idx_c = _bucket_or_mask(d_c)
    key = np.arange(TK)[:, None]
    idx_n = np.stack([_bucket_or_mask(dd + row - key) for dd in (0, TK)])
    boxes = tuple(_band_box(t) for t in (idx_c, *idx_n))
    return pl.pallas_call(
        functools.partial(_bias_tables_kernel, boxes),
        grid=(N_HEADS,),
        in_specs=[pl.BlockSpec(memory_space=pltpu.SMEM),
                  pl.BlockSpec((512, TQ), lambda h: (0, 0)),
                  pl.BlockSpec((2, TK, TQ), lambda h: (0, 0, 0))],
        out_specs=[pl.BlockSpec((1, 512, TQ), lambda h: (h, 0, 0)),
                   pl.BlockSpec((1, 2, TK, TQ), lambda h: (h, 0, 0, 0))],
        out_shape=[jax.ShapeDtypeStruct((N_HEADS, 512, TQ), jnp.float32),
                   jax.ShapeDtypeStruct((N_HEADS, 2, TK, TQ), jnp.float32)],
        name="bias_tables",
    )(rel_bias, jnp.asarray(idx_c), jnp.asarray(idx_n))


def _in_proj_kernel(x_ref, g_ref, wq_ref, wnat_ref, wvt_ref, wgt_ref, cw_ref,
                    qt_ref, kc_ref, vc_ref, ks_ref, kw_ref, vst_ref, vwt_ref, gt_ref,
                    oconv_ref, carry_ref, stage_ref):
    i = pl.program_id(1)
    tm = x_ref.shape[1]
    hb = _rms(x_ref[0], g_ref[...]).astype(jnp.bfloat16)

    qt_ref[0] = lax.dot_general(wq_ref[...], hb, _NT,
                                preferred_element_type=jnp.float32).astype(jnp.bfloat16)
    vt = lax.dot_general(wvt_ref[...], hb, _NT, preferred_element_type=jnp.float32)
    ones = jnp.ones((V_ROWS - HEAD_DIM, TK), jnp.bfloat16)
    for t in range(tm // TK):
        for g in range(N_KV):
            for out_ref, r0 in ((vst_ref, g * HEAD_DIM), (vwt_ref, KV_W + g * HEAD_DIM)):
                out_ref[0, t, g, 0:HEAD_DIM, :] = vt[r0:r0 + HEAD_DIM,
                                                      t * TK:(t + 1) * TK].astype(jnp.bfloat16)
                out_ref[0, t, g, HEAD_DIM:V_ROWS, :] = ones
    gl = lax.dot_general(wgt_ref[...], hb, _NT, preferred_element_type=jnp.float32)
    gt_ref[0] = jax.nn.sigmoid(gl)

    kv_c = jnp.dot(hb, wnat_ref[:, 0:2 * KV_W], preferred_element_type=jnp.float32)
    for a, out_ref in enumerate((kc_ref, vc_ref)):
        stage_ref[a] = kv_c[:, a * KV_W:(a + 1) * KV_W]
        for l in range(S_CMP):
            rows = stage_ref[a, pl.ds(l, tm // S_CMP, stride=S_CMP), :]
            for g in range(N_KV):
                out_ref[0, g, :, l * HEAD_DIM:(l + 1) * HEAD_DIM] = rows[
                    :, g * HEAD_DIM:(g + 1) * HEAD_DIM]
    k_sw = jnp.dot(hb, wnat_ref[:, 256:512], preferred_element_type=jnp.float32)
    blk = (i * tm + lax.broadcasted_iota(jnp.int32, (tm, KV_W), 0)) // L_SLC
    col = lax.broadcasted_iota(jnp.int32, (tm, KV_W), 1)
    onehot = jnp.where(blk == col, 1.0, 0.0)
    ks_ref[0, :, 0:KV_W] = k_sw[:, 0:KV_W].astype(jnp.bfloat16)
    ks_ref[0, :, KV_W:2 * KV_W] = onehot.astype(jnp.bfloat16)
    kw_ref[0] = k_sw[:, KV_W:2 * KV_W].astype(jnp.bfloat16)

    bg = jnp.dot(hb, wnat_ref[:, 512:1024], preferred_element_type=jnp.float32)
    cg = jnp.dot(hb, wnat_ref[:, 1024:1536], preferred_element_type=jnp.float32)
    xt = jnp.dot(hb, wnat_ref[:, 1536:2048], preferred_element_type=jnp.float32)
    z = cg * xt

    @pl.when(i == 0)
    def _():
        carry_ref[...] = jnp.zeros_like(carry_ref)

    prev1 = carry_ref[7:8, :]
    prev2 = carry_ref[6:7, :]
    row = lax.broadcasted_iota(jnp.int32, z.shape, 0)
    z1 = jnp.where(row == 0, prev1, pltpu.roll(z, 1, 0))
    z2 = jnp.where(row == 0, prev2, jnp.where(row == 1, prev1, pltpu.roll(z, 2, 0)))
    y = cw_ref[0:1, :] * z2 + cw_ref[1:2, :] * z1 + cw_ref[2:3, :] * z
    oconv_ref[0] = (bg * y).astype(jnp.bfloat16)
    carry_ref[...] = z[tm - 8:tm, :]


def _in_proj(x, g_pre, w_in, conv_w):
    B, S, D = x.shape
    tm = TM_IN
    wq = (w_in[:, 0:512].T * (HEAD_DIM ** -0.5 * LOG2E)).astype(jnp.bfloat16)
    wnat = jnp.concatenate([w_in[:, 512:896], w_in[:, 1024:1152],
                            w_in[:, 1304:2840]], axis=1).astype(jnp.bfloat16)
    wvt = jnp.concatenate([w_in[:, 896:1024], w_in[:, 1152:1280]], axis=1).T.astype(jnp.bfloat16)
    wgt = jnp.pad(w_in[:, 1280:1304], ((0, 0), (0, 8))).T.astype(jnp.bfloat16)
    nt = S // TK
    const = lambda b, i: (0, 0)
    outs = pl.pallas_call(
        _in_proj_kernel,
        grid=(B, S // tm),
        in_specs=[pl.BlockSpec((1, tm, D), lambda b, i: (b, i, 0)),
                  pl.BlockSpec((1, D), const),
                  pl.BlockSpec((512, D), const),
                  pl.BlockSpec((D, 2048), const),
                  pl.BlockSpec((256, D), const),
                  pl.BlockSpec((32, D), const),
                  pl.BlockSpec((3, D_CONV), const)],
        out_specs=[pl.BlockSpec((1, 512, tm), lambda b, i: (b, 0, i)),
                   pl.BlockSpec((1, N_KV, tm // S_CMP, S_CMP * HEAD_DIM), lambda b, i: (b, 0, i, 0)),
                   pl.BlockSpec((1, N_KV, tm // S_CMP, S_CMP * HEAD_DIM), lambda b, i: (b, 0, i, 0)),
                   pl.BlockSpec((1, tm, 256), lambda b, i: (b, i, 0)),
                   pl.BlockSpec((1, tm, 128), lambda b, i: (b, i, 0)),
                   pl.BlockSpec((1, tm // TK, N_KV, V_ROWS, TK), lambda b, i: (b, i, 0, 0, 0)),
                   pl.BlockSpec((1, tm // TK, N_KV, V_ROWS, TK), lambda b, i: (b, i, 0, 0, 0)),
                   pl.BlockSpec((1, 32, tm), lambda b, i: (b, 0, i)),
                   pl.BlockSpec((1, tm, D_CONV), lambda b, i: (b, i, 0))],
        out_shape=[jax.ShapeDtypeStruct((B, 512, S), jnp.bfloat16),
                   jax.ShapeDtypeStruct((B, N_KV, S // S_CMP, S_CMP * HEAD_DIM), jnp.float32),
                   jax.ShapeDtypeStruct((B, N_KV, S // S_CMP, S_CMP * HEAD_DIM), jnp.float32),
                   jax.ShapeDtypeStruct((B, S, 256), jnp.bfloat16),
                   jax.ShapeDtypeStruct((B, S, 128), jnp.bfloat16),
                   jax.ShapeDtypeStruct((B, nt, N_KV, V_ROWS, TK), jnp.bfloat16),
                   jax.ShapeDtypeStruct((B, nt, N_KV, V_ROWS, TK), jnp.bfloat16),
                   jax.ShapeDtypeStruct((B, 32, S), jnp.float32),
                   jax.ShapeDtypeStruct((B, S, D_CONV), jnp.bfloat16)],
        scratch_shapes=[pltpu.VMEM((8, D_CONV), jnp.float32),
                        pltpu.VMEM((2, tm, KV_W), jnp.float32)],
        compiler_params=pltpu.CompilerParams(
            dimension_semantics=("arbitrary", "arbitrary"), vmem_limit_bytes=VMEM_LIMIT),
        name="in_proj",
    )(x, g_pre.reshape(1, D), wq, wnat, wvt, wgt, conv_w)
    return outs


def _compress_kernel(ck_ref, cv_ref, pek_ref, pev_ref, wk1_ref, wk2_ref, wv1_ref, wv2t_ref,
                     ovl_ref, kc_ref, vct_ref):
    def hidden(c_ref, pe_ref, w1_ref):
        c = c_ref[0, 0]
        a = jnp.dot((c + pe_ref[0:1, :]).astype(jnp.bfloat16), w1_ref[0].astype(jnp.bfloat16),
                    preferred_element_type=jnp.float32)
        b = jnp.dot((c + pe_ref[1:2, :]).astype(jnp.bfloat16), w1_ref[1].astype(jnp.bfloat16),
                    preferred_element_type=jnp.float32)
        return _gelu(a + pltpu.roll(b, N_CMP_PAD - 1, 0)).astype(jnp.bfloat16)

    kc_ref[0, 0] = jnp.dot(hidden(ck_ref, pek_ref, wk1_ref), wk2_ref[...].astype(jnp.bfloat16),
                           preferred_element_type=jnp.float32).astype(jnp.bfloat16)
    vct_ref[0, 0, 0:HEAD_DIM, :] = lax.dot_general(
        wv2t_ref[...].astype(jnp.bfloat16), hidden(cv_ref, pev_ref, wv1_ref), _NT,
        preferred_element_type=jnp.float32).astype(jnp.bfloat16)
    vct_ref[0, 0, HEAD_DIM:V_ROWS, :] = jnp.ones((V_ROWS - HEAD_DIM, N_CMP_PAD), jnp.bfloat16)
    vct_ref[0, 0, V_ROWS:C_ROWS, :] = ovl_ref[...]


def _compress(kc_in, vc_in, pe_k, pe_v, wk1, wk2, wv1, wv2):
    B, _, n_str, half = kc_in.shape
    const2 = lambda b, g: (0, 0)
    const3 = lambda b, g: (0, 0, 0)
    return pl.pallas_call(
        _compress_kernel,
        grid=(B, N_KV),
        in_specs=[pl.BlockSpec((1, 1, n_str, half), lambda b, g: (b, g, 0, 0)),
                  pl.BlockSpec((1, 1, n_str, half), lambda b, g: (b, g, 0, 0)),
                  pl.BlockSpec((2, half), const2),
                  pl.BlockSpec((2, half), const2),
                  pl.BlockSpec((2, half, CMP_HIDDEN), const3),
                  pl.BlockSpec((CMP_HIDDEN, HEAD_DIM), const2),
                  pl.BlockSpec((2, half, CMP_HIDDEN), const3),
                  pl.BlockSpec((HEAD_DIM, CMP_HIDDEN), const2),
                  pl.BlockSpec((C_ROWS - V_ROWS, n_str), const2)],
        out_specs=[pl.BlockSpec((1, 1, n_str, HEAD_DIM), lambda b, g: (b, g, 0, 0)),
                   pl.BlockSpec((1, 1, C_ROWS, n_str), lambda b, g: (b, g, 0, 0))],
        out_shape=[jax.ShapeDtypeStruct((B, N_KV, n_str, HEAD_DIM), jnp.bfloat16),
                   jax.ShapeDtypeStruct((B, N_KV, C_ROWS, n_str), jnp.bfloat16)],
        compiler_params=pltpu.CompilerParams(
            dimension_semantics=("arbitrary", "arbitrary"), vmem_limit_bytes=VMEM_LIMIT),
        name="compress",
    )(kc_in, vc_in, pe_k.reshape(2, half), pe_v.reshape(2, half),
      wk1.reshape(2, half, CMP_HIDDEN), wk2, wv1.reshape(2, half, CMP_HIDDEN), wv2.T,
      _overlap_t(n_str * S_CMP))


def _overlap_t(S):
    n_slc = S // L_SLC
    c_start = np.arange(N_CMP_PAD) * S_CMP
    s_start = np.arange(n_slc) * L_SLC
    ov = np.clip(np.minimum(c_start[:, None] + L_CMP, s_start[None, :] + L_SLC)
                 - np.maximum(c_start[:, None], s_start[None, :]), 0, None).astype(np.float32) / L_CMP
    ov[(S - L_CMP) // S_CMP + 1:, :] = 0.0
    return jnp.asarray(ov.T, dtype=jnp.bfloat16)


def _select_blocks(score, allowed, tri):
    w = score
    cnt = jnp.zeros((1, TQ), jnp.float32)
    thr = jnp.zeros((1, TQ), jnp.float32)
    n_gt = jnp.zeros((1, TQ), jnp.float32)
    for _ in range(N_SEL):
        mx = jnp.max(w, axis=0, keepdims=True)
        eq = w == mx
        c = jnp.sum(jnp.where(eq, 1.0, 0.0), axis=0, keepdims=True)
        cross = (cnt < float(N_SEL)) & (cnt + c >= float(N_SEL))
        thr = jnp.where(cross, mx, thr)
        n_gt = jnp.where(cross, cnt, n_gt)
        cnt = cnt + c
        w = jnp.where(eq, -jnp.inf, w)
    at_thr = score == thr
    earlier = jnp.dot(tri, jnp.where(at_thr, 1.0, 0.0).astype(jnp.bfloat16),
                      preferred_element_type=jnp.float32)
    return allowed & ((score > thr) | (at_thr & (earlier + n_gt < float(N_SEL))))


def _nsa_attn_kernel(qt_ref, kc_ref, vc_ref, tabc_ref, tri_ref, ks_ref, kw_ref, vst_ref, vwt_ref,
                     near_ref, edge_ref, gt_ref, o_ref,
                     m_ref, acc_ref, mw_ref, accw_ref, s_ref, tmax_ref, sel_ref, oc_ref):
    i = pl.program_id(1)
    zeros64 = jnp.zeros((HEAD_DIM, TQ), jnp.bfloat16)
    off1 = jnp.where(i >= 1, 0.0, MASK_NEG)
    off2 = jnp.where(i >= 2, 0.0, MASK_NEG)
    j1 = jnp.maximum(i - 1, 0)
    j2 = jnp.maximum(i - 2, 0)
    dot = functools.partial(jnp.dot, preferred_element_type=jnp.float32)

    def q_window(h):
        qh = qt_ref[0, h * HEAD_DIM:(h + 1) * HEAD_DIM, :]
        return jnp.concatenate([qh, zeros64] if h // REP == 0 else [zeros64, qh], axis=0)

    def q_selected(h):
        return jnp.concatenate([q_window(h), sel_ref[h // REP]], axis=0)

    def stage_logits(slot, k_ref, j, q_of, bias_of=None):
        k = k_ref[0, pl.ds(pl.multiple_of(j * TK, TK), TK), :]
        for h in range(N_HEADS):
            s = dot(k, q_of(h))
            if bias_of is not None:
                s = s + bias_of(h)
            s_ref[slot, h] = s
            tmax_ref[slot, h:h + 1, :] = jnp.max(s, axis=0, keepdims=True)

    def accumulate(slot, v_ref, j, state):
        ms_ref, as_ref = state
        for h in range(N_HEADS):
            m = ms_ref[h:h + 1, :]
            m_new = jnp.maximum(m, tmax_ref[slot, h:h + 1, :])
            alpha = jnp.exp2(m - m_new)
            p = jnp.exp2(s_ref[slot, h] - m_new).astype(jnp.bfloat16)
            as_ref[h] = alpha * as_ref[h] + dot(v_ref[0, j, h // REP], p)
            ms_ref[h:h + 1, :] = m_new

    for ref in (m_ref, mw_ref):
        ref[...] = jnp.full(ref.shape, MASK_NEG, jnp.float32)
    for ref in (acc_ref, accw_ref):
        ref[...] = jnp.zeros(ref.shape, jnp.float32)
    sel_state = (m_ref, acc_ref)
    win_state = (mw_ref, accw_ref)
    near0 = lambda h: near_ref[h, 0]
    near1 = lambda h: near_ref[h, 1] + off1
    edge = lambda h: edge_ref[...] + off2

    off_c = pl.multiple_of(240 - 16 * i, 16)
    for h in range(N_HEADS):
        s = (dot(kc_ref[0, h // REP], qt_ref[0, h * HEAD_DIM:(h + 1) * HEAD_DIM, :])
             + tabc_ref[h, pl.ds(off_c, N_CMP_PAD), :])
        s_ref[0, h] = s
        tmax_ref[0, h:h + 1, :] = jnp.max(s, axis=0, keepdims=True)
    stage_logits(1, kw_ref, i, q_window, near0)

    t_row = i * TQ + lax.broadcasted_iota(jnp.int32, (1, TQ), 1)
    any_cmp = jnp.where(t_row >= L_CMP - 1, 1.0, 0.0)
    n_blk = tri_ref.shape[0]
    j_idx = lax.broadcasted_iota(jnp.int32, (n_blk, TQ), 0)
    cur = (i * TQ + lax.broadcasted_iota(jnp.int32, (n_blk, TQ), 1)) // L_SLC
    forced = (j_idx == 0) | (j_idx == cur) | (j_idx == cur - 1)
    allowed = j_idx <= cur
    bonus = jnp.where(forced, FORCE_BONUS, 0.0)
    for g in range(N_KV):
        imp = jnp.zeros((n_blk, TQ), jnp.float32)
        for h in range(g * REP, (g + 1) * REP):
            p = jnp.exp2(s_ref[0, h] - tmax_ref[0, h:h + 1, :]).astype(jnp.bfloat16)
            a = dot(vc_ref[0, g], p)
            inv = any_cmp / a[HEAD_DIM:HEAD_DIM + 1, :]
            oc_ref[h * HEAD_DIM:(h + 1) * HEAD_DIM, :] = a[0:HEAD_DIM, :] * inv
            imp = imp + a[V_ROWS:V_ROWS + n_blk, :] * inv
        score = jnp.where(allowed, imp + bonus, MASK_NEG)
        sel = _select_blocks(score, allowed, tri_ref[...])
        sel_ref[g, 0:n_blk, :] = jnp.where(sel, 0.0, MASK_NEG).astype(jnp.bfloat16)
        sel_ref[g, n_blk:2 * n_blk, :] = jnp.zeros((n_blk, TQ), jnp.bfloat16)

    accumulate(1, vwt_ref, i, win_state)
    stage_logits(0, ks_ref, i, q_selected, near0)
    stage_logits(1, kw_ref, j1, q_window, near1)
    accumulate(0, vst_ref, i, sel_state)
    stage_logits(0, ks_ref, j1, q_selected, near1)
    accumulate(1, vwt_ref, j1, win_state)
    stage_logits(1, kw_ref, j2, q_window, edge)
    accumulate(0, vst_ref, j1, sel_state)

    n_far = j1
    last = jnp.maximum(n_far - 1, 0)
    stage_logits(0, ks_ref, 0, q_selected)
    accumulate(1, vwt_ref, j2, win_state)

    def far_pair(c, carry):
        t = 2 * c
        stage_logits(1, ks_ref, t + 1, q_selected)
        accumulate(0, vst_ref, t, sel_state)
        stage_logits(0, ks_ref, jnp.minimum(t + 2, last), q_selected)
        accumulate(1, vst_ref, t + 1, sel_state)
        return carry

    lax.fori_loop(0, n_far // 2, far_pair, 0)

    @pl.when(n_far % 2 == 1)
    def _():
        accumulate(0, vst_ref, last, sel_state)

    for pair in range(N_HEADS // 2):
        outs = []
        for h in (2 * pair, 2 * pair + 1):
            rows = slice(h * HEAD_DIM, (h + 1) * HEAD_DIM)
            o_s = acc_ref[h, 0:HEAD_DIM, :] / acc_ref[h, HEAD_DIM:HEAD_DIM + 1, :]
            o_w = accw_ref[h, 0:HEAD_DIM, :] / accw_ref[h, HEAD_DIM:HEAD_DIM + 1, :]
            outs.append(gt_ref[0, 3 * h:3 * h + 1, :] * oc_ref[rows, :]
                        + gt_ref[0, 3 * h + 1:3 * h + 2, :] * o_s
                        + gt_ref[0, 3 * h + 2:3 * h + 3, :] * o_w)
        o_pair = jnp.concatenate(outs, axis=0)
        o_ref[0, :, pair * 128:(pair + 1) * 128] = o_pair.T.astype(jnp.bfloat16)


def _nsa_attn(qt, kc, vc, tab_c, ks, kw, vst, vwt, near, gt):
    B, _, S = qt.shape
    ni = S // TQ
    nt = S // TK
    n_blk = S // L_SLC
    key = np.arange(TK)[:, None]
    row = np.arange(TQ)[None, :]
    edge = jnp.asarray(np.where(2 * TK + row - key < WINDOW, 0.0, MASK_NEG).astype(np.float32))
    tri = jnp.asarray(np.tril(np.ones((n_blk, n_blk), np.float32), -1), dtype=jnp.bfloat16)
    per_batch = lambda shape: pl.BlockSpec((1,) + shape, lambda b, i: (b,) + (0,) * len(shape))
    const = lambda shape: pl.BlockSpec(shape, lambda b, i: (0,) * len(shape),
                                       pipeline_mode=pl.Buffered(1))
    return pl.pallas_call(
        _nsa_attn_kernel,
        grid=(B, ni),
        in_specs=[pl.BlockSpec((1, D_NSA, TQ), lambda b, i: (b, 0, i)),
                  per_batch((N_KV, N_CMP_PAD, HEAD_DIM)),
                  per_batch((N_KV, C_ROWS, N_CMP_PAD)),
                  const((N_HEADS, 512, TQ)),
                  const((n_blk, n_blk)),
                  per_batch((S, 256)),
                  per_batch((S, 128)),
                  per_batch((nt, N_KV, V_ROWS, TK)),
                  per_batch((nt, N_KV, V_ROWS, TK)),
                  const((N_HEADS, 2, TK, TQ)),
                  const((TK, TQ)),
                  pl.BlockSpec((1, 32, TQ), lambda b, i: (b, 0, i))],
        out_specs=pl.BlockSpec((1, TQ, D_NSA), lambda b, i: (b, i, 0)),
        out_shape=jax.ShapeDtypeStruct((B, S, D_NSA), jnp.bfloat16),
        scratch_shapes=[pltpu.VMEM((N_HEADS, TQ), jnp.float32),
                        pltpu.VMEM((N_HEADS, V_ROWS, TQ), jnp.float32),
                        pltpu.VMEM((N_HEADS, TQ), jnp.float32),
                        pltpu.VMEM((N_HEADS, V_ROWS, TQ), jnp.float32),
                        pltpu.VMEM((2, N_HEADS, TK, TQ), jnp.float32),
                        pltpu.VMEM((2, N_HEADS, TQ), jnp.float32),
                        pltpu.VMEM((N_KV, 128, TQ), jnp.bfloat16),
                        pltpu.VMEM((D_NSA, TQ), jnp.float32)],
        compiler_params=pltpu.CompilerParams(
            dimension_semantics=("arbitrary", "arbitrary"), vmem_limit_bytes=VMEM_LIMIT),
        name="nsa_attn",
    )(qt, kc, vc, tab_c, tri, ks, kw, vst, vwt, near, edge, gt)


def _out_ffn_kernel(on_ref, oc_ref, x_ref, wo_ref, gpost_ref, gffn_ref, wup_ref, cw_ref, wdn_ref,
                    g_ref, o_ref, carry_ref, act_ref):
    i = pl.program_id(1)
    tm = x_ref.shape[1]
    y = jnp.dot(on_ref[0], wo_ref[0:D_NSA, :], preferred_element_type=jnp.float32)
    y = y + jnp.dot(oc_ref[0], wo_ref[D_NSA:D_NSA + D_CONV, :], preferred_element_type=jnp.float32)
    x1 = x_ref[0] + _rms(y, gpost_ref[...])
    hb = _rms(x1, gffn_ref[...]).astype(jnp.bfloat16)

    @pl.when(i == 0)
    def _():
        carry_ref[...] = jnp.zeros_like(carry_ref)

    row = lax.broadcasted_iota(jnp.int32, (tm, FF_CHUNK), 0)
    is0 = row == 0
    is1 = row == 1

    def conv(z, c0):
        cols = slice(c0, c0 + FF_CHUNK)
        prev1 = carry_ref[7:8, cols]
        prev2 = carry_ref[6:7, cols]
        z1 = jnp.where(is0, prev1, pltpu.roll(z, 1, 0))
        z2 = jnp.where(is0, prev2, jnp.where(is1, prev1, pltpu.roll(z, 2, 0)))
        carry_ref[:, cols] = z[tm - 8:tm, :]
        return cw_ref[0:1, cols] * z2 + cw_ref[1:2, cols] * z1 + cw_ref[2:3, cols] * z

    for c in range(D_FF // FF_CHUNK):
        g0 = c * FF_CHUNK
        u0 = D_FF + c * FF_CHUNK
        zg = jnp.dot(hb, wup_ref[:, g0:g0 + FF_CHUNK], preferred_element_type=jnp.float32)
        zu = jnp.dot(hb, wup_ref[:, u0:u0 + FF_CHUNK], preferred_element_type=jnp.float32)
        act = _gelu(conv(zg, g0)) * conv(zu, u0)
        act_ref[:, g0:g0 + FF_CHUNK] = act.astype(jnp.bfloat16)
    y2 = jnp.dot(act_ref[...], wdn_ref[...], preferred_element_type=jnp.float32)
    o_ref[0] = x1 + _rms(y2, g_ref[...])


def _out_ffn(o_nsa, o_conv, x, w_out, g_post, g_ffn, w_up, conv_w, w_down, g_ffn_post):
    B, S, D = x.shape
    tm = TM_FFN
    const = lambda b, i: (0, 0)
    single = pl.Buffered(1)
    row_tile = lambda width: pl.BlockSpec((1, tm, width), lambda b, i: (b, i, 0))
    weight = lambda shape: pl.BlockSpec(shape, const, pipeline_mode=single)
    return pl.pallas_call(
        _out_ffn_kernel,
        grid=(B, S // tm),
        in_specs=[row_tile(D_NSA), row_tile(D_CONV), row_tile(D),
                  weight((D, D)), weight((1, D)), weight((1, D)),
                  weight((D, 2 * D_FF)), weight((3, 2 * D_FF)), weight((D_FF, D)), weight((1, D))],
        out_specs=row_tile(D),
        out_shape=jax.ShapeDtypeStruct((B, S, D), jnp.float32),
        scratch_shapes=[pltpu.VMEM((8, 2 * D_FF), jnp.float32),
                        pltpu.VMEM((tm, D_FF), jnp.bfloat16)],
        compiler_params=pltpu.CompilerParams(
            dimension_semantics=("arbitrary", "arbitrary"), vmem_limit_bytes=VMEM_LIMIT),
        name="out_ffn",
    )(o_nsa, o_conv, x, w_out.astype(jnp.bfloat16), g_post.reshape(1, D), g_ffn.reshape(1, D),
      w_up.astype(jnp.bfloat16), conv_w, w_down.astype(jnp.bfloat16), g_ffn_post.reshape(1, D))


def kernel(x, norm_mix_pre, norm_mix_post, norm_ffn_pre, norm_ffn_post, w_in, pe_cmp_k, pe_cmp_v,
           w_cmp_k1, w_cmp_k2, w_cmp_v1, w_cmp_v2, rel_bias, conv_mix_w, w_out, w_ffn_up,
           ffn_conv_w, w_ffn_down):
    B, S, D = x.shape
    assert (S, D) == (4096, D_MODEL) and norm_mix_pre.shape[0] == 1
    tab_c, near = _bias_tables(rel_bias)
    for l in range(norm_mix_pre.shape[0]):
        qt, kc_in, vc_in, ks, kw, vst, vwt, gt, o_conv = _in_proj(
            x, norm_mix_pre[l], w_in[l], conv_mix_w[l])
        kc, vc = _compress(kc_in, vc_in, pe_cmp_k[l], pe_cmp_v[l], w_cmp_k1[l], w_cmp_k2[l],
                           w_cmp_v1[l], w_cmp_v2[l])
        o_nsa = _nsa_attn(qt, kc, vc, tab_c, ks, kw, vst, vwt, near, gt)
        x = _out_ffn(o_nsa, o_conv, x, w_out[l], norm_mix_post[l], norm_ffn_pre[l],
                     w_ffn_up[l], ffn_conv_w[l], w_ffn_down[l], norm_ffn_post[l])
    return x
```

```python
import functools
import math

import numpy as np
import jax
import jax.numpy as jnp
from jax import lax
from jax.experimental import pallas as pl
from jax.experimental.pallas import tpu as pltpu

D_MODEL = 1024
D_NSA = 512
D_CONV = 512
HEAD_DIM = 64
N_HEADS = 8
N_KV = 2
REP = 4
KV_W = 128
N_GATES = 24
L_CMP = 32
S_CMP = 16
L_SLC = 64
N_SEL = 16
WINDOW = 512
CMP_HIDDEN = 256
D_FF = 2816
N_BUCKETS = 32
MAX_DIST = 128
RMS_EPS = 1e-6
MASK_NEG = -1e30
N_FREE = N_SEL - 3
LOG2E = math.log2(math.e)

TQ = 256
TK = 256
V_ROWS = 80
C_ROWS = V_ROWS + 64
N_CMP_PAD = 256
TM_IN = 1024
TM_FFN = 1024
FF_CHUNK = 256
VMEM_LIMIT = 56 * 1024 * 1024

_NT = (((1,), (1,)), ((), ()))


def _bucket_np(d):
    max_exact = N_BUCKETS // 2
    d = np.maximum(d, 0)
    df = np.maximum(d, 1).astype(np.float32)
    large = max_exact + (np.log(df / max_exact) / math.log(MAX_DIST / max_exact)
                         * (N_BUCKETS - max_exact)).astype(np.int32)
    return np.where(d < max_exact, d, np.minimum(large, N_BUCKETS - 1)).astype(np.int32)


def _bucket_or_mask(d):
    return np.where(d >= 0, _bucket_np(d), -1).astype(np.int32)


def _rms(x, g):
    return x * lax.rsqrt(jnp.mean(x * x, axis=-1, keepdims=True) + RMS_EPS) * g


def _gelu(x):
    return jax.nn.gelu(x, approximate=True)


def _band_box(idx):
    rows, cols = np.nonzero((idx >= 0) & (idx < N_BUCKETS - 1))
    r0, r1, c0, c1 = int(rows.min()), int(rows.max()) + 1, int(cols.min()), int(cols.max()) + 1
    return r0 // 8 * 8, -(-r1 // 8) * 8, c0 // 128 * 128, -(-c1 // 128) * 128


def _bias_tables_kernel(boxes, rb_ref, idx_c_ref, idx_n_ref, tab_c_ref, tab_n_ref):
    h = pl.program_id(0)
    last = rb_ref[h, N_BUCKETS - 1]

    def fill(idx, out, box):
        out[...] = jnp.where(idx < 0, MASK_NEG, 0.0)
        r0, r1, c0, c1 = box
        sub = idx[r0:r1, c0:c1]
        val = jnp.zeros(sub.shape, jnp.float32)
        for b in range(N_BUCKETS - 1):
            val = jnp.where(sub == b, rb_ref[h, b] - last, val)
        out[r0:r1, c0:c1] = jnp.where(sub < 0, MASK_NEG, val * LOG2E)

    fill(idx_c_ref[...], tab_c_ref.at[0], boxes[0])
    for k in range(idx_n_ref.shape[0]):
        fill(idx_n_ref[k], tab_n_ref.at[0, k], boxes[1 + k])


def _bias_tables(rel_bias):
    u = np.arange(512)[:, None] - 240
    row = np.arange(TQ)[None, :]
    d_c = row - S_CMP * u - (L_CMP - 1)
    idx_c = _bucket_or_mask(d_c)
    key = np.arange(TK)[:, None]
    idx_n = np.stack([_bucket_or_mask(dd + row - key) for dd in (0, TK)])
    boxes = tuple(_band_box(t) for t in (idx_c, *idx_n))
    return pl.pallas_call(
        functools.partial(_bias_tables_kernel, boxes),
        grid=(N_HEADS,),
        in_specs=[pl.BlockSpec(memory_space=pltpu.SMEM),
                  pl.BlockSpec((512, TQ), lambda h: (0, 0)),
                  pl.BlockSpec((2, TK, TQ), lambda h: (0, 0, 0))],
        out_specs=[pl.BlockSpec((1, 512, TQ), lambda h: (h, 0, 0)),
                   pl.BlockSpec((1, 2, TK, TQ), lambda h: (h, 0, 0, 0))],
        out_shape=[jax.ShapeDtypeStruct((N_HEADS, 512, TQ), jnp.float32),
                   jax.ShapeDtypeStruct((N_HEADS, 2, TK, TQ), jnp.float32)],
        name="bias_tables",
    )(rel_bias, jnp.asarray(idx_c), jnp.asarray(idx_n))


def _in_proj_kernel(x_ref, g_ref, wq_ref, wnat_ref, wvt_ref, wgt_ref, cw_ref,
                    qt_ref, kc_ref, vc_ref, ks_ref, kw_ref, vst_ref, vwt_ref, gt_ref,
                    oconv_ref, carry_ref, stage_ref):
    i = pl.program_id(1)
    tm = x_ref.shape[1]
    hb = _rms(x_ref[0], g_ref[...]).astype(jnp.bfloat16)

    qt_ref[0] = lax.dot_general(wq_ref[...], hb, _NT,
                                preferred_element_type=jnp.float32).astype(jnp.bfloat16)
    vt = lax.dot_general(wvt_ref[...], hb, _NT, preferred_element_type=jnp.float32)
    ones = jnp.ones((V_ROWS - HEAD_DIM, TK), jnp.bfloat16)
    for t in range(tm // TK):
        for g in range(N_KV):
            for out_ref, r0 in ((vst_ref, g * HEAD_DIM), (vwt_ref, KV_W + g * HEAD_DIM)):
                out_ref[0, t, g, 0:HEAD_DIM, :] = vt[r0:r0 + HEAD_DIM,
                                                      t * TK:(t + 1) * TK].astype(jnp.bfloat16)
                out_ref[0, t, g, HEAD_DIM:V_ROWS, :] = ones
    gl = lax.dot_general(wgt_ref[...], hb, _NT, preferred_element_type=jnp.float32)
    gt_ref[0] = jax.nn.sigmoid(gl)

    kv_c = jnp.dot(hb, wnat_ref[:, 0:2 * KV_W], preferred_element_type=jnp.float32)
    for a, out_ref in enumerate((kc_ref, vc_ref)):
        stage_ref[a] = kv_c[:, a * KV_W:(a + 1) * KV_W]
        for l in range(S_CMP):
            rows = stage_ref[a, pl.ds(l, tm // S_CMP, stride=S_CMP), :]
            for g in range(N_KV):
                out_ref[0, g, :, l * HEAD_DIM:(l + 1) * HEAD_DIM] = rows[
                    :, g * HEAD_DIM:(g + 1) * HEAD_DIM]
    k_sw = jnp.dot(hb, wnat_ref[:, 256:512], preferred_element_type=jnp.float32)
    blk = (i * tm + lax.broadcasted_iota(jnp.int32, (tm, KV_W), 0)) // L_SLC
    col = lax.broadcasted_iota(jnp.int32, (tm, KV_W), 1)
    onehot = jnp.where(blk == col, 1.0, 0.0)
    ks_ref[0, :, 0:KV_W] = k_sw[:, 0:KV_W].astype(jnp.bfloat16)
    ks_ref[0, :, KV_W:2 * KV_W] = onehot.astype(jnp.bfloat16)
    kw_ref[0] = k_sw[:, KV_W:2 * KV_W].astype(jnp.bfloat16)

    bg = jnp.dot(hb, wnat_ref[:, 512:1024], preferred_element_type=jnp.float32)
    cg = jnp.dot(hb, wnat_ref[:, 1024:1536], preferred_element_type=jnp.float32)
    xt = jnp.dot(hb, wnat_ref[:, 1536:2048], preferred_element_type=jnp.float32)
    z = cg * xt

    @pl.when(i == 0)
    def _():
        carry_ref[...] = jnp.zeros_like(carry_ref)

    prev1 = carry_ref[7:8, :]
    prev2 = carry_ref[6:7, :]
    row = lax.broadcasted_iota(jnp.int32, z.shape, 0)
    z1 = jnp.where(row == 0, prev1, pltpu.roll(z, 1, 0))
    z2 = jnp.where(row == 0, prev2, jnp.where(row == 1, prev1, pltpu.roll(z, 2, 0)))
    y = cw_ref[0:1, :] * z2 + cw_ref[1:2, :] * z1 + cw_ref[2:3, :] * z
    oconv_ref[0] = (bg * y).astype(jnp.bfloat16)
    carry_ref[...] = z[tm - 8:tm, :]


def _in_proj(x, g_pre, w_in, conv_w):
    B, S, D = x.shape
    tm = TM_IN
    wq = (w_in[:, 0:512].T * (HEAD_DIM ** -0.5 * LOG2E)).astype(jnp.bfloat16)
    wnat = jnp.concatenate([w_in[:, 512:896], w_in[:, 1024:1152],
                            w_in[:, 1304:2840]], axis=1).astype(jnp.bfloat16)
    wvt = jnp.concatenate([w_in[:, 896:1024], w_in[:, 1152:1280]], axis=1).T.astype(jnp.bfloat16)
    wgt = jnp.pad(w_in[:, 1280:1304], ((0, 0), (0, 8))).T.astype(jnp.bfloat16)
    nt = S // TK
    const = lambda b, i: (0, 0)
    outs = pl.pallas_call(
        _in_proj_kernel,
        grid=(B, S // tm),
        in_specs=[pl.BlockSpec((1, tm, D), lambda b, i: (b, i, 0)),
                  pl.BlockSpec((1, D), const),
                  pl.BlockSpec((512, D), const),
                  pl.BlockSpec((D, 2048), const),
                  pl.BlockSpec((256, D), const),
                  pl.BlockSpec((32, D), const),
                  pl.BlockSpec((3, D_CONV), const)],
        out_specs=[pl.BlockSpec((1, 512, tm), lambda b, i: (b, 0, i)),
                   pl.BlockSpec((1, N_KV, tm // S_CMP, S_CMP * HEAD_DIM), lambda b, i: (b, 0, i, 0)),
                   pl.BlockSpec((1, N_KV, tm // S_CMP, S_CMP * HEAD_DIM), lambda b, i: (b, 0, i, 0)),
                   pl.BlockSpec((1, tm, 256), lambda b, i: (b, i, 0)),
                   pl.BlockSpec((1, tm, 128), lambda b, i: (b, i, 0)),
                   pl.BlockSpec((1, tm // TK, N_KV, V_ROWS, TK), lambda b, i: (b, i, 0, 0, 0)),
                   pl.BlockSpec((1, tm // TK, N_KV, V_ROWS, TK), lambda b, i: (b, i, 0, 0, 0)),
                   pl.BlockSpec((1, 32, tm), lambda b, i: (b, 0, i)),
                   pl.BlockSpec((1, tm, D_CONV), lambda b, i: (b, i, 0))],
        out_shape=[jax.ShapeDtypeStruct((B, 512, S), jnp.bfloat16),
                   jax.ShapeDtypeStruct((B, N_KV, S // S_CMP, S_CMP * HEAD_DIM), jnp.float32),
                   jax.ShapeDtypeStruct((B, N_KV, S // S_CMP, S_CMP * HEAD_DIM), jnp.float32),
                   jax.ShapeDtypeStruct((B, S, 256), jnp.bfloat16),
                   jax.ShapeDtypeStruct((B, S, 128), jnp.bfloat16),
                   jax.ShapeDtypeStruct((B, nt, N_KV, V_ROWS, TK), jnp.bfloat16),
                   jax.ShapeDtypeStruct((B, nt, N_KV, V_ROWS, TK), jnp.bfloat16),
                   jax.ShapeDtypeStruct((B, 32, S), jnp.float32),
                   jax.ShapeDtypeStruct((B, S, D_CONV), jnp.bfloat16)],
        scratch_shapes=[pltpu.VMEM((8, D_CONV), jnp.float32),
                        pltpu.VMEM((2, tm, KV_W), jnp.float32)],
        compiler_params=pltpu.CompilerParams(
            dimension_semantics=("arbitrary", "arbitrary"), vmem_limit_bytes=VMEM_LIMIT),
        name="in_proj",
    )(x, g_pre.reshape(1, D), wq, wnat, wvt, wgt, conv_w)
    return outs


def _compress_kernel(ck_ref, cv_ref, pek_ref, pev_ref, wk1_ref, wk2_ref, wv1_ref, wv2t_ref,
                     ovl_ref, kc_ref, vct_ref):
    def hidden(c_ref, pe_ref, w1_ref):
        c = c_ref[0, 0]
        a = jnp.dot((c + pe_ref[0:1, :]).astype(jnp.bfloat16), w1_ref[0].astype(jnp.bfloat16),
                    preferred_element_type=jnp.float32)
        b = jnp.dot((c + pe_ref[1:2, :]).astype(jnp.bfloat16), w1_ref[1].astype(jnp.bfloat16),
                    preferred_element_type=jnp.float32)
        return _gelu(a + pltpu.roll(b, N_CMP_PAD - 1, 0)).astype(jnp.bfloat16)

    kc_ref[0, 0] = jnp.dot(hidden(ck_ref, pek_ref, wk1_ref), wk2_ref[...].astype(jnp.bfloat16),
                           preferred_element_type=jnp.float32).astype(jnp.bfloat16)
    vct_ref[0, 0, 0:HEAD_DIM, :] = lax.dot_general(
        wv2t_ref[...].astype(jnp.bfloat16), hidden(cv_ref, pev_ref, wv1_ref), _NT,
        preferred_element_type=jnp.float32).astype(jnp.bfloat16)
    vct_ref[0, 0, HEAD_DIM:V_ROWS, :] = jnp.ones((V_ROWS - HEAD_DIM, N_CMP_PAD), jnp.bfloat16)
    vct_ref[0, 0, V_ROWS:C_ROWS, :] = ovl_ref[...]


def _compress(kc_in, vc_in, pe_k, pe_v, wk1, wk2, wv1, wv2):
    B, _, n_str, half = kc_in.shape
    const2 = lambda b, g: (0, 0)
    const3 = lambda b, g: (0, 0, 0)
    return pl.pallas_call(
        _compress_kernel,
        grid=(B, N_KV),
        in_specs=[pl.BlockSpec((1, 1, n_str, half), lambda b, g: (b, g, 0, 0)),
                  pl.BlockSpec((1, 1, n_str, half), lambda b, g: (b, g, 0, 0)),
                  pl.BlockSpec((2, half), const2),
                  pl.BlockSpec((2, half), const2),
                  pl.BlockSpec((2, half, CMP_HIDDEN), const3),
                  pl.BlockSpec((CMP_HIDDEN, HEAD_DIM), const2),
                  pl.BlockSpec((2, half, CMP_HIDDEN), const3),
                  pl.BlockSpec((HEAD_DIM, CMP_HIDDEN), const2),
                  pl.BlockSpec((C_ROWS - V_ROWS, n_str), const2)],
        out_specs=[pl.BlockSpec((1, 1, n_str, HEAD_DIM), lambda b, g: (b, g, 0, 0)),
                   pl.BlockSpec((1, 1, C_ROWS, n_str), lambda b, g: (b, g, 0, 0))],
        out_shape=[jax.ShapeDtypeStruct((B, N_KV, n_str, HEAD_DIM), jnp.bfloat16),
                   jax.ShapeDtypeStruct((B, N_KV, C_ROWS, n_str), jnp.bfloat16)],
        compiler_params=pltpu.CompilerParams(
            dimension_semantics=("arbitrary", "arbitrary"), vmem_limit_bytes=VMEM_LIMIT),
        name="compress",
    )(kc_in, vc_in, pe_k.reshape(2, half), pe_v.reshape(2, half),
      wk1.reshape(2, half, CMP_HIDDEN), wk2, wv1.reshape(2, half, CMP_HIDDEN), wv2.T,
      _overlap_t(n_str * S_CMP))


def _overlap_t(S):
    n_slc = S // L_SLC
    c_start = np.arange(N_CMP_PAD) * S_CMP
    s_start = np.arange(n_slc) * L_SLC
    ov = np.clip(np.minimum(c_start[:, None] + L_CMP, s_start[None, :] + L_SLC)
                 - np.maximum(c_start[:, None], s_start[None, :]), 0, None).astype(np.float32) / L_CMP
    ov[(S - L_CMP) // S_CMP + 1:, :] = 0.0
    return jnp.asarray(ov.T, dtype=jnp.bfloat16)


def _select_top(score, k, tri):
    w = score
    cnt = jnp.zeros((1, TQ), jnp.float32)
    thr = jnp.zeros((1, TQ), jnp.float32)
    n_gt = jnp.zeros((1, TQ), jnp.float32)
    for _ in range(N_FREE):
        mx = jnp.max(w, axis=0, keepdims=True)
        eq = w == mx
        c = jnp.sum(jnp.where(eq, 1.0, 0.0), axis=0, keepdims=True)
        cross = (cnt < k) & (cnt + c >= k)
        thr = jnp.where(cross, mx, thr)
        n_gt = jnp.where(cross, cnt, n_gt)
        cnt = cnt + c
        w = jnp.where(eq, -jnp.inf, w)
    at_thr = score == thr
    earlier = jnp.dot(tri, jnp.where(at_thr, 1.0, 0.0).astype(jnp.bfloat16),
                      preferred_element_type=jnp.float32)
    return (score > thr) | (at_thr & (earlier + n_gt < k))


def _nsa_attn_kernel(qt_ref, kc_ref, vc_ref, tabc_ref, tri_ref, ks_ref, kw_ref, vst_ref, vwt_ref,
                     near_ref, edge_ref, gt_ref, o_ref,
                     m_ref, acc_ref, mw_ref, accw_ref, s_ref, tmax_ref, sel_ref, oc_ref):
    i = pl.program_id(1)
    zeros64 = jnp.zeros((HEAD_DIM, TQ), jnp.bfloat16)
    off1 = jnp.where(i >= 1, 0.0, MASK_NEG)
    off2 = jnp.where(i >= 2, 0.0, MASK_NEG)
    j1 = jnp.maximum(i - 1, 0)
    j2 = jnp.maximum(i - 2, 0)
    dot = functools.partial(jnp.dot, preferred_element_type=jnp.float32)

    def q_window(h):
        qh = qt_ref[0, h * HEAD_DIM:(h + 1) * HEAD_DIM, :]
        return jnp.concatenate([qh, zeros64] if h // REP == 0 else [zeros64, qh], axis=0)

    def q_selected(h):
        return jnp.concatenate([q_window(h), sel_ref[h // REP]], axis=0)

    def stage_logits(slot, k_ref, j, q_of, bias_of=None):
        k = k_ref[0, pl.ds(pl.multiple_of(j * TK, TK), TK), :]
        for h in range(N_HEADS):
            s = dot(k, q_of(h))
            if bias_of is not None:
                s = s + bias_of(h)
            s_ref[slot, h] = s
            tmax_ref[slot, h:h + 1, :] = jnp.max(s, axis=0, keepdims=True)

    def accumulate(slot, v_ref, j, state, first=False, shift=None):
        ms_ref, as_ref = state
        for h in range(N_HEADS):
            t_max = tmax_ref[slot, h:h + 1, :]
            if shift is not None:
                t_max = t_max + shift
            m_new = t_max if first else jnp.maximum(ms_ref[h:h + 1, :], t_max)
            m_sub = m_new if shift is None else m_new - shift
            p = jnp.exp2(s_ref[slot, h] - m_sub).astype(jnp.bfloat16)
            pv = dot(v_ref[0, j, h // REP], p)
            if first:
                as_ref[h] = pv
            else:
                as_ref[h] = jnp.exp2(ms_ref[h:h + 1, :] - m_new) * as_ref[h] + pv
            ms_ref[h:h + 1, :] = m_new

    sel_state = (m_ref, acc_ref)
    win_state = (mw_ref, accw_ref)
    near0 = lambda h: near_ref[h, 0]
    near1 = lambda h: near_ref[h, 1]
    edge = lambda h: edge_ref[...]

    off_c = pl.multiple_of(240 - 16 * i, 16)
    for h in range(N_HEADS):
        s = (dot(kc_ref[0, h // REP], qt_ref[0, h * HEAD_DIM:(h + 1) * HEAD_DIM, :])
             + tabc_ref[h, pl.ds(off_c, N_CMP_PAD), :])
        s_ref[0, h] = s
        tmax_ref[0, h:h + 1, :] = jnp.max(s, axis=0, keepdims=True)
    stage_logits(1, kw_ref, i, q_window, near0)

    t_row = i * TQ + lax.broadcasted_iota(jnp.int32, (1, TQ), 1)
    any_cmp = jnp.where(t_row >= L_CMP - 1, 1.0, 0.0)
    n_blk = tri_ref.shape[0]
    j_idx = lax.broadcasted_iota(jnp.int32, (n_blk, TQ), 0)
    cur = (i * TQ + lax.broadcasted_iota(jnp.int32, (n_blk, TQ), 1)) // L_SLC
    forced = (j_idx == 0) | (j_idx == cur) | (j_idx == cur - 1)
    candidate = (j_idx < cur - 1) & (j_idx > 0)
    n_free = float(N_SEL - 1) - jnp.minimum(t_row // L_SLC, 2).astype(jnp.float32)
    for g in range(N_KV):
        imp = jnp.zeros((n_blk, TQ), jnp.float32)
        for h in range(g * REP, (g + 1) * REP):
            p = jnp.exp2(s_ref[0, h] - tmax_ref[0, h:h + 1, :]).astype(jnp.bfloat16)
            a = dot(vc_ref[0, g], p)
            inv = any_cmp / a[HEAD_DIM:HEAD_DIM + 1, :]
            oc_ref[h * HEAD_DIM:(h + 1) * HEAD_DIM, :] = a[0:HEAD_DIM, :] * inv
            imp = imp + a[V_ROWS:V_ROWS + n_blk, :] * inv
        free = _select_top(jnp.where(candidate, imp, MASK_NEG), n_free, tri_ref[...])
        sel = forced | (candidate & free)
        sel_ref[g, 0:n_blk, :] = jnp.where(sel, 0.0, MASK_NEG).astype(jnp.bfloat16)
        sel_ref[g, n_blk:2 * n_blk, :] = jnp.zeros((n_blk, TQ), jnp.bfloat16)

    accumulate(1, vwt_ref, i, win_state, first=True)
    stage_logits(0, ks_ref, i, q_selected, near0)
    stage_logits(1, kw_ref, j1, q_window, near1)
    accumulate(0, vst_ref, i, sel_state, first=True)
    stage_logits(0, ks_ref, j1, q_selected, near1)
    accumulate(1, vwt_ref, j1, win_state, shift=off1)
    stage_logits(1, kw_ref, j2, q_window, edge)
    accumulate(0, vst_ref, j1, sel_state, shift=off1)

    n_far = j1
    last = jnp.maximum(n_far - 1, 0)
    stage_logits(0, ks_ref, 0, q_selected)
    accumulate(1, vwt_ref, j2, win_state, shift=off2)

    def far_pair(c, carry):
        t = 2 * c
        stage_logits(1, ks_ref, t + 1, q_selected)
        accumulate(0, vst_ref, t, sel_state)
        stage_logits(0, ks_ref, jnp.minimum(t + 2, last), q_selected)
        accumulate(1, vst_ref, t + 1, sel_state)
        return carry

    lax.fori_loop(0, n_far // 2, far_pair, 0)

    @pl.when(n_far % 2 == 1)
    def _():
        accumulate(0, vst_ref, last, sel_state)

    for pair in range(N_HEADS // 2):
        outs = []
        for h in (2 * pair, 2 * pair + 1):
            rows = slice(h * HEAD_DIM, (h + 1) * HEAD_DIM)
            o_s = acc_ref[h, 0:HEAD_DIM, :] / acc_ref[h, HEAD_DIM:HEAD_DIM + 1, :]
            o_w = accw_ref[h, 0:HEAD_DIM, :] / accw_ref[h, HEAD_DIM:HEAD_DIM + 1, :]
            outs.append(gt_ref[0, 3 * h:3 * h + 1, :] * oc_ref[rows, :]
                        + gt_ref[0, 3 * h + 1:3 * h + 2, :] * o_s
                        + gt_ref[0, 3 * h + 2:3 * h + 3, :] * o_w)
        o_pair = jnp.concatenate(outs, axis=0)
        o_ref[0, :, pair * 128:(pair + 1) * 128] = o_pair.T.astype(jnp.bfloat16)


def _nsa_attn(qt, kc, vc, tab_c, ks, kw, vst, vwt, near, gt):
    B, _, S = qt.shape
    ni = S // TQ
    nt = S // TK
    n_blk = S // L_SLC
    key = np.arange(TK)[:, None]
    row = np.arange(TQ)[None, :]
    edge = jnp.asarray(np.where(2 * TK + row - key < WINDOW, 0.0, MASK_NEG).astype(np.float32))
    tri = jnp.asarray(np.tril(np.ones((n_blk, n_blk), np.float32), -1), dtype=jnp.bfloat16)
    per_batch = lambda shape: pl.BlockSpec((1,) + shape, lambda b, i: (b,) + (0,) * len(shape))
    const = lambda shape: pl.BlockSpec(shape, lambda b, i: (0,) * len(shape),
                                       pipeline_mode=pl.Buffered(1))
    return pl.pallas_call(
        _nsa_attn_kernel,
        grid=(B, ni),
        in_specs=[pl.BlockSpec((1, D_NSA, TQ), lambda b, i: (b, 0, i)),
                  per_batch((N_KV, N_CMP_PAD, HEAD_DIM)),
                  per_batch((N_KV, C_ROWS, N_CMP_PAD)),
                  const((N_HEADS, 512, TQ)),
                  const((n_blk, n_blk)),
                  per_batch((S, 256)),
                  per_batch((S, 128)),
                  per_batch((nt, N_KV, V_ROWS, TK)),
                  per_batch((nt, N_KV, V_ROWS, TK)),
                  const((N_HEADS, 2, TK, TQ)),
                  const((TK, TQ)),
                  pl.BlockSpec((1, 32, TQ), lambda b, i: (b, 0, i))],
        out_specs=pl.BlockSpec((1, TQ, D_NSA), lambda b, i: (b, i, 0)),
        out_shape=jax.ShapeDtypeStruct((B, S, D_NSA), jnp.bfloat16),
        scratch_shapes=[pltpu.VMEM((N_HEADS, TQ), jnp.float32),
                        pltpu.VMEM((N_HEADS, V_ROWS, TQ), jnp.float32),
                        pltpu.VMEM((N_HEADS, TQ), jnp.float32),
                        pltpu.VMEM((N_HEADS, V_ROWS, TQ), jnp.float32),
                        pltpu.VMEM((2, N_HEADS, TK, TQ), jnp.float32),
                        pltpu.VMEM((2, N_HEADS, TQ), jnp.float32),
                        pltpu.VMEM((N_KV, 128, TQ), jnp.bfloat16),
                        pltpu.VMEM((D_NSA, TQ), jnp.float32)],
        compiler_params=pltpu.CompilerParams(
            dimension_semantics=("arbitrary", "arbitrary"), vmem_limit_bytes=VMEM_LIMIT),
        name="nsa_attn",
    )(qt, kc, vc, tab_c, tri, ks, kw, vst, vwt, near, edge, gt)


def _out_ffn_kernel(on_ref, oc_ref, x_ref, wo_ref, gpost_ref, gffn_ref, wup_ref, cw_ref, wdn_ref,
                    g_ref, o_ref, carry_ref, act_ref):
    i = pl.program_id(1)
    tm = x_ref.shape[1]
    y = jnp.dot(on_ref[0], wo_ref[0:D_NSA, :], preferred_element_type=jnp.float32)
    y = y + jnp.dot(oc_ref[0], wo_ref[D_NSA:D_NSA + D_CONV, :], preferred_element_type=jnp.float32)
    x1 = x_ref[0] + _rms(y, gpost_ref[...])
    hb = _rms(x1, gffn_ref[...]).astype(jnp.bfloat16)

    @pl.when(i == 0)
    def _():
        carry_ref[...] = jnp.zeros_like(carry_ref)

    row = lax.broadcasted_iota(jnp.int32, (tm, FF_CHUNK), 0)
    is0 = row == 0
    is1 = row == 1

    def conv(z, c0):
        cols = slice(c0, c0 + FF_CHUNK)
        prev1 = carry_ref[7:8, cols]
        prev2 = carry_ref[6:7, cols]
        z1 = jnp.where(is0, prev1, pltpu.roll(z, 1, 0))
        z2 = jnp.where(is0, prev2, jnp.where(is1, prev1, pltpu.roll(z, 2, 0)))
        carry_ref[:, cols] = z[tm - 8:tm, :]
        return cw_ref[0:1, cols] * z2 + cw_ref[1:2, cols] * z1 + cw_ref[2:3, cols] * z

    for c in range(D_FF // FF_CHUNK):
        g0 = c * FF_CHUNK
        u0 = D_FF + c * FF_CHUNK
        zg = jnp.dot(hb, wup_ref[:, g0:g0 + FF_CHUNK], preferred_element_type=jnp.float32)
        zu = jnp.dot(hb, wup_ref[:, u0:u0 + FF_CHUNK], preferred_element_type=jnp.float32)
        act = _gelu(conv(zg, g0)) * conv(zu, u0)
        act_ref[:, g0:g0 + FF_CHUNK] = act.astype(jnp.bfloat16)
    y2 = jnp.dot(act_ref[...], wdn_ref[...], preferred_element_type=jnp.float32)
    o_ref[0] = x1 + _rms(y2, g_ref[...])


def _out_ffn(o_nsa, o_conv, x, w_out, g_post, g_ffn, w_up, conv_w, w_down, g_ffn_post):
    B, S, D = x.shape
    tm = TM_FFN
    const = lambda b, i: (0, 0)
    single = pl.Buffered(1)
    row_tile = lambda width: pl.BlockSpec((1, tm, width), lambda b, i: (b, i, 0))
    weight = lambda shape: pl.BlockSpec(shape, const, pipeline_mode=single)
    return pl.pallas_call(
        _out_ffn_kernel,
        grid=(B, S // tm),
        in_specs=[row_tile(D_NSA), row_tile(D_CONV), row_tile(D),
                  weight((D, D)), weight((1, D)), weight((1, D)),
                  weight((D, 2 * D_FF)), weight((3, 2 * D_FF)), weight((D_FF, D)), weight((1, D))],
        out_specs=row_tile(D),
        out_shape=jax.ShapeDtypeStruct((B, S, D), jnp.float32),
        scratch_shapes=[pltpu.VMEM((8, 2 * D_FF), jnp.float32),
                        pltpu.VMEM((tm, D_FF), jnp.bfloat16)],
        compiler_params=pltpu.CompilerParams(
            dimension_semantics=("arbitrary", "arbitrary"), vmem_limit_bytes=VMEM_LIMIT),
        name="out_ffn",
    )(o_nsa, o_conv, x, w_out.astype(jnp.bfloat16), g_post.reshape(1, D), g_ffn.reshape(1, D),
      w_up.astype(jnp.bfloat16), conv_w, w_down.astype(jnp.bfloat16), g_ffn_post.reshape(1, D))


def kernel(x, norm_mix_pre, norm_mix_post, norm_ffn_pre, norm_ffn_post, w_in, pe_cmp_k, pe_cmp_v,
           w_cmp_k1, w_cmp_k2, w_cmp_v1, w_cmp_v2, rel_bias, conv_mix_w, w_out, w_ffn_up,
           ffn_conv_w, w_ffn_down):
    B, S, D = x.shape
    assert (S, D) == (4096, D_MODEL) and norm_mix_pre.shape[0] == 1
    tab_c, near = _bias_tables(rel_bias)
    for l in range(norm_mix_pre.shape[0]):
        qt, kc_in, vc_in, ks, kw, vst, vwt, gt, o_conv = _in_proj(
            x, norm_mix_pre[l], w_in[l], conv_mix_w[l])
        kc, vc = _compress(kc_in, vc_in, pe_cmp_k[l], pe_cmp_v[l], w_cmp_k1[l], w_cmp_k2[l],
                           w_cmp_v1[l], w_cmp_v2[l])
        o_nsa = _nsa_attn(qt, kc, vc, tab_c, ks, kw, vst, vwt, near, gt)
        x = _out_ffn(o_nsa, o_conv, x, w_out[l], norm_mix_post[l], norm_ffn_pre[l],
                     w_ffn_up[l], ffn_conv_w[l], w_ffn_down[l], norm_ffn_post[l])
    return x
```

```python
import functools
import math

import numpy as np
import jax
import jax.numpy as jnp
from jax import lax
from jax.experimental import pallas as pl
from jax.experimental.pallas import tpu as pltpu

D_MODEL = 1024
D_NSA = 512
D_CONV = 512
HEAD_DIM = 64
N_HEADS = 8
N_KV = 2
REP = 4
KV_W = 128
N_GATES = 24
L_CMP = 32
S_CMP = 16
L_SLC = 64
N_SEL = 16
WINDOW = 512
CMP_HIDDEN = 256
D_FF = 2816
N_BUCKETS = 32
MAX_DIST = 128
RMS_EPS = 1e-6
MASK_NEG = -1e30
N_FREE = N_SEL - 3
LOG2E = math.log2(math.e)

TQ = 256
TK = 256
V_ROWS = 80
C_ROWS = V_ROWS + 64
N_CMP_PAD = 256
TM_IN = 1024
TM_FFN = 1024
FF_CHUNK = 256
VMEM_LIMIT = 56 * 1024 * 1024

_NT = (((1,), (1,)), ((), ()))


def _bucket_np(d):
    max_exact = N_BUCKETS // 2
    d = np.maximum(d, 0)
    df = np.maximum(d, 1).astype(np.float32)
    large = max_exact + (np.log(df / max_exact) / math.log(MAX_DIST / max_exact)
                         * (N_BUCKETS - max_exact)).astype(np.int32)
    return np.where(d < max_exact, d, np.minimum(large, N_BUCKETS - 1)).astype(np.int32)


def _bucket_or_mask(d):
    return np.where(d >= 0, _bucket_np(d), -1).astype(np.int32)


def _rms(x, g):
    return x * lax.rsqrt(jnp.mean(x * x, axis=-1, keepdims=True) + RMS_EPS) * g


def _gelu(x):
    return jax.nn.gelu(x, approximate=True)


def _band_box(idx):
    rows, cols = np.nonzero((idx >= 0) & (idx < N_BUCKETS - 1))
    r0, r1, c0, c1 = int(rows.min()), int(rows.max()) + 1, int(cols.min()), int(cols.max()) + 1
    return r0 // 8 * 8, -(-r1 // 8) * 8, c0 // 128 * 128, -(-c1 // 128) * 128


def _bias_tables_kernel(boxes, rb_ref, idx_c_ref, idx_n_ref, tab_c_ref, tab_n_ref):
    h = pl.program_id(0)
    last = rb_ref[h, N_BUCKETS - 1]

    def fill(idx, out, box):
        out[...] = jnp.where(idx < 0, MASK_NEG, 0.0)
        r0, r1, c0, c1 = box
        sub = idx[r0:r1, c0:c1]
        val = jnp.zeros(sub.shape, jnp.float32)
        for b in range(N_BUCKETS - 1):
            val = jnp.where(sub == b, rb_ref[h, b] - last, val)
        out[r0:r1, c0:c1] = jnp.where(sub < 0, MASK_NEG, val * LOG2E)

    fill(idx_c_ref[...], tab_c_ref.at[0], boxes[0])
    for k in range(idx_n_ref.shape[0]):
        fill(idx_n_ref[k], tab_n_ref.at[0, k], boxes[1 + k])


def _bias_tables(rel_bias):
    u = np.arange(512)[:, None] - 240
    row = np.arange(TQ)[None, :]
    d_c = row - S_CMP * u - (L_CMP - 1)
    idx_c = _bucket_or_mask(d_c)
    key = np.arange(TK)[:, None]
    idx_n = np.stack([_bucket_or_mask(dd + row - key) for dd in (0, TK)])
    boxes = tuple(_band_box(t) for t in (idx_c, *idx_n))
    return pl.pallas_call(
        functools.partial(_bias_tables_kernel, boxes),
        grid=(N_HEADS,),
        in_specs=[pl.BlockSpec(memory_space=pltpu.SMEM),
                  pl.BlockSpec((512, TQ), lambda h: (0, 0)),
                  pl.BlockSpec((2, TK, TQ), lambda h: (0, 0, 0))],
        out_specs=[pl.BlockSpec((1, 512, TQ), lambda h: (h, 0, 0)),
                   pl.BlockSpec((1, 2, TK, TQ), lambda h: (h, 0, 0, 0))],
        out_shape=[jax.ShapeDtypeStruct((N_HEADS, 512, TQ), jnp.float32),
                   jax.ShapeDtypeStruct((N_HEADS, 2, TK, TQ), jnp.float32)],
        name="bias_tables",
    )(rel_bias, jnp.asarray(idx_c), jnp.asarray(idx_n))


def _in_proj_kernel(x_ref, g_ref, wq_ref, wnat_ref, wvt_ref, wgt_ref, cw_ref,
                    qt_ref, kc_ref, vc_ref, ks_ref, kw_ref, vst_ref, vwt_ref, gt_ref,
                    oconv_ref, carry_ref, stage_ref):
    i = pl.program_id(1)
    tm = x_ref.shape[1]
    hb = _rms(x_ref[0], g_ref[...]).astype(jnp.bfloat16)

    qt_ref[0] = lax.dot_general(wq_ref[...], hb, _NT,
                                preferred_element_type=jnp.float32).astype(jnp.bfloat16)
    vt = lax.dot_general(wvt_ref[...], hb, _NT, preferred_element_type=jnp.float32)
    ones = jnp.ones((V_ROWS - HEAD_DIM, TK), jnp.bfloat16)
    for t in range(tm // TK):
        for g in range(N_KV):
            for out_ref, r0 in ((vst_ref, g * HEAD_DIM), (vwt_ref, KV_W + g * HEAD_DIM)):
                out_ref[0, t, g, 0:HEAD_DIM, :] = vt[r0:r0 + HEAD_DIM,
                                                      t * TK:(t + 1) * TK].astype(jnp.bfloat16)
                out_ref[0, t, g, HEAD_DIM:V_ROWS, :] = ones
    gl = lax.dot_general(wgt_ref[...], hb, _NT, preferred_element_type=jnp.float32)
    gt_ref[0] = jax.nn.sigmoid(gl)

    kv_c = jnp.dot(hb, wnat_ref[:, 0:2 * KV_W], preferred_element_type=jnp.float32)
    for a, out_ref in enumerate((kc_ref, vc_ref)):
        stage_ref[a] = kv_c[:, a * KV_W:(a + 1) * KV_W]
        for l in range(S_CMP):
            rows = stage_ref[a, pl.ds(l, tm // S_CMP, stride=S_CMP), :]
            for g in range(N_KV):
                out_ref[0, g, :, l * HEAD_DIM:(l + 1) * HEAD_DIM] = rows[
                    :, g * HEAD_DIM:(g + 1) * HEAD_DIM]
    k_sw = jnp.dot(hb, wnat_ref[:, 256:512], preferred_element_type=jnp.float32)
    blk = (i * tm + lax.broadcasted_iota(jnp.int32, (tm, KV_W), 0)) // L_SLC
    col = lax.broadcasted_iota(jnp.int32, (tm, KV_W), 1)
    onehot = jnp.where(blk == col, 1.0, 0.0)
    ks_ref[0, :, 0:KV_W] = k_sw[:, 0:KV_W].astype(jnp.bfloat16)
    ks_ref[0, :, KV_W:2 * KV_W] = onehot.astype(jnp.bfloat16)
    kw_ref[0] = k_sw[:, KV_W:2 * KV_W].astype(jnp.bfloat16)

    bg = jnp.dot(hb, wnat_ref[:, 512:1024], preferred_element_type=jnp.float32)
    cg = jnp.dot(hb, wnat_ref[:, 1024:1536], preferred_element_type=jnp.float32)
    xt = jnp.dot(hb, wnat_ref[:, 1536:2048], preferred_element_type=jnp.float32)
    z = cg * xt

    @pl.when(i == 0)
    def _():
        carry_ref[...] = jnp.zeros_like(carry_ref)

    prev1 = carry_ref[7:8, :]
    prev2 = carry_ref[6:7, :]
    row = lax.broadcasted_iota(jnp.int32, z.shape, 0)
    z1 = jnp.where(row == 0, prev1, pltpu.roll(z, 1, 0))
    z2 = jnp.where(row == 0, prev2, jnp.where(row == 1, prev1, pltpu.roll(z, 2, 0)))
    y = cw_ref[0:1, :] * z2 + cw_ref[1:2, :] * z1 + cw_ref[2:3, :] * z
    oconv_ref[0] = (bg * y).astype(jnp.bfloat16)
    carry_ref[...] = z[tm - 8:tm, :]


def _in_proj(x, g_pre, w_in, conv_w):
    B, S, D = x.shape
    tm = TM_IN
    wq = (w_in[:, 0:512].T * (HEAD_DIM ** -0.5 * LOG2E)).astype(jnp.bfloat16)
    wnat = jnp.concatenate([w_in[:, 512:896], w_in[:, 1024:1152],
                            w_in[:, 1304:2840]], axis=1).astype(jnp.bfloat16)
    wvt = jnp.concatenate([w_in[:, 896:1024], w_in[:, 1152:1280]], axis=1).T.astype(jnp.bfloat16)
    wgt = jnp.pad(w_in[:, 1280:1304], ((0, 0), (0, 8))).T.astype(jnp.bfloat16)
    nt = S // TK
    const = lambda b, i: (0, 0)
    outs = pl.pallas_call(
        _in_proj_kernel,
        grid=(B, S // tm),
        in_specs=[pl.BlockSpec((1, tm, D), lambda b, i: (b, i, 0)),
                  pl.BlockSpec((1, D), const),
                  pl.BlockSpec((512, D), const),
                  pl.BlockSpec((D, 2048), const),
                  pl.BlockSpec((256, D), const),
                  pl.BlockSpec((32, D), const),
                  pl.BlockSpec((3, D_CONV), const)],
        out_specs=[pl.BlockSpec((1, 512, tm), lambda b, i: (b, 0, i)),
                   pl.BlockSpec((1, N_KV, tm // S_CMP, S_CMP * HEAD_DIM), lambda b, i: (b, 0, i, 0)),
                   pl.BlockSpec((1, N_KV, tm // S_CMP, S_CMP * HEAD_DIM), lambda b, i: (b, 0, i, 0)),
                   pl.BlockSpec((1, tm, 256), lambda b, i: (b, i, 0)),
                   pl.BlockSpec((1, tm, 128), lambda b, i: (b, i, 0)),
                   pl.BlockSpec((1, tm // TK, N_KV, V_ROWS, TK), lambda b, i: (b, i, 0, 0, 0)),
                   pl.BlockSpec((1, tm // TK, N_KV, V_ROWS, TK), lambda b, i: (b, i, 0, 0, 0)),
                   pl.BlockSpec((1, 32, tm), lambda b, i: (b, 0, i)),
                   pl.BlockSpec((1, tm, D_CONV), lambda b, i: (b, i, 0))],
        out_shape=[jax.ShapeDtypeStruct((B, 512, S), jnp.bfloat16),
                   jax.ShapeDtypeStruct((B, N_KV, S // S_CMP, S_CMP * HEAD_DIM), jnp.float32),
                   jax.ShapeDtypeStruct((B, N_KV, S // S_CMP, S_CMP * HEAD_DIM), jnp.float32),
                   jax.ShapeDtypeStruct((B, S, 256), jnp.bfloat16),
                   jax.ShapeDtypeStruct((B, S, 128), jnp.bfloat16),
                   jax.ShapeDtypeStruct((B, nt, N_KV, V_ROWS, TK), jnp.bfloat16),
                   jax.ShapeDtypeStruct((B, nt, N_KV, V_ROWS, TK), jnp.bfloat16),
                   jax.ShapeDtypeStruct((B, 32, S), jnp.float32),
                   jax.ShapeDtypeStruct((B, S, D_CONV), jnp.bfloat16)],
        scratch_shapes=[pltpu.VMEM((8, D_CONV), jnp.float32),
                        pltpu.VMEM((2, tm, KV_W), jnp.float32)],
        compiler_params=pltpu.CompilerParams(
            dimension_semantics=("arbitrary", "arbitrary"), vmem_limit_bytes=VMEM_LIMIT),
        name="in_proj",
    )(x, g_pre.reshape(1, D), wq, wnat, wvt, wgt, conv_w)
    return outs


def _compress_kernel(ck_ref, cv_ref, pek_ref, pev_ref, wk1_ref, wk2_ref, wv1_ref, wv2t_ref,
                     ovl_ref, kc_ref, vct_ref):
    def hidden(c_ref, pe_ref, w1_ref):
        c = c_ref[0, 0]
        a = jnp.dot((c + pe_ref[0:1, :]).astype(jnp.bfloat16), w1_ref[0].astype(jnp.bfloat16),
                    preferred_element_type=jnp.float32)
        b = jnp.dot((c + pe_ref[1:2, :]).astype(jnp.bfloat16), w1_ref[1].astype(jnp.bfloat16),
                    preferred_element_type=jnp.float32)
        return _gelu(a + pltpu.roll(b, N_CMP_PAD - 1, 0)).astype(jnp.bfloat16)

    kc_ref[0, 0] = jnp.dot(hidden(ck_ref, pek_ref, wk1_ref), wk2_ref[...].astype(jnp.bfloat16),
                           preferred_element_type=jnp.float32).astype(jnp.bfloat16)
    vct_ref[0, 0, 0:HEAD_DIM, :] = lax.dot_general(
        wv2t_ref[...].astype(jnp.bfloat16), hidden(cv_ref, pev_ref, wv1_ref), _NT,
        preferred_element_type=jnp.float32).astype(jnp.bfloat16)
    vct_ref[0, 0, HEAD_DIM:V_ROWS, :] = jnp.ones((V_ROWS - HEAD_DIM, N_CMP_PAD), jnp.bfloat16)
    vct_ref[0, 0, V_ROWS:C_ROWS, :] = ovl_ref[...]


def _compress(kc_in, vc_in, pe_k, pe_v, wk1, wk2, wv1, wv2):
    B, _, n_str, half = kc_in.shape
    const2 = lambda b, g: (0, 0)
    const3 = lambda b, g: (0, 0, 0)
    return pl.pallas_call(
        _compress_kernel,
        grid=(B, N_KV),
        in_specs=[pl.BlockSpec((1, 1, n_str, half), lambda b, g: (b, g, 0, 0)),
                  pl.BlockSpec((1, 1, n_str, half), lambda b, g: (b, g, 0, 0)),
                  pl.BlockSpec((2, half), const2),
                  pl.BlockSpec((2, half), const2),
                  pl.BlockSpec((2, half, CMP_HIDDEN), const3),
                  pl.BlockSpec((CMP_HIDDEN, HEAD_DIM), const2),
                  pl.BlockSpec((2, half, CMP_HIDDEN), const3),
                  pl.BlockSpec((HEAD_DIM, CMP_HIDDEN), const2),
                  pl.BlockSpec((C_ROWS - V_ROWS, n_str), const2)],
        out_specs=[pl.BlockSpec((1, 1, n_str, HEAD_DIM), lambda b, g: (b, g, 0, 0)),
                   pl.BlockSpec((1, 1, C_ROWS, n_str), lambda b, g: (b, g, 0, 0))],
        out_shape=[jax.ShapeDtypeStruct((B, N_KV, n_str, HEAD_DIM), jnp.bfloat16),
                   jax.ShapeDtypeStruct((B, N_KV, C_ROWS, n_str), jnp.bfloat16)],
        compiler_params=pltpu.CompilerParams(
            dimension_semantics=("arbitrary", "arbitrary"), vmem_limit_bytes=VMEM_LIMIT),
        name="compress",
    )(kc_in, vc_in, pe_k.reshape(2, half), pe_v.reshape(2, half),
      wk1.reshape(2, half, CMP_HIDDEN), wk2, wv1.reshape(2, half, CMP_HIDDEN), wv2.T,
      _overlap_t(n_str * S_CMP))


def _overlap_t(S):
    n_slc = S // L_SLC
    c_start = np.arange(N_CMP_PAD) * S_CMP
    s_start = np.arange(n_slc) * L_SLC
    ov = np.clip(np.minimum(c_start[:, None] + L_CMP, s_start[None, :] + L_SLC)
                 - np.maximum(c_start[:, None], s_start[None, :]), 0, None).astype(np.float32) / L_CMP
    ov[(S - L_CMP) // S_CMP + 1:, :] = 0.0
    return jnp.asarray(ov.T, dtype=jnp.bfloat16)


def _select_top(score, k, tri):
    w = score
    cnt = jnp.zeros((1, TQ), jnp.float32)
    thr = jnp.zeros((1, TQ), jnp.float32)
    n_gt = jnp.zeros((1, TQ), jnp.float32)
    for _ in range(N_FREE):
        mx = jnp.max(w, axis=0, keepdims=True)
        eq = w == mx
        c = jnp.sum(jnp.where(eq, 1.0, 0.0), axis=0, keepdims=True)
        cross = (cnt < k) & (cnt + c >= k)
        thr = jnp.where(cross, mx, thr)
        n_gt = jnp.where(cross, cnt, n_gt)
        cnt = cnt + c
        w = jnp.where(eq, -jnp.inf, w)
    at_thr = score == thr
    earlier = jnp.dot(tri, jnp.where(at_thr, 1.0, 0.0).astype(jnp.bfloat16),
                      preferred_element_type=jnp.float32)
    return (score > thr) | (at_thr & (earlier + n_gt < k))


def _nsa_attn_kernel(qt_ref, kc_ref, vc_ref, tabc_ref, tri_ref, ks_ref, kw_ref, vst_ref, vwt_ref,
                     near_ref, edge_ref, gt_ref, o_ref,
                     m_ref, acc_ref, mw_ref, accw_ref, s_ref, tmax_ref, sel_ref, oc_ref):
    i = pl.program_id(1)
    zeros64 = jnp.zeros((HEAD_DIM, TQ), jnp.bfloat16)
    off1 = jnp.where(i >= 1, 0.0, MASK_NEG)
    off2 = jnp.where(i >= 2, 0.0, MASK_NEG)
    j1 = jnp.maximum(i - 1, 0)
    j2 = jnp.maximum(i - 2, 0)
    dot = functools.partial(jnp.dot, preferred_element_type=jnp.float32)

    def q_window(h):
        qh = qt_ref[0, h * HEAD_DIM:(h + 1) * HEAD_DIM, :]
        return jnp.concatenate([qh, zeros64] if h // REP == 0 else [zeros64, qh], axis=0)

    def q_selected(h):
        return jnp.concatenate([q_window(h), sel_ref[h // REP]], axis=0)

    def stage_logits(slot, k_ref, j, q_of, bias_of=None):
        k = k_ref[0, pl.ds(pl.multiple_of(j * TK, TK), TK), :]
        for h in range(N_HEADS):
            s = dot(k, q_of(h))
            if bias_of is not None:
                s = s + bias_of(h)
            s_ref[slot, h] = s
            tmax_ref[slot, h:h + 1, :] = jnp.max(s, axis=0, keepdims=True)

    def accumulate(slot, v_ref, j, state, first=False, shift=None):
        ms_ref, as_ref = state
        for h in range(N_HEADS):
            t_max = tmax_ref[slot, h:h + 1, :]
            if shift is not None:
                t_max = t_max + shift
            m_new = t_max if first else jnp.maximum(ms_ref[h:h + 1, :], t_max)
            m_sub = m_new if shift is None else m_new - shift
            p = jnp.exp2(s_ref[slot, h] - m_sub).astype(jnp.bfloat16)
            pv = dot(v_ref[0, j, h // REP], p)
            if first:
                as_ref[h] = pv
            else:
                as_ref[h] = jnp.exp2(ms_ref[h:h + 1, :] - m_new) * as_ref[h] + pv
            ms_ref[h:h + 1, :] = m_new

    sel_state = (m_ref, acc_ref)
    win_state = (mw_ref, accw_ref)
    near0 = lambda h: near_ref[h, 0]
    near1 = lambda h: near_ref[h, 1]
    edge = lambda h: edge_ref[...]

    off_c = pl.multiple_of(240 - 16 * i, 16)
    for h in range(N_HEADS):
        s = (dot(kc_ref[0, h // REP], qt_ref[0, h * HEAD_DIM:(h + 1) * HEAD_DIM, :])
             + tabc_ref[h, pl.ds(off_c, N_CMP_PAD), :])
        s_ref[0, h] = s
        tmax_ref[0, h:h + 1, :] = jnp.max(s, axis=0, keepdims=True)
    stage_logits(1, kw_ref, i, q_window, near0)

    t_row = i * TQ + lax.broadcasted_iota(jnp.int32, (1, TQ), 1)
    any_cmp = jnp.where(t_row >= L_CMP - 1, 1.0, 0.0)
    n_blk = tri_ref.shape[0]
    j_idx = lax.broadcasted_iota(jnp.int32, (n_blk, TQ), 0)
    cur = (i * TQ + lax.broadcasted_iota(jnp.int32, (n_blk, TQ), 1)) // L_SLC
    forced = (j_idx == 0) | (j_idx == cur) | (j_idx == cur - 1)
    candidate = (j_idx < cur - 1) & (j_idx > 0)
    n_free = float(N_SEL - 1) - jnp.minimum(t_row // L_SLC, 2).astype(jnp.float32)
    for g in range(N_KV):
        imp = jnp.zeros((n_blk, TQ), jnp.float32)
        for h in range(g * REP, (g + 1) * REP):
            p = jnp.exp2(s_ref[0, h] - tmax_ref[0, h:h + 1, :]).astype(jnp.bfloat16)
            a = dot(vc_ref[0, g], p)
            inv = any_cmp / a[HEAD_DIM:HEAD_DIM + 1, :]
            oc_ref[h * HEAD_DIM:(h + 1) * HEAD_DIM, :] = a[0:HEAD_DIM, :] * inv
            imp = imp + a[V_ROWS:V_ROWS + n_blk, :] * inv
        free = _select_top(jnp.where(candidate, imp, MASK_NEG), n_free, tri_ref[...])
        sel = forced | (candidate & free)
        sel_ref[g, 0:n_blk, :] = jnp.where(sel, 0.0, MASK_NEG).astype(jnp.bfloat16)
        sel_ref[g, n_blk:2 * n_blk, :] = jnp.zeros((n_blk, TQ), jnp.bfloat16)

    accumulate(1, vwt_ref, i, win_state, first=True)
    stage_logits(0, ks_ref, i, q_selected, near0)
    stage_logits(1, kw_ref, j1, q_window, near1)
    accumulate(0, vst_ref, i, sel_state, first=True)
    stage_logits(0, ks_ref, j1, q_selected, near1)
    accumulate(1, vwt_ref, j1, win_state, shift=off1)
    stage_logits(1, kw_ref, j2, q_window, edge)
    accumulate(0, vst_ref, j1, sel_state, shift=off1)

    n_far = j1
    last = jnp.maximum(n_far - 1, 0)
    stage_logits(0, ks_ref, 0, q_selected)
    accumulate(1, vwt_ref, j2, win_state, shift=off2)

    def far_pair(t):
        stage_logits(1, ks_ref, t + 1, q_selected)
        accumulate(0, vst_ref, t, sel_state)
        stage_logits(0, ks_ref, jnp.minimum(t + 2, last), q_selected)
        accumulate(1, vst_ref, t + 1, sel_state)

    def far_quad(c, carry):
        far_pair(4 * c)
        far_pair(4 * c + 2)
        return carry

    lax.fori_loop(0, n_far // 4, far_quad, 0)

    @pl.when(n_far % 4 >= 2)
    def _():
        far_pair(n_far // 4 * 4)

    @pl.when(n_far % 2 == 1)
    def _():
        accumulate(0, vst_ref, last, sel_state)

    for pair in range(N_HEADS // 2):
        outs = []
        for h in (2 * pair, 2 * pair + 1):
            rows = slice(h * HEAD_DIM, (h + 1) * HEAD_DIM)
            o_s = acc_ref[h, 0:HEAD_DIM, :] / acc_ref[h, HEAD_DIM:HEAD_DIM + 1, :]
            o_w = accw_ref[h, 0:HEAD_DIM, :] / accw_ref[h, HEAD_DIM:HEAD_DIM + 1, :]
            outs.append(gt_ref[0, 3 * h:3 * h + 1, :] * oc_ref[rows, :]
                        + gt_ref[0, 3 * h + 1:3 * h + 2, :] * o_s
                        + gt_ref[0, 3 * h + 2:3 * h + 3, :] * o_w)
        o_pair = jnp.concatenate(outs, axis=0)
        o_ref[0, :, pair * 128:(pair + 1) * 128] = o_pair.T.astype(jnp.bfloat16)


def _nsa_attn(qt, kc, vc, tab_c, ks, kw, vst, vwt, near, gt):
    B, _, S = qt.shape
    ni = S // TQ
    nt = S // TK
    n_blk = S // L_SLC
    key = np.arange(TK)[:, None]
    row = np.arange(TQ)[None, :]
    edge = jnp.asarray(np.where(2 * TK + row - key < WINDOW, 0.0, MASK_NEG).astype(np.float32))
    tri = jnp.asarray(np.tril(np.ones((n_blk, n_blk), np.float32), -1), dtype=jnp.bfloat16)
    per_batch = lambda shape: pl.BlockSpec((1,) + shape, lambda b, i: (b,) + (0,) * len(shape))
    const = lambda shape: pl.BlockSpec(shape, lambda b, i: (0,) * len(shape),
                                       pipeline_mode=pl.Buffered(1))
    return pl.pallas_call(
        _nsa_attn_kernel,
        grid=(B, ni),
        in_specs=[pl.BlockSpec((1, D_NSA, TQ), lambda b, i: (b, 0, i)),
                  per_batch((N_KV, N_CMP_PAD, HEAD_DIM)),
                  per_batch((N_KV, C_ROWS, N_CMP_PAD)),
                  const((N_HEADS, 512, TQ)),
                  const((n_blk, n_blk)),
                  per_batch((S, 256)),
                  per_batch((S, 128)),
                  per_batch((nt, N_KV, V_ROWS, TK)),
                  per_batch((nt, N_KV, V_ROWS, TK)),
                  const((N_HEADS, 2, TK, TQ)),
                  const((TK, TQ)),
                  pl.BlockSpec((1, 32, TQ), lambda b, i: (b, 0, i))],
        out_specs=pl.BlockSpec((1, TQ, D_NSA), lambda b, i: (b, i, 0)),
        out_shape=jax.ShapeDtypeStruct((B, S, D_NSA), jnp.bfloat16),
        scratch_shapes=[pltpu.VMEM((N_HEADS, TQ), jnp.float32),
                        pltpu.VMEM((N_HEADS, V_ROWS, TQ), jnp.float32),
                        pltpu.VMEM((N_HEADS, TQ), jnp.float32),
                        pltpu.VMEM((N_HEADS, V_ROWS, TQ), jnp.float32),
                        pltpu.VMEM((2, N_HEADS, TK, TQ), jnp.float32),
                        pltpu.VMEM((2, N_HEADS, TQ), jnp.float32),
                        pltpu.VMEM((N_KV, 128, TQ), jnp.bfloat16),
                        pltpu.VMEM((D_NSA, TQ), jnp.float32)],
        compiler_params=pltpu.CompilerParams(
            dimension_semantics=("arbitrary", "arbitrary"), vmem_limit_bytes=VMEM_LIMIT),
        name="nsa_attn",
    )(qt, kc, vc, tab_c, tri, ks, kw, vst, vwt, near, edge, gt)


def _out_ffn_kernel(on_ref, oc_ref, x_ref, wo_ref, gpost_ref, gffn_ref, wup_ref, cw_ref, wdn_ref,
                    g_ref, o_ref, carry_ref, act_ref):
    i = pl.program_id(1)
    tm = x_ref.shape[1]
    y = jnp.dot(on_ref[0], wo_ref[0:D_NSA, :], preferred_element_type=jnp.float32)
    y = y + jnp.dot(oc_ref[0], wo_ref[D_NSA:D_NSA + D_CONV, :], preferred_element_type=jnp.float32)
    x1 = x_ref[0] + _rms(y, gpost_ref[...])
    hb = _rms(x1, gffn_ref[...]).astype(jnp.bfloat16)

    @pl.when(i == 0)
    def _():
        carry_ref[...] = jnp.zeros_like(carry_ref)

    row = lax.broadcasted_iota(jnp.int32, (tm, FF_CHUNK), 0)
    is0 = row == 0
    is1 = row == 1

    def conv(z, c0):
        cols = slice(c0, c0 + FF_CHUNK)
        prev1 = carry_ref[7:8, cols]
        prev2 = carry_ref[6:7, cols]
        z1 = jnp.where(is0, prev1, pltpu.roll(z, 1, 0))
        z2 = jnp.where(is0, prev2, jnp.where(is1, prev1, pltpu.roll(z, 2, 0)))
        carry_ref[:, cols] = z[tm - 8:tm, :]
        return cw_ref[0:1, cols] * z2 + cw_ref[1:2, cols] * z1 + cw_ref[2:3, cols] * z

    for c in range(D_FF // FF_CHUNK):
        g0 = c * FF_CHUNK
        u0 = D_FF + c * FF_CHUNK
        zg = jnp.dot(hb, wup_ref[:, g0:g0 + FF_CHUNK], preferred_element_type=jnp.float32)
        zu = jnp.dot(hb, wup_ref[:, u0:u0 + FF_CHUNK], preferred_element_type=jnp.float32)
        act = _gelu(conv(zg, g0)) * conv(zu, u0)
        act_ref[:, g0:g0 + FF_CHUNK] = act.astype(jnp.bfloat16)
    y2 = jnp.dot(act_ref[...], wdn_ref[...], preferred_element_type=jnp.float32)
    o_ref[0] = x1 + _rms(y2, g_ref[...])


def _out_ffn(o_nsa, o_conv, x, w_out, g_post, g_ffn, w_up, conv_w, w_down, g_ffn_post):
    B, S, D = x.shape
    tm = TM_FFN
    const = lambda b, i: (0, 0)
    single = pl.Buffered(1)
    row_tile = lambda width: pl.BlockSpec((1, tm, width), lambda b, i: (b, i, 0))
    weight = lambda shape: pl.BlockSpec(shape, const, pipeline_mode=single)
    return pl.pallas_call(
        _out_ffn_kernel,
        grid=(B, S // tm),
        in_specs=[row_tile(D_NSA), row_tile(D_CONV), row_tile(D),
                  weight((D, D)), weight((1, D)), weight((1, D)),
                  weight((D, 2 * D_FF)), weight((3, 2 * D_FF)), weight((D_FF, D)), weight((1, D))],
        out_specs=row_tile(D),
        out_shape=jax.ShapeDtypeStruct((B, S, D), jnp.float32),
        scratch_shapes=[pltpu.VMEM((8, 2 * D_FF), jnp.float32),
                        pltpu.VMEM((tm, D_FF), jnp.bfloat16)],
        compiler_params=pltpu.CompilerParams(
            dimension_semantics=("arbitrary", "arbitrary"), vmem_limit_bytes=VMEM_LIMIT),
        name="out_ffn",
    )(o_nsa, o_conv, x, w_out.astype(jnp.bfloat16), g_post.reshape(1, D), g_ffn.reshape(1, D),
      w_up.astype(jnp.bfloat16), conv_w, w_down.astype(jnp.bfloat16), g_ffn_post.reshape(1, D))


def kernel(x, norm_mix_pre, norm_mix_post, norm_ffn_pre, norm_ffn_post, w_in, pe_cmp_k, pe_cmp_v,
           w_cmp_k1, w_cmp_k2, w_cmp_v1, w_cmp_v2, rel_bias, conv_mix_w, w_out, w_ffn_up,
           ffn_conv_w, w_ffn_down):
    B, S, D = x.shape
    assert (S, D) == (4096, D_MODEL) and norm_mix_pre.shape[0] == 1
    tab_c, near = _bias_tables(rel_bias)
    for l in range(norm_mix_pre.shape[0]):
        qt, kc_in, vc_in, ks, kw, vst, vwt, gt, o_conv = _in_proj(
            x, norm_mix_pre[l], w_in[l], conv_mix_w[l])
        kc, vc = _compress(kc_in, vc_in, pe_cmp_k[l], pe_cmp_v[l], w_cmp_k1[l], w_cmp_k2[l],
                           w_cmp_v1[l], w_cmp_v2[l])
        o_nsa = _nsa_attn(qt, kc, vc, tab_c, ks, kw, vst, vwt, near, gt)
        x = _out_ffn(o_nsa, o_conv, x, w_out[l], norm_mix_post[l], norm_ffn_pre[l],
                     w_ffn_up[l], ffn_conv_w[l], w_ffn_down[l], norm_ffn_post[l])
    return x
```

```python
import functools
import math

import numpy as np
import jax
import jax.numpy as jnp
from jax import lax
from jax.experimental import pallas as pl
from jax.experimental.pallas import tpu as pltpu

D_MODEL = 1024
D_NSA = 512
D_CONV = 512
HEAD_DIM = 64
N_HEADS = 8
N_KV = 2
REP = 4
KV_W = 128
N_GATES = 24
L_CMP = 32
S_CMP = 16
L_SLC = 64
N_SEL = 16
WINDOW = 512
CMP_HIDDEN = 256
D_FF = 2816
N_BUCKETS = 32
MAX_DIST = 128
RMS_EPS = 1e-6
MASK_NEG = -1e30
N_FREE = N_SEL - 3
LOG2E = math.log2(math.e)

TQ = 256
TK = 256
V_ROWS = 80
C_ROWS = V_ROWS + 64
N_CMP_PAD = 256
TM_IN = 1024
TM_FFN = 1024
FF_CHUNK = 256
VMEM_LIMIT = 56 * 1024 * 1024

_NT = (((1,), (1,)), ((), ()))


def _bucket_np(d):
    max_exact = N_BUCKETS // 2
    d = np.maximum(d, 0)
    df = np.maximum(d, 1).astype(np.float32)
    large = max_exact + (np.log(df / max_exact) / math.log(MAX_DIST / max_exact)
                         * (N_BUCKETS - max_exact)).astype(np.int32)
    return np.where(d < max_exact, d, np.minimum(large, N_BUCKETS - 1)).astype(np.int32)


def _bucket_or_mask(d):
    return np.where(d >= 0, _bucket_np(d), -1).astype(np.int32)


def _rms(x, g):
    return x * lax.rsqrt(jnp.mean(x * x, axis=-1, keepdims=True) + RMS_EPS) * g


def _gelu(x):
    return jax.nn.gelu(x, approximate=True)


def _band_box(idx):
    rows, cols = np.nonzero((idx >= 0) & (idx < N_BUCKETS - 1))
    r0, r1, c0, c1 = int(rows.min()), int(rows.max()) + 1, int(cols.min()), int(cols.max()) + 1
    return r0 // 8 * 8, -(-r1 // 8) * 8, c0 // 128 * 128, -(-c1 // 128) * 128


def _bias_tables_kernel(boxes, rb_ref, idx_c_ref, idx_n_ref, tab_c_ref, tab_n_ref):
    h = pl.program_id(0)
    last = rb_ref[h, N_BUCKETS - 1]

    def fill(idx, out, box):
        out[...] = jnp.where(idx < 0, MASK_NEG, 0.0)
        r0, r1, c0, c1 = box
        sub = idx[r0:r1, c0:c1]
        val = jnp.zeros(sub.shape, jnp.float32)
        for b in range(N_BUCKETS - 1):
            val = jnp.where(sub == b, rb_ref[h, b] - last, val)
        out[r0:r1, c0:c1] = jnp.where(sub < 0, MASK_NEG, val * LOG2E)

    fill(idx_c_ref[...], tab_c_ref.at[0], boxes[0])
    for k in range(idx_n_ref.shape[0]):
        fill(idx_n_ref[k], tab_n_ref.at[0, k], boxes[1 + k])


def _bias_tables(rel_bias):
    u = np.arange(512)[:, None] - 240
    row = np.arange(TQ)[None, :]
    d_c = row - S_CMP * u - (L_CMP - 1)
    idx_c = _bucket_or_mask(d_c)
    key = np.arange(TK)[:, None]
    idx_n = np.stack([_bucket_or_mask(dd + row - key) for dd in (0, TK)])
    boxes = tuple(_band_box(t) for t in (idx_c, *idx_n))
    return pl.pallas_call(
        functools.partial(_bias_tables_kernel, boxes),
        grid=(N_HEADS,),
        in_specs=[pl.BlockSpec(memory_space=pltpu.SMEM),
                  pl.BlockSpec((512, TQ), lambda h: (0, 0)),
                  pl.BlockSpec((2, TK, TQ), lambda h: (0, 0, 0))],
        out_specs=[pl.BlockSpec((1, 512, TQ), lambda h: (h, 0, 0)),
                   pl.BlockSpec((1, 2, TK, TQ), lambda h: (h, 0, 0, 0))],
        out_shape=[jax.ShapeDtypeStruct((N_HEADS, 512, TQ), jnp.float32),
                   jax.ShapeDtypeStruct((N_HEADS, 2, TK, TQ), jnp.float32)],
        name="bias_tables",
    )(rel_bias, jnp.asarray(idx_c), jnp.asarray(idx_n))


def _in_proj_kernel(x_ref, g_ref, wq_ref, wnat_ref, wvt_ref, wgt_ref, cw_ref,
                    qt_ref, kc_ref, vc_ref, ks_ref, kw_ref, vst_ref, vwt_ref, gt_ref,
                    oconv_ref, carry_ref, stage_ref):
    i = pl.program_id(1)
    tm = x_ref.shape[1]

    @pl.when(i == 0)
    def _():
        carry_ref[...] = jnp.zeros_like(carry_ref)

    hb = _rms(x_ref[0], g_ref[...]).astype(jnp.bfloat16)

    qt_ref[0] = lax.dot_general(wq_ref[...], hb, _NT,
                                preferred_element_type=jnp.float32).astype(jnp.bfloat16)
    vt = lax.dot_general(wvt_ref[...], hb, _NT, preferred_element_type=jnp.float32)
    ones = jnp.ones((V_ROWS - HEAD_DIM, TK), jnp.bfloat16)
    for t in range(tm // TK):
        for g in range(N_KV):
            for out_ref, r0 in ((vst_ref, g * HEAD_DIM), (vwt_ref, KV_W + g * HEAD_DIM)):
                out_ref[0, t, g, 0:HEAD_DIM, :] = vt[r0:r0 + HEAD_DIM,
                                                      t * TK:(t + 1) * TK].astype(jnp.bfloat16)
                out_ref[0, t, g, HEAD_DIM:V_ROWS, :] = ones
    gl = lax.dot_general(wgt_ref[...], hb, _NT, preferred_element_type=jnp.float32)
    gt_ref[0] = jax.nn.sigmoid(gl)

    kv_c = jnp.dot(hb, wnat_ref[:, 0:2 * KV_W], preferred_element_type=jnp.float32)
    for a, out_ref in enumerate((kc_ref, vc_ref)):
        stage_ref[a] = kv_c[:, a * KV_W:(a + 1) * KV_W]
        for l in range(S_CMP):
            rows = stage_ref[a, pl.ds(l, tm // S_CMP, stride=S_CMP), :]
            for g in range(N_KV):
                out_ref[0, g, :, l * HEAD_DIM:(l + 1) * HEAD_DIM] = rows[
                    :, g * HEAD_DIM:(g + 1) * HEAD_DIM]
    k_sw = jnp.dot(hb, wnat_ref[:, 256:512], preferred_element_type=jnp.float32)
    blk = (i * tm + lax.broadcasted_iota(jnp.int32, (tm, KV_W), 0)) // L_SLC
    col = lax.broadcasted_iota(jnp.int32, (tm, KV_W), 1)
    onehot = jnp.where(blk == col, 1.0, 0.0)
    ks_ref[0, :, 0:KV_W] = k_sw[:, 0:KV_W].astype(jnp.bfloat16)
    ks_ref[0, :, KV_W:2 * KV_W] = onehot.astype(jnp.bfloat16)
    kw_ref[0] = k_sw[:, KV_W:2 * KV_W].astype(jnp.bfloat16)

    bg = jnp.dot(hb, wnat_ref[:, 512:1024], preferred_element_type=jnp.float32)
    cg = jnp.dot(hb, wnat_ref[:, 1024:1536], preferred_element_type=jnp.float32)
    xt = jnp.dot(hb, wnat_ref[:, 1536:2048], preferred_element_type=jnp.float32)
    z = cg * xt
    prev1 = carry_ref[7:8, :]
    prev2 = carry_ref[6:7, :]
    row = lax.broadcasted_iota(jnp.int32, z.shape, 0)
    z1 = jnp.where(row == 0, prev1, pltpu.roll(z, 1, 0))
    z2 = jnp.where(row == 0, prev2, jnp.where(row == 1, prev1, pltpu.roll(z, 2, 0)))
    y = cw_ref[0:1, :] * z2 + cw_ref[1:2, :] * z1 + cw_ref[2:3, :] * z
    oconv_ref[0] = (bg * y).astype(jnp.bfloat16)
    carry_ref[...] = z[tm - 8:tm, :]


def _in_proj(x, g_pre, w_in, conv_w):
    B, S, D = x.shape
    tm = TM_IN
    wq = (w_in[:, 0:512].T * (HEAD_DIM ** -0.5 * LOG2E)).astype(jnp.bfloat16)
    wnat = jnp.concatenate([w_in[:, 512:896], w_in[:, 1024:1152],
                            w_in[:, 1304:2840]], axis=1).astype(jnp.bfloat16)
    wvt = jnp.concatenate([w_in[:, 896:1024], w_in[:, 1152:1280]], axis=1).T.astype(jnp.bfloat16)
    wgt = jnp.pad(w_in[:, 1280:1304], ((0, 0), (0, 8))).T.astype(jnp.bfloat16)
    nt = S // TK
    const = lambda b, i: (0, 0)
    outs = pl.pallas_call(
        _in_proj_kernel,
        grid=(B, S // tm),
        in_specs=[pl.BlockSpec((1, tm, D), lambda b, i: (b, i, 0)),
                  pl.BlockSpec((1, D), const),
                  pl.BlockSpec((512, D), const),
                  pl.BlockSpec((D, 2048), const),
                  pl.BlockSpec((256, D), const),
                  pl.BlockSpec((32, D), const),
                  pl.BlockSpec((3, D_CONV), const)],
        out_specs=[pl.BlockSpec((1, 512, tm), lambda b, i: (b, 0, i)),
                   pl.BlockSpec((1, N_KV, tm // S_CMP, S_CMP * HEAD_DIM), lambda b, i: (b, 0, i, 0)),
                   pl.BlockSpec((1, N_KV, tm // S_CMP, S_CMP * HEAD_DIM), lambda b, i: (b, 0, i, 0)),
                   pl.BlockSpec((1, tm, 256), lambda b, i: (b, i, 0)),
                   pl.BlockSpec((1, tm, 128), lambda b, i: (b, i, 0)),
                   pl.BlockSpec((1, tm // TK, N_KV, V_ROWS, TK), lambda b, i: (b, i, 0, 0, 0)),
                   pl.BlockSpec((1, tm // TK, N_KV, V_ROWS, TK), lambda b, i: (b, i, 0, 0, 0)),
                   pl.BlockSpec((1, 32, tm), lambda b, i: (b, 0, i)),
                   pl.BlockSpec((1, tm, D_CONV), lambda b, i: (b, i, 0))],
        out_shape=[jax.ShapeDtypeStruct((B, 512, S), jnp.bfloat16),
                   jax.ShapeDtypeStruct((B, N_KV, S // S_CMP, S_CMP * HEAD_DIM), jnp.float32),
                   jax.ShapeDtypeStruct((B, N_KV, S // S_CMP, S_CMP * HEAD_DIM), jnp.float32),
                   jax.ShapeDtypeStruct((B, S, 256), jnp.bfloat16),
                   jax.ShapeDtypeStruct((B, S, 128), jnp.bfloat16),
                   jax.ShapeDtypeStruct((B, nt, N_KV, V_ROWS, TK), jnp.bfloat16),
                   jax.ShapeDtypeStruct((B, nt, N_KV, V_ROWS, TK), jnp.bfloat16),
                   jax.ShapeDtypeStruct((B, 32, S), jnp.float32),
                   jax.ShapeDtypeStruct((B, S, D_CONV), jnp.bfloat16)],
        scratch_shapes=[pltpu.VMEM((8, D_CONV), jnp.float32),
                        pltpu.VMEM((2, tm, KV_W), jnp.float32)],
        compiler_params=pltpu.CompilerParams(
            dimension_semantics=("arbitrary", "arbitrary"), vmem_limit_bytes=VMEM_LIMIT),
        name="in_proj",
    )(x, g_pre.reshape(1, D), wq, wnat, wvt, wgt, conv_w)
    return outs


def _compress_kernel(ck_ref, cv_ref, pek_ref, pev_ref, wk1_ref, wk2_ref, wv1_ref, wv2t_ref,
                     ovl_ref, kc_ref, vct_ref):
    def hidden(c_ref, pe_ref, w1_ref):
        c = c_ref[0, 0]
        a = jnp.dot((c + pe_ref[0:1, :]).astype(jnp.bfloat16), w1_ref[0].astype(jnp.bfloat16),
                    preferred_element_type=jnp.float32)
        b = jnp.dot((c + pe_ref[1:2, :]).astype(jnp.bfloat16), w1_ref[1].astype(jnp.bfloat16),
                    preferred_element_type=jnp.float32)
        return _gelu(a + pltpu.roll(b, N_CMP_PAD - 1, 0)).astype(jnp.bfloat16)

    kc_ref[0, 0] = jnp.dot(hidden(ck_ref, pek_ref, wk1_ref), wk2_ref[...].astype(jnp.bfloat16),
                           preferred_element_type=jnp.float32).astype(jnp.bfloat16)
    vct_ref[0, 0, 0:HEAD_DIM, :] = lax.dot_general(
        wv2t_ref[...].astype(jnp.bfloat16), hidden(cv_ref, pev_ref, wv1_ref), _NT,
        preferred_element_type=jnp.float32).astype(jnp.bfloat16)
    vct_ref[0, 0, HEAD_DIM:V_ROWS, :] = jnp.ones((V_ROWS - HEAD_DIM, N_CMP_PAD), jnp.bfloat16)
    vct_ref[0, 0, V_ROWS:C_ROWS, :] = ovl_ref[...]


def _compress(kc_in, vc_in, pe_k, pe_v, wk1, wk2, wv1, wv2):
    B, _, n_str, half = kc_in.shape
    const2 = lambda b, g: (0, 0)
    const3 = lambda b, g: (0, 0, 0)
    return pl.pallas_call(
        _compress_kernel,
        grid=(B, N_KV),
        in_specs=[pl.BlockSpec((1, 1, n_str, half), lambda b, g: (b, g, 0, 0)),
                  pl.BlockSpec((1, 1, n_str, half), lambda b, g: (b, g, 0, 0)),
                  pl.BlockSpec((2, half), const2),
                  pl.BlockSpec((2, half), const2),
                  pl.BlockSpec((2, half, CMP_HIDDEN), const3),
                  pl.BlockSpec((CMP_HIDDEN, HEAD_DIM), const2),
                  pl.BlockSpec((2, half, CMP_HIDDEN), const3),
                  pl.BlockSpec((HEAD_DIM, CMP_HIDDEN), const2),
                  pl.BlockSpec((C_ROWS - V_ROWS, n_str), const2)],
        out_specs=[pl.BlockSpec((1, 1, n_str, HEAD_DIM), lambda b, g: (b, g, 0, 0)),
                   pl.BlockSpec((1, 1, C_ROWS, n_str), lambda b, g: (b, g, 0, 0))],
        out_shape=[jax.ShapeDtypeStruct((B, N_KV, n_str, HEAD_DIM), jnp.bfloat16),
                   jax.ShapeDtypeStruct((B, N_KV, C_ROWS, n_str), jnp.bfloat16)],
        compiler_params=pltpu.CompilerParams(
            dimension_semantics=("arbitrary", "arbitrary"), vmem_limit_bytes=VMEM_LIMIT),
        name="compress",
    )(kc_in, vc_in, pe_k.reshape(2, half), pe_v.reshape(2, half),
      wk1.reshape(2, half, CMP_HIDDEN), wk2, wv1.reshape(2, half, CMP_HIDDEN), wv2.T,
      _overlap_t(n_str * S_CMP))


def _overlap_t(S):
    n_slc = S // L_SLC
    c_start = np.arange(N_CMP_PAD) * S_CMP
    s_start = np.arange(n_slc) * L_SLC
    ov = np.clip(np.minimum(c_start[:, None] + L_CMP, s_start[None, :] + L_SLC)
                 - np.maximum(c_start[:, None], s_start[None, :]), 0, None).astype(np.float32) / L_CMP
    ov[(S - L_CMP) // S_CMP + 1:, :] = 0.0
    return jnp.asarray(ov.T, dtype=jnp.bfloat16)


def _select_top(score, k, tri):
    w = score
    cnt = jnp.zeros((1, TQ), jnp.float32)
    thr = jnp.zeros((1, TQ), jnp.float32)
    n_gt = jnp.zeros((1, TQ), jnp.float32)
    for _ in range(N_FREE):
        mx = jnp.max(w, axis=0, keepdims=True)
        eq = w == mx
        c = jnp.sum(jnp.where(eq, 1.0, 0.0), axis=0, keepdims=True)
        cross = (cnt < k) & (cnt + c >= k)
        thr = jnp.where(cross, mx, thr)
        n_gt = jnp.where(cross, cnt, n_gt)
        cnt = cnt + c
        w = jnp.where(eq, -jnp.inf, w)
    at_thr = score == thr
    earlier = jnp.dot(tri, jnp.where(at_thr, 1.0, 0.0).astype(jnp.bfloat16),
                      preferred_element_type=jnp.float32)
    return (score > thr) | (at_thr & (earlier + n_gt < k))


def _nsa_attn_kernel(qt_ref, kc_ref, vc_ref, tabc_ref, tri_ref, ks_ref, kw_ref, vst_ref, vwt_ref,
                     near_ref, edge_ref, gt_ref, o_ref,
                     m_ref, acc_ref, mw_ref, accw_ref, s_ref, tmax_ref, sel_ref, oc_ref):
    i = pl.program_id(1)
    zeros64 = jnp.zeros((HEAD_DIM, TQ), jnp.bfloat16)
    off1 = jnp.where(i >= 1, 0.0, MASK_NEG)
    off2 = jnp.where(i >= 2, 0.0, MASK_NEG)
    j1 = jnp.maximum(i - 1, 0)
    j2 = jnp.maximum(i - 2, 0)
    dot = functools.partial(jnp.dot, preferred_element_type=jnp.float32)

    def q_window(h):
        qh = qt_ref[0, h * HEAD_DIM:(h + 1) * HEAD_DIM, :]
        return jnp.concatenate([qh, zeros64] if h // REP == 0 else [zeros64, qh], axis=0)

    def q_selected(h):
        return jnp.concatenate([q_window(h), sel_ref[h // REP]], axis=0)

    def stage_logits(slot, k_ref, j, q_of, bias_of=None):
        k = k_ref[0, pl.ds(pl.multiple_of(j * TK, TK), TK), :]
        for h in range(N_HEADS):
            s = dot(k, q_of(h))
            if bias_of is not None:
                s = s + bias_of(h)
            s_ref[slot, h] = s
            tmax_ref[slot, h:h + 1, :] = jnp.max(s, axis=0, keepdims=True)

    def accumulate(slot, v_ref, j, state, first=False, shift=None):
        ms_ref, as_ref = state
        for h in range(N_HEADS):
            t_max = tmax_ref[slot, h:h + 1, :]
            if shift is not None:
                t_max = t_max + shift
            m_new = t_max if first else jnp.maximum(ms_ref[h:h + 1, :], t_max)
            m_sub = m_new if shift is None else m_new - shift
            p = jnp.exp2(s_ref[slot, h] - m_sub).astype(jnp.bfloat16)
            pv = dot(v_ref[0, j, h // REP], p)
            if first:
                as_ref[h] = pv
            else:
                as_ref[h] = jnp.exp2(ms_ref[h:h + 1, :] - m_new) * as_ref[h] + pv
            ms_ref[h:h + 1, :] = m_new

    sel_state = (m_ref, acc_ref)
    win_state = (mw_ref, accw_ref)
    near0 = lambda h: near_ref[h, 0]
    near1 = lambda h: near_ref[h, 1]
    edge = lambda h: edge_ref[...]

    off_c = pl.multiple_of(240 - 16 * i, 16)
    for h in range(N_HEADS):
        s = (dot(kc_ref[0, h // REP], qt_ref[0, h * HEAD_DIM:(h + 1) * HEAD_DIM, :])
             + tabc_ref[h, pl.ds(off_c, N_CMP_PAD), :])
        s_ref[0, h] = s
        tmax_ref[0, h:h + 1, :] = jnp.max(s, axis=0, keepdims=True)
    stage_logits(1, kw_ref, i, q_window, near0)

    t_row = i * TQ + lax.broadcasted_iota(jnp.int32, (1, TQ), 1)
    any_cmp = jnp.where(t_row >= L_CMP - 1, 1.0, 0.0)
    n_blk = tri_ref.shape[0]
    j_idx = lax.broadcasted_iota(jnp.int32, (n_blk, TQ), 0)
    cur = (i * TQ + lax.broadcasted_iota(jnp.int32, (n_blk, TQ), 1)) // L_SLC
    forced = (j_idx == 0) | (j_idx == cur) | (j_idx == cur - 1)
    candidate = (j_idx < cur - 1) & (j_idx > 0)
    n_free = float(N_SEL - 1) - jnp.minimum(t_row // L_SLC, 2).astype(jnp.float32)
    for g in range(N_KV):
        imp = jnp.zeros((n_blk, TQ), jnp.float32)
        for h in range(g * REP, (g + 1) * REP):
            p = jnp.exp2(s_ref[0, h] - tmax_ref[0, h:h + 1, :]).astype(jnp.bfloat16)
            a = dot(vc_ref[0, g], p)
            inv = any_cmp / a[HEAD_DIM:HEAD_DIM + 1, :]
            oc_ref[h * HEAD_DIM:(h + 1) * HEAD_DIM, :] = a[0:HEAD_DIM, :] * inv
            imp = imp + a[V_ROWS:V_ROWS + n_blk, :] * inv
        free = _select_top(jnp.where(candidate, imp, MASK_NEG), n_free, tri_ref[...])
        sel = forced | (candidate & free)
        sel_ref[g, 0:n_blk, :] = jnp.where(sel, 0.0, MASK_NEG).astype(jnp.bfloat16)
        sel_ref[g, n_blk:2 * n_blk, :] = jnp.zeros((n_blk, TQ), jnp.bfloat16)

    accumulate(1, vwt_ref, i, win_state, first=True)
    stage_logits(0, ks_ref, i, q_selected, near0)
    stage_logits(1, kw_ref, j1, q_window, near1)
    accumulate(0, vst_ref, i, sel_state, first=True)
    stage_logits(0, ks_ref, j1, q_selected, near1)
    accumulate(1, vwt_ref, j1, win_state, shift=off1)
    stage_logits(1, kw_ref, j2, q_window, edge)
    accumulate(0, vst_ref, j1, sel_state, shift=off1)

    n_far = j1
    last = jnp.maximum(n_far - 1, 0)
    stage_logits(0, ks_ref, 0, q_selected)
    accumulate(1, vwt_ref, j2, win_state, shift=off2)

    def far_pair(t):
        stage_logits(1, ks_ref, t + 1, q_selected)
        accumulate(0, vst_ref, t, sel_state)
        stage_logits(0, ks_ref, jnp.minimum(t + 2, last), q_selected)
        accumulate(1, vst_ref, t + 1, sel_state)

    def far_quad(c, carry):
        far_pair(4 * c)
        far_pair(4 * c + 2)
        return carry

    lax.fori_loop(0, n_far // 4, far_quad, 0)

    @pl.when(n_far % 4 >= 2)
    def _():
        far_pair(n_far // 4 * 4)

    @pl.when(n_far % 2 == 1)
    def _():
        accumulate(0, vst_ref, last, sel_state)

    for pair in range(N_HEADS // 2):
        outs = []
        for h in (2 * pair, 2 * pair + 1):
            rows = slice(h * HEAD_DIM, (h + 1) * HEAD_DIM)
            o_s = acc_ref[h, 0:HEAD_DIM, :] / acc_ref[h, HEAD_DIM:HEAD_DIM + 1, :]
            o_w = accw_ref[h, 0:HEAD_DIM, :] / accw_ref[h, HEAD_DIM:HEAD_DIM + 1, :]
            outs.append(gt_ref[0, 3 * h:3 * h + 1, :] * oc_ref[rows, :]
                        + gt_ref[0, 3 * h + 1:3 * h + 2, :] * o_s
                        + gt_ref[0, 3 * h + 2:3 * h + 3, :] * o_w)
        o_pair = jnp.concatenate(outs, axis=0)
        o_ref[0, :, pair * 128:(pair + 1) * 128] = o_pair.T.astype(jnp.bfloat16)


def _nsa_attn(qt, kc, vc, tab_c, ks, kw, vst, vwt, near, gt):
    B, _, S = qt.shape
    ni = S // TQ
    nt = S // TK
    n_blk = S // L_SLC
    key = np.arange(TK)[:, None]
    row = np.arange(TQ)[None, :]
    edge = jnp.asarray(np.where(2 * TK + row - key < WINDOW, 0.0, MASK_NEG).astype(np.float32))
    tri = jnp.asarray(np.tril(np.ones((n_blk, n_blk), np.float32), -1), dtype=jnp.bfloat16)
    per_batch = lambda shape: pl.BlockSpec((1,) + shape, lambda b, i: (b,) + (0,) * len(shape))
    const = lambda shape: pl.BlockSpec(shape, lambda b, i: (0,) * len(shape),
                                       pipeline_mode=pl.Buffered(1))
    return pl.pallas_call(
        _nsa_attn_kernel,
        grid=(B, ni),
        in_specs=[pl.BlockSpec((1, D_NSA, TQ), lambda b, i: (b, 0, i)),
                  per_batch((N_KV, N_CMP_PAD, HEAD_DIM)),
                  per_batch((N_KV, C_ROWS, N_CMP_PAD)),
                  const((N_HEADS, 512, TQ)),
                  const((n_blk, n_blk)),
                  per_batch((S, 256)),
                  per_batch((S, 128)),
                  per_batch((nt, N_KV, V_ROWS, TK)),
                  per_batch((nt, N_KV, V_ROWS, TK)),
                  const((N_HEADS, 2, TK, TQ)),
                  const((TK, TQ)),
                  pl.BlockSpec((1, 32, TQ), lambda b, i: (b, 0, i))],
        out_specs=pl.BlockSpec((1, TQ, D_NSA), lambda b, i: (b, i, 0)),
        out_shape=jax.ShapeDtypeStruct((B, S, D_NSA), jnp.bfloat16),
        scratch_shapes=[pltpu.VMEM((N_HEADS, TQ), jnp.float32),
                        pltpu.VMEM((N_HEADS, V_ROWS, TQ), jnp.float32),
                        pltpu.VMEM((N_HEADS, TQ), jnp.float32),
                        pltpu.VMEM((N_HEADS, V_ROWS, TQ), jnp.float32),
                        pltpu.VMEM((2, N_HEADS, TK, TQ), jnp.float32),
                        pltpu.VMEM((2, N_HEADS, TQ), jnp.float32),
                        pltpu.VMEM((N_KV, 128, TQ), jnp.bfloat16),
                        pltpu.VMEM((D_NSA, TQ), jnp.float32)],
        compiler_params=pltpu.CompilerParams(
            dimension_semantics=("arbitrary", "arbitrary"), vmem_limit_bytes=VMEM_LIMIT),
        name="nsa_attn",
    )(qt, kc, vc, tab_c, tri, ks, kw, vst, vwt, near, edge, gt)


def _out_ffn_kernel(on_ref, oc_ref, x_ref, wo_ref, gpost_ref, gffn_ref, wup_ref, cw_ref, wdn_ref,
                    g_ref, o_ref, carry_ref, act_ref):
    i = pl.program_id(1)
    tm = x_ref.shape[1]

    @pl.when(i == 0)
    def _():
        carry_ref[...] = jnp.zeros_like(carry_ref)

    y = jnp.dot(on_ref[0], wo_ref[0:D_NSA, :], preferred_element_type=jnp.float32)
    y = y + jnp.dot(oc_ref[0], wo_ref[D_NSA:D_NSA + D_CONV, :], preferred_element_type=jnp.float32)
    x1 = x_ref[0] + _rms(y, gpost_ref[...])
    hb = _rms(x1, gffn_ref[...]).astype(jnp.bfloat16)

    row = lax.broadcasted_iota(jnp.int32, (tm, FF_CHUNK), 0)
    is0 = row == 0
    is1 = row == 1

    def conv(z, c0):
        cols = slice(c0, c0 + FF_CHUNK)
        prev1 = carry_ref[7:8, cols]
        prev2 = carry_ref[6:7, cols]
        z1 = jnp.where(is0, prev1, pltpu.roll(z, 1, 0))
        z2 = jnp.where(is0, prev2, jnp.where(is1, prev1, pltpu.roll(z, 2, 0)))
        carry_ref[:, cols] = z[tm - 8:tm, :]
        return cw_ref[0:1, cols] * z2 + cw_ref[1:2, cols] * z1 + cw_ref[2:3, cols] * z

    for c in range(D_FF // FF_CHUNK):
        g0 = c * FF_CHUNK
        u0 = D_FF + c * FF_CHUNK
        zg = jnp.dot(hb, wup_ref[:, g0:g0 + FF_CHUNK], preferred_element_type=jnp.float32)
        zu = jnp.dot(hb, wup_ref[:, u0:u0 + FF_CHUNK], preferred_element_type=jnp.float32)
        act = _gelu(conv(zg, g0)) * conv(zu, u0)
        act_ref[:, g0:g0 + FF_CHUNK] = act.astype(jnp.bfloat16)
    y2 = jnp.dot(act_ref[...], wdn_ref[...], preferred_element_type=jnp.float32)
    o_ref[0] = x1 + _rms(y2, g_ref[...])


def _out_ffn(o_nsa, o_conv, x, w_out, g_post, g_ffn, w_up, conv_w, w_down, g_ffn_post):
    B, S, D = x.shape
    tm = TM_FFN
    const = lambda b, i: (0, 0)
    single = pl.Buffered(1)
    row_tile = lambda width: pl.BlockSpec((1, tm, width), lambda b, i: (b, i, 0))
    weight = lambda shape: pl.BlockSpec(shape, const, pipeline_mode=single)
    return pl.pallas_call(
        _out_ffn_kernel,
        grid=(B, S // tm),
        in_specs=[row_tile(D_NSA), row_tile(D_CONV), row_tile(D),
                  weight((D, D)), weight((1, D)), weight((1, D)),
                  weight((D, 2 * D_FF)), weight((3, 2 * D_FF)), weight((D_FF, D)), weight((1, D))],
        out_specs=row_tile(D),
        out_shape=jax.ShapeDtypeStruct((B, S, D), jnp.float32),
        scratch_shapes=[pltpu.VMEM((8, 2 * D_FF), jnp.float32),
                        pltpu.VMEM((tm, D_FF), jnp.bfloat16)],
        compiler_params=pltpu.CompilerParams(
            dimension_semantics=("arbitrary", "arbitrary"), vmem_limit_bytes=VMEM_LIMIT),
        name="out_ffn",
    )(o_nsa, o_conv, x, w_out.astype(jnp.bfloat16), g_post.reshape(1, D), g_ffn.reshape(1, D),
      w_up.astype(jnp.bfloat16), conv_w, w_down.astype(jnp.bfloat16), g_ffn_post.reshape(1, D))


def kernel(x, norm_mix_pre, norm_mix_post, norm_ffn_pre, norm_ffn_post, w_in, pe_cmp_k, pe_cmp_v,
           w_cmp_k1, w_cmp_k2, w_cmp_v1, w_cmp_v2, rel_bias, conv_mix_w, w_out, w_ffn_up,
           ffn_conv_w, w_ffn_down):
    B, S, D = x.shape
    assert (S, D) == (4096, D_MODEL) and norm_mix_pre.shape[0] == 1
    tab_c, near = _bias_tables(rel_bias)
    for l in range(norm_mix_pre.shape[0]):
        qt, kc_in, vc_in, ks, kw, vst, vwt, gt, o_conv = _in_proj(
            x, norm_mix_pre[l], w_in[l], conv_mix_w[l])
        kc, vc = _compress(kc_in, vc_in, pe_cmp_k[l], pe_cmp_v[l], w_cmp_k1[l], w_cmp_k2[l],
                           w_cmp_v1[l], w_cmp_v2[l])
        o_nsa = _nsa_attn(qt, kc, vc, tab_c, ks, kw, vst, vwt, near, gt)
        x = _out_ffn(o_nsa, o_conv, x, w_out[l], norm_mix_post[l], norm_ffn_pre[l],
                     w_ffn_up[l], ffn_conv_w[l], w_ffn_down[l], norm_ffn_post[l])
    return x
```

```python
import functools
import math

import numpy as np
import jax
import jax.numpy as jnp
from jax import lax
from jax.experimental import pallas as pl
from jax.experimental.pallas import tpu as pltpu

D_MODEL = 1024
D_NSA = 512
D_CONV = 512
HEAD_DIM = 64
N_HEADS = 8
N_KV = 2
REP = 4
KV_W = 128
N_GATES = 24
L_CMP = 32
S_CMP = 16
L_SLC = 64
N_SEL = 16
WINDOW = 512
CMP_HIDDEN = 256
D_FF = 2816
N_BUCKETS = 32
MAX_DIST = 128
RMS_EPS = 1e-6
MASK_NEG = -1e30
N_FREE = N_SEL - 3
LOG2E = math.log2(math.e)

LANES = 128
SUBLANES = 8
BF16_ROWS = 2 * SUBLANES
GATE_ROWS = -(-N_GATES // BF16_ROWS) * BF16_ROWS
TQ = 256
TK = 256
V_ROWS = 80
C_ROWS = V_ROWS + 64
N_CMP_PAD = 256
TAB_C_ROWS = 2 * N_CMP_PAD
TAB_C_ORIGIN = N_CMP_PAD - TQ // S_CMP
TM_IN = 1024
TM_FFN = 1024
FF_CHUNK = 256
VMEM_LIMIT = 56 * 1024 * 1024

_NT = (((1,), (1,)), ((), ()))


def _bucket_np(d):
    max_exact = N_BUCKETS // 2
    d = np.maximum(d, 0)
    df = np.maximum(d, 1).astype(np.float32)
    large = max_exact + (np.log(df / max_exact) / math.log(MAX_DIST / max_exact)
                         * (N_BUCKETS - max_exact)).astype(np.int32)
    return np.where(d < max_exact, d, np.minimum(large, N_BUCKETS - 1)).astype(np.int32)


def _bucket_or_mask(d):
    return np.where(d >= 0, _bucket_np(d), -1).astype(np.int32)


def _rms(x, g):
    return x * lax.rsqrt(jnp.mean(x * x, axis=-1, keepdims=True) + RMS_EPS) * g


def _gelu(x):
    return jax.nn.gelu(x, approximate=True)


def _band_box(idx):
    rows, cols = np.nonzero((idx >= 0) & (idx < N_BUCKETS - 1))
    r0, r1, c0, c1 = int(rows.min()), int(rows.max()) + 1, int(cols.min()), int(cols.max()) + 1
    return (r0 // SUBLANES * SUBLANES, -(-r1 // SUBLANES) * SUBLANES,
            c0 // LANES * LANES, -(-c1 // LANES) * LANES)


def _bias_tables_kernel(boxes, rb_ref, idx_c_ref, idx_n_ref, tab_c_ref, tab_n_ref):
    h = pl.program_id(0)
    last = rb_ref[h, N_BUCKETS - 1]

    def fill(idx, out, box):
        out[...] = jnp.where(idx < 0, MASK_NEG, 0.0)
        r0, r1, c0, c1 = box
        sub = idx[r0:r1, c0:c1]
        val = jnp.zeros(sub.shape, jnp.float32)
        for b in range(N_BUCKETS - 1):
            val = jnp.where(sub == b, rb_ref[h, b] - last, val)
        out[r0:r1, c0:c1] = jnp.where(sub < 0, MASK_NEG, val * LOG2E)

    fill(idx_c_ref[...], tab_c_ref.at[0], boxes[0])
    for k in range(idx_n_ref.shape[0]):
        fill(idx_n_ref[k], tab_n_ref.at[0, k], boxes[1 + k])


def _bias_tables(rel_bias):
    u = np.arange(TAB_C_ROWS)[:, None] - TAB_C_ORIGIN
    row = np.arange(TQ)[None, :]
    d_c = row - S_CMP * u - (L_CMP - 1)
    idx_c = _bucket_or_mask(d_c)
    key = np.arange(TK)[:, None]
    idx_n = np.stack([_bucket_or_mask(dd + row - key) for dd in (0, TK)])
    boxes = tuple(_band_box(t) for t in (idx_c, *idx_n))
    return pl.pallas_call(
        functools.partial(_bias_tables_kernel, boxes),
        grid=(N_HEADS,),
        in_specs=[pl.BlockSpec(memory_space=pltpu.SMEM),
                  pl.BlockSpec((TAB_C_ROWS, TQ), lambda h: (0, 0)),
                  pl.BlockSpec((2, TK, TQ), lambda h: (0, 0, 0))],
        out_specs=[pl.BlockSpec((1, TAB_C_ROWS, TQ), lambda h: (h, 0, 0)),
                   pl.BlockSpec((1, 2, TK, TQ), lambda h: (h, 0, 0, 0))],
        out_shape=[jax.ShapeDtypeStruct((N_HEADS, TAB_C_ROWS, TQ), jnp.float32),
                   jax.ShapeDtypeStruct((N_HEADS, 2, TK, TQ), jnp.float32)],
        name="bias_tables",
    )(rel_bias, jnp.asarray(idx_c), jnp.asarray(idx_n))


def _in_proj_kernel(x_ref, g_ref, wq_ref, wnat_ref, wvt_ref, wgt_ref, cw_ref,
                    qt_ref, kc_ref, vc_ref, ks_ref, kw_ref, vst_ref, vwt_ref, gt_ref,
                    oconv_ref, carry_ref, stage_ref):
    i = pl.program_id(1)
    tm = x_ref.shape[1]

    @pl.when(i == 0)
    def _():
        carry_ref[...] = jnp.zeros_like(carry_ref)

    hb = _rms(x_ref[0], g_ref[...]).astype(jnp.bfloat16)

    qt_ref[0] = lax.dot_general(wq_ref[...], hb, _NT,
                                preferred_element_type=jnp.float32).astype(jnp.bfloat16)
    vt = lax.dot_general(wvt_ref[...], hb, _NT, preferred_element_type=jnp.float32)
    ones = jnp.ones((V_ROWS - HEAD_DIM, TK), jnp.bfloat16)
    for t in range(tm // TK):
        for g in range(N_KV):
            for out_ref, r0 in ((vst_ref, g * HEAD_DIM), (vwt_ref, KV_W + g * HEAD_DIM)):
                out_ref[0, t, g, 0:HEAD_DIM, :] = vt[r0:r0 + HEAD_DIM,
                                                      t * TK:(t + 1) * TK].astype(jnp.bfloat16)
                out_ref[0, t, g, HEAD_DIM:V_ROWS, :] = ones
    gl = lax.dot_general(wgt_ref[...], hb, _NT, preferred_element_type=jnp.float32)
    gt_ref[0] = jax.nn.sigmoid(gl)

    kv_c = jnp.dot(hb, wnat_ref[:, 0:2 * KV_W], preferred_element_type=jnp.float32)
    for a, out_ref in enumerate((kc_ref, vc_ref)):
        stage_ref[a] = kv_c[:, a * KV_W:(a + 1) * KV_W]
        for l in range(S_CMP):
            rows = stage_ref[a, pl.ds(l, tm // S_CMP, stride=S_CMP), :]
            for g in range(N_KV):
                out_ref[0, g, :, l * HEAD_DIM:(l + 1) * HEAD_DIM] = rows[
                    :, g * HEAD_DIM:(g + 1) * HEAD_DIM]
    k_sw = jnp.dot(hb, wnat_ref[:, 2 * KV_W:4 * KV_W],
                   preferred_element_type=jnp.float32)
    blk = (i * tm + lax.broadcasted_iota(jnp.int32, (tm, KV_W), 0)) // L_SLC
    col = lax.broadcasted_iota(jnp.int32, (tm, KV_W), 1)
    onehot = jnp.where(blk == col, 1.0, 0.0)
    ks_ref[0, :, 0:KV_W] = k_sw[:, 0:KV_W].astype(jnp.bfloat16)
    ks_ref[0, :, KV_W:2 * KV_W] = onehot.astype(jnp.bfloat16)
    kw_ref[0] = k_sw[:, KV_W:2 * KV_W].astype(jnp.bfloat16)

    c0 = 4 * KV_W
    bg = jnp.dot(hb, wnat_ref[:, c0:c0 + D_CONV], preferred_element_type=jnp.float32)
    cg = jnp.dot(hb, wnat_ref[:, c0 + D_CONV:c0 + 2 * D_CONV], preferred_element_type=jnp.float32)
    xt = jnp.dot(hb, wnat_ref[:, c0 + 2 * D_CONV:c0 + 3 * D_CONV],
                 preferred_element_type=jnp.float32)
    z = cg * xt
    prev1 = carry_ref[SUBLANES - 1:SUBLANES, :]
    prev2 = carry_ref[SUBLANES - 2:SUBLANES - 1, :]
    row = lax.broadcasted_iota(jnp.int32, z.shape, 0)
    z1 = jnp.where(row == 0, prev1, pltpu.roll(z, 1, 0))
    z2 = jnp.where(row == 0, prev2, jnp.where(row == 1, prev1, pltpu.roll(z, 2, 0)))
    y = cw_ref[0:1, :] * z2 + cw_ref[1:2, :] * z1 + cw_ref[2:3, :] * z
    oconv_ref[0] = (bg * y).astype(jnp.bfloat16)
    carry_ref[...] = z[tm - SUBLANES:tm, :]


def _in_proj(x, g_pre, w_in, conv_w):
    B, S, D = x.shape
    tm = TM_IN
    widths = dict(q=D_NSA, k_c=KV_W, v_c=KV_W, k_s=KV_W, v_s=KV_W, k_w=KV_W, v_w=KV_W,
                  gate=N_GATES, b=D_CONV, c=D_CONV, x=D_CONV)
    assert w_in.shape[1] == sum(widths.values())
    col, cols = 0, {}
    for name, width in widths.items():
        cols[name] = w_in[:, col:col + width]
        col += width
    wq = (cols['q'].T * (HEAD_DIM ** -0.5 * LOG2E)).astype(jnp.bfloat16)
    wnat = jnp.concatenate([cols[n] for n in ('k_c', 'v_c', 'k_s', 'k_w', 'b', 'c', 'x')],
                           axis=1).astype(jnp.bfloat16)
    wvt = jnp.concatenate([cols['v_s'], cols['v_w']], axis=1).T.astype(jnp.bfloat16)
    wgt = jnp.pad(cols['gate'], ((0, 0), (0, GATE_ROWS - N_GATES))).T.astype(jnp.bfloat16)
    nat_w = 4 * KV_W + 3 * D_CONV
    nt = S // TK
    const = lambda b, i: (0, 0)
    outs = pl.pallas_call(
        _in_proj_kernel,
        grid=(B, S // tm),
        in_specs=[pl.BlockSpec((1, tm, D), lambda b, i: (b, i, 0)),
                  pl.BlockSpec((1, D), const),
                  pl.BlockSpec((D_NSA, D), const),
                  pl.BlockSpec((D, nat_w), const),
                  pl.BlockSpec((2 * KV_W, D), const),
                  pl.BlockSpec((GATE_ROWS, D), const),
                  pl.BlockSpec((3, D_CONV), const)],
        out_specs=[pl.BlockSpec((1, D_NSA, tm), lambda b, i: (b, 0, i)),
                   pl.BlockSpec((1, N_KV, tm // S_CMP, S_CMP * HEAD_DIM), lambda b, i: (b, 0, i, 0)),
                   pl.BlockSpec((1, N_KV, tm // S_CMP, S_CMP * HEAD_DIM), lambda b, i: (b, 0, i, 0)),
                   pl.BlockSpec((1, tm, 2 * KV_W), lambda b, i: (b, i, 0)),
                   pl.BlockSpec((1, tm, KV_W), lambda b, i: (b, i, 0)),
                   pl.BlockSpec((1, tm // TK, N_KV, V_ROWS, TK), lambda b, i: (b, i, 0, 0, 0)),
                   pl.BlockSpec((1, tm // TK, N_KV, V_ROWS, TK), lambda b, i: (b, i, 0, 0, 0)),
                   pl.BlockSpec((1, GATE_ROWS, tm), lambda b, i: (b, 0, i)),
                   pl.BlockSpec((1, tm, D_CONV), lambda b, i: (b, i, 0))],
        out_shape=[jax.ShapeDtypeStruct((B, D_NSA, S), jnp.bfloat16),
                   jax.ShapeDtypeStruct((B, N_KV, S // S_CMP, S_CMP * HEAD_DIM), jnp.float32),
                   jax.ShapeDtypeStruct((B, N_KV, S // S_CMP, S_CMP * HEAD_DIM), jnp.float32),
                   jax.ShapeDtypeStruct((B, S, 2 * KV_W), jnp.bfloat16),
                   jax.ShapeDtypeStruct((B, S, KV_W), jnp.bfloat16),
                   jax.ShapeDtypeStruct((B, nt, N_KV, V_ROWS, TK), jnp.bfloat16),
                   jax.ShapeDtypeStruct((B, nt, N_KV, V_ROWS, TK), jnp.bfloat16),
                   jax.ShapeDtypeStruct((B, GATE_ROWS, S), jnp.float32),
                   jax.ShapeDtypeStruct((B, S, D_CONV), jnp.bfloat16)],
        scratch_shapes=[pltpu.VMEM((SUBLANES, D_CONV), jnp.float32),
                        pltpu.VMEM((2, tm, KV_W), jnp.float32)],
        compiler_params=pltpu.CompilerParams(
            dimension_semantics=("arbitrary", "arbitrary"), vmem_limit_bytes=VMEM_LIMIT),
        name="in_proj",
    )(x, g_pre.reshape(1, D), wq, wnat, wvt, wgt, conv_w)
    return outs


def _compress_kernel(ck_ref, cv_ref, pek_ref, pev_ref, wk1_ref, wk2_ref, wv1_ref, wv2t_ref,
                     ovl_ref, kc_ref, vct_ref):
    def hidden(c_ref, pe_ref, w1_ref):
        c = c_ref[0, 0]
        a = jnp.dot((c + pe_ref[0:1, :]).astype(jnp.bfloat16), w1_ref[0].astype(jnp.bfloat16),
                    preferred_element_type=jnp.float32)
        b = jnp.dot((c + pe_ref[1:2, :]).astype(jnp.bfloat16), w1_ref[1].astype(jnp.bfloat16),
                    preferred_element_type=jnp.float32)
        return _gelu(a + pltpu.roll(b, N_CMP_PAD - 1, 0)).astype(jnp.bfloat16)

    kc_ref[0, 0] = jnp.dot(hidden(ck_ref, pek_ref, wk1_ref), wk2_ref[...].astype(jnp.bfloat16),
                           preferred_element_type=jnp.float32).astype(jnp.bfloat16)
    vct_ref[0, 0, 0:HEAD_DIM, :] = lax.dot_general(
        wv2t_ref[...].astype(jnp.bfloat16), hidden(cv_ref, pev_ref, wv1_ref), _NT,
        preferred_element_type=jnp.float32).astype(jnp.bfloat16)
    vct_ref[0, 0, HEAD_DIM:V_ROWS, :] = jnp.ones((V_ROWS - HEAD_DIM, N_CMP_PAD), jnp.bfloat16)
    vct_ref[0, 0, V_ROWS:C_ROWS, :] = ovl_ref[...]


def _compress(kc_in, vc_in, pe_k, pe_v, wk1, wk2, wv1, wv2):
    B, _, n_str, half = kc_in.shape
    const2 = lambda b, g: (0, 0)
    const3 = lambda b, g: (0, 0, 0)
    return pl.pallas_call(
        _compress_kernel,
        grid=(B, N_KV),
        in_specs=[pl.BlockSpec((1, 1, n_str, half), lambda b, g: (b, g, 0, 0)),
                  pl.BlockSpec((1, 1, n_str, half), lambda b, g: (b, g, 0, 0)),
                  pl.BlockSpec((2, half), const2),
                  pl.BlockSpec((2, half), const2),
                  pl.BlockSpec((2, half, CMP_HIDDEN), const3),
                  pl.BlockSpec((CMP_HIDDEN, HEAD_DIM), const2),
                  pl.BlockSpec((2, half, CMP_HIDDEN), const3),
                  pl.BlockSpec((HEAD_DIM, CMP_HIDDEN), const2),
                  pl.BlockSpec((C_ROWS - V_ROWS, n_str), const2)],
        out_specs=[pl.BlockSpec((1, 1, n_str, HEAD_DIM), lambda b, g: (b, g, 0, 0)),
                   pl.BlockSpec((1, 1, C_ROWS, n_str), lambda b, g: (b, g, 0, 0))],
        out_shape=[jax.ShapeDtypeStruct((B, N_KV, n_str, HEAD_DIM), jnp.bfloat16),
                   jax.ShapeDtypeStruct((B, N_KV, C_ROWS, n_str), jnp.bfloat16)],
        compiler_params=pltpu.CompilerParams(
            dimension_semantics=("arbitrary", "arbitrary"), vmem_limit_bytes=VMEM_LIMIT),
        name="compress",
    )(kc_in, vc_in, pe_k.reshape(2, half), pe_v.reshape(2, half),
      wk1.reshape(2, half, CMP_HIDDEN), wk2, wv1.reshape(2, half, CMP_HIDDEN), wv2.T,
      _overlap_t(n_str * S_CMP))


def _overlap_t(S):
    n_slc = S // L_SLC
    c_start = np.arange(N_CMP_PAD) * S_CMP
    s_start = np.arange(n_slc) * L_SLC
    ov = np.clip(np.minimum(c_start[:, None] + L_CMP, s_start[None, :] + L_SLC)
                 - np.maximum(c_start[:, None], s_start[None, :]), 0, None).astype(np.float32) / L_CMP
    ov[(S - L_CMP) // S_CMP + 1:, :] = 0.0
    return jnp.asarray(ov.T, dtype=jnp.bfloat16)


def _select_top(score, k, tri):
    w = score
    cnt = jnp.zeros((1, TQ), jnp.float32)
    thr = jnp.zeros((1, TQ), jnp.float32)
    n_gt = jnp.zeros((1, TQ), jnp.float32)
    for _ in range(N_FREE):
        mx = jnp.max(w, axis=0, keepdims=True)
        eq = w == mx
        c = jnp.sum(jnp.where(eq, 1.0, 0.0), axis=0, keepdims=True)
        cross = (cnt < k) & (cnt + c >= k)
        thr = jnp.where(cross, mx, thr)
        n_gt = jnp.where(cross, cnt, n_gt)
        cnt = cnt + c
        w = jnp.where(eq, -jnp.inf, w)
    at_thr = score == thr
    earlier = jnp.dot(tri, jnp.where(at_thr, 1.0, 0.0).astype(jnp.bfloat16),
                      preferred_element_type=jnp.float32)
    return (score > thr) | (at_thr & (earlier + n_gt < k))


def _nsa_attn_kernel(qt_ref, kc_ref, vc_ref, tabc_ref, tri_ref, ks_ref, kw_ref, vst_ref, vwt_ref,
                     near_ref, edge_ref, gt_ref, o_ref,
                     m_ref, acc_ref, mw_ref, accw_ref, s_ref, tmax_ref, sel_ref, oc_ref):
    i = pl.program_id(1)
    zeros64 = jnp.zeros((HEAD_DIM, TQ), jnp.bfloat16)
    off1 = jnp.where(i >= 1, 0.0, MASK_NEG)
    off2 = jnp.where(i >= 2, 0.0, MASK_NEG)
    j1 = jnp.maximum(i - 1, 0)
    j2 = jnp.maximum(i - 2, 0)
    dot = functools.partial(jnp.dot, preferred_element_type=jnp.float32)

    def q_window(h):
        qh = qt_ref[0, h * HEAD_DIM:(h + 1) * HEAD_DIM, :]
        return jnp.concatenate([qh, zeros64] if h // REP == 0 else [zeros64, qh], axis=0)

    def q_selected(h):
        return jnp.concatenate([q_window(h), sel_ref[h // REP]], axis=0)

    def stage_logits(slot, k_ref, j, q_of, bias_of=None):
        k = k_ref[0, pl.ds(pl.multiple_of(j * TK, TK), TK), :]
        for h in range(N_HEADS):
            s = dot(k, q_of(h))
            if bias_of is not None:
                s = s + bias_of(h)
            s_ref[slot, h] = s
            tmax_ref[slot, h:h + 1, :] = jnp.max(s, axis=0, keepdims=True)

    def accumulate(slot, v_ref, j, state, first=False, shift=None):
        ms_ref, as_ref = state
        for h in range(N_HEADS):
            t_max = tmax_ref[slot, h:h + 1, :]
            if shift is not None:
                t_max = t_max + shift
            m_new = t_max if first else jnp.maximum(ms_ref[h:h + 1, :], t_max)
            m_sub = m_new if shift is None else m_new - shift
            p = jnp.exp2(s_ref[slot, h] - m_sub).astype(jnp.bfloat16)
            pv = dot(v_ref[0, j, h // REP], p)
            if first:
                as_ref[h] = pv
            else:
                as_ref[h] = jnp.exp2(ms_ref[h:h + 1, :] - m_new) * as_ref[h] + pv
            ms_ref[h:h + 1, :] = m_new

    sel_state = (m_ref, acc_ref)
    win_state = (mw_ref, accw_ref)
    near0 = lambda h: near_ref[h, 0]
    near1 = lambda h: near_ref[h, 1]
    edge = lambda h: edge_ref[...]

    off_c = pl.multiple_of(TAB_C_ORIGIN - (TQ // S_CMP) * i, TQ // S_CMP)
    for h in range(N_HEADS):
        s = (dot(kc_ref[0, h // REP], qt_ref[0, h * HEAD_DIM:(h + 1) * HEAD_DIM, :])
             + tabc_ref[h, pl.ds(off_c, N_CMP_PAD), :])
        s_ref[0, h] = s
        tmax_ref[0, h:h + 1, :] = jnp.max(s, axis=0, keepdims=True)
    stage_logits(1, kw_ref, i, q_window, near0)

    t_row = i * TQ + lax.broadcasted_iota(jnp.int32, (1, TQ), 1)
    any_cmp = jnp.where(t_row >= L_CMP - 1, 1.0, 0.0)
    n_blk = tri_ref.shape[0]
    j_idx = lax.broadcasted_iota(jnp.int32, (n_blk, TQ), 0)
    cur = (i * TQ + lax.broadcasted_iota(jnp.int32, (n_blk, TQ), 1)) // L_SLC
    forced = (j_idx == 0) | (j_idx == cur) | (j_idx == cur - 1)
    candidate = (j_idx < cur - 1) & (j_idx > 0)
    n_free = float(N_SEL - 1) - jnp.minimum(t_row // L_SLC, 2).astype(jnp.float32)
    for g in range(N_KV):
        imp = jnp.zeros((n_blk, TQ), jnp.float32)
        for h in range(g * REP, (g + 1) * REP):
            p = jnp.exp2(s_ref[0, h] - tmax_ref[0, h:h + 1, :]).astype(jnp.bfloat16)
            a = dot(vc_ref[0, g], p)
            inv = any_cmp / a[HEAD_DIM:HEAD_DIM + 1, :]
            oc_ref[h * HEAD_DIM:(h + 1) * HEAD_DIM, :] = a[0:HEAD_DIM, :] * inv
            imp = imp + a[V_ROWS:V_ROWS + n_blk, :] * inv
        free = _select_top(jnp.where(candidate, imp, MASK_NEG), n_free, tri_ref[...])
        sel = forced | (candidate & free)
        sel_ref[g, 0:n_blk, :] = jnp.where(sel, 0.0, MASK_NEG).astype(jnp.bfloat16)
        sel_ref[g, n_blk:2 * n_blk, :] = jnp.zeros((n_blk, TQ), jnp.bfloat16)

    accumulate(1, vwt_ref, i, win_state, first=True)
    stage_logits(0, ks_ref, i, q_selected, near0)
    stage_logits(1, kw_ref, j1, q_window, near1)
    accumulate(0, vst_ref, i, sel_state, first=True)
    stage_logits(0, ks_ref, j1, q_selected, near1)
    accumulate(1, vwt_ref, j1, win_state, shift=off1)
    stage_logits(1, kw_ref, j2, q_window, edge)
    accumulate(0, vst_ref, j1, sel_state, shift=off1)

    n_far = j1
    last = jnp.maximum(n_far - 1, 0)
    stage_logits(0, ks_ref, 0, q_selected)
    accumulate(1, vwt_ref, j2, win_state, shift=off2)

    def far_pair(t):
        stage_logits(1, ks_ref, t + 1, q_selected)
        accumulate(0, vst_ref, t, sel_state)
        stage_logits(0, ks_ref, jnp.minimum(t + 2, last), q_selected)
        accumulate(1, vst_ref, t + 1, sel_state)

    def far_quad(c, carry):
        far_pair(4 * c)
        far_pair(4 * c + 2)
        return carry

    lax.fori_loop(0, n_far // 4, far_quad, 0)

    @pl.when(n_far % 4 >= 2)
    def _():
        far_pair(n_far // 4 * 4)

    @pl.when(n_far % 2 == 1)
    def _():
        accumulate(0, vst_ref, last, sel_state)

    for pair in range(N_HEADS // 2):
        outs = []
        for h in (2 * pair, 2 * pair + 1):
            rows = slice(h * HEAD_DIM, (h + 1) * HEAD_DIM)
            o_s = acc_ref[h, 0:HEAD_DIM, :] / acc_ref[h, HEAD_DIM:HEAD_DIM + 1, :]
            o_w = accw_ref[h, 0:HEAD_DIM, :] / accw_ref[h, HEAD_DIM:HEAD_DIM + 1, :]
            outs.append(gt_ref[0, 3 * h:3 * h + 1, :] * oc_ref[rows, :]
                        + gt_ref[0, 3 * h + 1:3 * h + 2, :] * o_s
                        + gt_ref[0, 3 * h + 2:3 * h + 3, :] * o_w)
        o_pair = jnp.concatenate(outs, axis=0)
        o_ref[0, :, pair * 2 * HEAD_DIM:(pair + 1) * 2 * HEAD_DIM] = o_pair.T.astype(jnp.bfloat16)


def _nsa_attn(qt, kc, vc, tab_c, ks, kw, vst, vwt, near, gt):
    B, _, S = qt.shape
    ni = S // TQ
    nt = S // TK
    n_blk = S // L_SLC
    key = np.arange(TK)[:, None]
    row = np.arange(TQ)[None, :]
    edge = jnp.asarray(np.where(2 * TK + row - key < WINDOW, 0.0, MASK_NEG).astype(np.float32))
    tri = jnp.asarray(np.tril(np.ones((n_blk, n_blk), np.float32), -1), dtype=jnp.bfloat16)
    per_batch = lambda shape: pl.BlockSpec((1,) + shape, lambda b, i: (b,) + (0,) * len(shape))
    const = lambda shape: pl.BlockSpec(shape, lambda b, i: (0,) * len(shape),
                                       pipeline_mode=pl.Buffered(1))
    return pl.pallas_call(
        _nsa_attn_kernel,
        grid=(B, ni),
        in_specs=[pl.BlockSpec((1, D_NSA, TQ), lambda b, i: (b, 0, i)),
                  per_batch((N_KV, N_CMP_PAD, HEAD_DIM)),
                  per_batch((N_KV, C_ROWS, N_CMP_PAD)),
                  const((N_HEADS, TAB_C_ROWS, TQ)),
                  const((n_blk, n_blk)),
                  per_batch((S, 2 * KV_W)),
                  per_batch((S, KV_W)),
                  per_batch((nt, N_KV, V_ROWS, TK)),
                  per_batch((nt, N_KV, V_ROWS, TK)),
                  const((N_HEADS, 2, TK, TQ)),
                  const((TK, TQ)),
                  pl.BlockSpec((1, GATE_ROWS, TQ), lambda b, i: (b, 0, i))],
        out_specs=pl.BlockSpec((1, TQ, D_NSA), lambda b, i: (b, i, 0)),
        out_shape=jax.ShapeDtypeStruct((B, S, D_NSA), jnp.bfloat16),
        scratch_shapes=[pltpu.VMEM((N_HEADS, TQ), jnp.float32),
                        pltpu.VMEM((N_HEADS, V_ROWS, TQ), jnp.float32),
                        pltpu.VMEM((N_HEADS, TQ), jnp.float32),
                        pltpu.VMEM((N_HEADS, V_ROWS, TQ), jnp.float32),
                        pltpu.VMEM((2, N_HEADS, TK, TQ), jnp.float32),
                        pltpu.VMEM((2, N_HEADS, TQ), jnp.float32),
                        pltpu.VMEM((N_KV, 2 * n_blk, TQ), jnp.bfloat16),
                        pltpu.VMEM((D_NSA, TQ), jnp.float32)],
        compiler_params=pltpu.CompilerParams(
            dimension_semantics=("arbitrary", "arbitrary"), vmem_limit_bytes=VMEM_LIMIT),
        name="nsa_attn",
    )(qt, kc, vc, tab_c, tri, ks, kw, vst, vwt, near, edge, gt)


def _out_ffn_kernel(on_ref, oc_ref, x_ref, wo_ref, gpost_ref, gffn_ref, wup_ref, cw_ref, wdn_ref,
                    g_ref, o_ref, carry_ref, act_ref):
    i = pl.program_id(1)
    tm = x_ref.shape[1]

    @pl.when(i == 0)
    def _():
        carry_ref[...] = jnp.zeros_like(carry_ref)

    y = jnp.dot(on_ref[0], wo_ref[0:D_NSA, :], preferred_element_type=jnp.float32)
    y = y + jnp.dot(oc_ref[0], wo_ref[D_NSA:D_NSA + D_CONV, :], preferred_element_type=jnp.float32)
    x1 = x_ref[0] + _rms(y, gpost_ref[...])
    hb = _rms(x1, gffn_ref[...]).astype(jnp.bfloat16)

    row = lax.broadcasted_iota(jnp.int32, (tm, FF_CHUNK), 0)
    is0 = row == 0
    is1 = row == 1

    def conv(z, c0):
        cols = slice(c0, c0 + FF_CHUNK)
        prev1 = carry_ref[SUBLANES - 1:SUBLANES, cols]
        prev2 = carry_ref[SUBLANES - 2:SUBLANES - 1, cols]
        z1 = jnp.where(is0, prev1, pltpu.roll(z, 1, 0))
        z2 = jnp.where(is0, prev2, jnp.where(is1, prev1, pltpu.roll(z, 2, 0)))
        carry_ref[:, cols] = z[tm - SUBLANES:tm, :]
        return cw_ref[0:1, cols] * z2 + cw_ref[1:2, cols] * z1 + cw_ref[2:3, cols] * z

    for c in range(D_FF // FF_CHUNK):
        g0 = c * FF_CHUNK
        u0 = D_FF + c * FF_CHUNK
        zg = jnp.dot(hb, wup_ref[:, g0:g0 + FF_CHUNK], preferred_element_type=jnp.float32)
        zu = jnp.dot(hb, wup_ref[:, u0:u0 + FF_CHUNK], preferred_element_type=jnp.float32)
        act = _gelu(conv(zg, g0)) * conv(zu, u0)
        act_ref[:, g0:g0 + FF_CHUNK] = act.astype(jnp.bfloat16)
    y2 = jnp.dot(act_ref[...], wdn_ref[...], preferred_element_type=jnp.float32)
    o_ref[0] = x1 + _rms(y2, g_ref[...])


def _out_ffn(o_nsa, o_conv, x, w_out, g_post, g_ffn, w_up, conv_w, w_down, g_ffn_post):
    B, S, D = x.shape
    tm = TM_FFN
    const = lambda b, i: (0, 0)
    single = pl.Buffered(1)
    row_tile = lambda width: pl.BlockSpec((1, tm, width), lambda b, i: (b, i, 0))
    weight = lambda shape: pl.BlockSpec(shape, const, pipeline_mode=single)
    return pl.pallas_call(
        _out_ffn_kernel,
        grid=(B, S // tm),
        in_specs=[row_tile(D_NSA), row_tile(D_CONV), row_tile(D),
                  weight((D, D)), weight((1, D)), weight((1, D)),
                  weight((D, 2 * D_FF)), weight((3, 2 * D_FF)), weight((D_FF, D)), weight((1, D))],
        out_specs=row_tile(D),
        out_shape=jax.ShapeDtypeStruct((B, S, D), jnp.float32),
        scratch_shapes=[pltpu.VMEM((SUBLANES, 2 * D_FF), jnp.float32),
                        pltpu.VMEM((tm, D_FF), jnp.bfloat16)],
        compiler_params=pltpu.CompilerParams(
            dimension_semantics=("arbitrary", "arbitrary"), vmem_limit_bytes=VMEM_LIMIT),
        name="out_ffn",
    )(o_nsa, o_conv, x, w_out.astype(jnp.bfloat16), g_post.reshape(1, D), g_ffn.reshape(1, D),
      w_up.astype(jnp.bfloat16), conv_w, w_down.astype(jnp.bfloat16), g_ffn_post.reshape(1, D))


def kernel(x, norm_mix_pre, norm_mix_post, norm_ffn_pre, norm_ffn_post, w_in, pe_cmp_k, pe_cmp_v,
           w_cmp_k1, w_cmp_k2, w_cmp_v1, w_cmp_v2, rel_bias, conv_mix_w, w_out, w_ffn_up,
           ffn_conv_w, w_ffn_down):
    B, S, D = x.shape
    assert (S, D) == (4096, D_MODEL) and norm_mix_pre.shape[0] == 1
    tab_c, near = _bias_tables(rel_bias)
    for l in range(norm_mix_pre.shape[0]):
        qt, kc_in, vc_in, ks, kw, vst, vwt, gt, o_conv = _in_proj(
            x, norm_mix_pre[l], w_in[l], conv_mix_w[l])
        kc, vc = _compress(kc_in, vc_in, pe_cmp_k[l], pe_cmp_v[l], w_cmp_k1[l], w_cmp_k2[l],
                           w_cmp_v1[l], w_cmp_v2[l])
        o_nsa = _nsa_attn(qt, kc, vc, tab_c, ks, kw, vst, vwt, near, gt)
        x = _out_ffn(o_nsa, o_conv, x, w_out[l], norm_mix_post[l], norm_ffn_pre[l],
                     w_ffn_up[l], ffn_conv_w[l], w_ffn_down[l], norm_ffn_post[l])
    return x
```

```python
import functools
import math

import numpy as np
import jax
import jax.numpy as jnp
from jax import lax
from jax.experimental import pallas as pl
from jax.experimental.pallas import tpu as pltpu

D_MODEL = 1024
D_NSA = 512
D_CONV = 512
HEAD_DIM = 64
N_HEADS = 8
N_KV = 2
REP = 4
KV_W = 128
N_GATES = 24
L_CMP = 32
S_CMP = 16
L_SLC = 64
N_SEL = 16
WINDOW = 512
CMP_HIDDEN = 256
D_FF = 2816
N_BUCKETS = 32
MAX_DIST = 128
RMS_EPS = 1e-6
MASK_NEG = -1e30
N_FREE = N_SEL - 3
LOG2E = math.log2(math.e)

LANES = 128
SUBLANES = 8
BF16_ROWS = 2 * SUBLANES
GATE_ROWS = -(-N_GATES // BF16_ROWS) * BF16_ROWS
TQ = 256
TK = 256
V_ROWS = 80
C_ROWS = V_ROWS + 64
N_CMP_PAD = 256
TAB_C_ROWS = 2 * N_CMP_PAD
TAB_C_ORIGIN = N_CMP_PAD - TQ // S_CMP
TM_IN = 1024
TM_FFN = 1024
FF_CHUNK = 256
VMEM_LIMIT = 56 * 1024 * 1024

_NT = (((1,), (1,)), ((), ()))


def _bucket_np(d):
    max_exact = N_BUCKETS // 2
    d = np.maximum(d, 0)
    df = np.maximum(d, 1).astype(np.float32)
    large = max_exact + (np.log(df / max_exact) / math.log(MAX_DIST / max_exact)
                         * (N_BUCKETS - max_exact)).astype(np.int32)
    return np.where(d < max_exact, d, np.minimum(large, N_BUCKETS - 1)).astype(np.int32)


def _bucket_or_mask(d):
    return np.where(d >= 0, _bucket_np(d), -1).astype(np.int32)


def _rms(x, g):
    return x * lax.rsqrt(jnp.mean(x * x, axis=-1, keepdims=True) + RMS_EPS) * g


def _gelu(x):
    return jax.nn.gelu(x, approximate=True)


def _band_box(idx):
    rows, cols = np.nonzero((idx >= 0) & (idx < N_BUCKETS - 1))
    r0, r1, c0, c1 = int(rows.min()), int(rows.max()) + 1, int(cols.min()), int(cols.max()) + 1
    return (r0 // SUBLANES * SUBLANES, -(-r1 // SUBLANES) * SUBLANES,
            c0 // LANES * LANES, -(-c1 // LANES) * LANES)


def _bias_tables_kernel(boxes, rb_ref, idx_c_ref, idx_n_ref, tab_c_ref, tab_n_ref):
    h = pl.program_id(0)
    last = rb_ref[h, N_BUCKETS - 1]

    def fill(idx, out, box):
        out[...] = jnp.where(idx < 0, MASK_NEG, 0.0)
        r0, r1, c0, c1 = box
        sub = idx[r0:r1, c0:c1]
        val = jnp.zeros(sub.shape, jnp.float32)
        for b in range(N_BUCKETS - 1):
            val = jnp.where(sub == b, rb_ref[h, b] - last, val)
        out[r0:r1, c0:c1] = jnp.where(sub < 0, MASK_NEG, val * LOG2E)

    fill(idx_c_ref[...], tab_c_ref.at[0], boxes[0])
    for k in range(idx_n_ref.shape[0]):
        fill(idx_n_ref[k], tab_n_ref.at[0, k], boxes[1 + k])


def _bias_tables(rel_bias):
    u = np.arange(TAB_C_ROWS)[:, None] - TAB_C_ORIGIN
    row = np.arange(TQ)[None, :]
    d_c = row - S_CMP * u - (L_CMP - 1)
    idx_c = _bucket_or_mask(d_c)
    key = np.arange(TK)[:, None]
    idx_n = np.stack([_bucket_or_mask(dd + row - key) for dd in (0, TK)])
    boxes = tuple(_band_box(t) for t in (idx_c, *idx_n))
    return pl.pallas_call(
        functools.partial(_bias_tables_kernel, boxes),
        grid=(N_HEADS,),
        in_specs=[pl.BlockSpec(memory_space=pltpu.SMEM),
                  pl.BlockSpec((TAB_C_ROWS, TQ), lambda h: (0, 0)),
                  pl.BlockSpec((2, TK, TQ), lambda h: (0, 0, 0))],
        out_specs=[pl.BlockSpec((1, TAB_C_ROWS, TQ), lambda h: (h, 0, 0)),
                   pl.BlockSpec((1, 2, TK, TQ), lambda h: (h, 0, 0, 0))],
        out_shape=[jax.ShapeDtypeStruct((N_HEADS, TAB_C_ROWS, TQ), jnp.float32),
                   jax.ShapeDtypeStruct((N_HEADS, 2, TK, TQ), jnp.float32)],
        name="bias_tables",
    )(rel_bias, jnp.asarray(idx_c), jnp.asarray(idx_n))


def _in_proj_kernel(x_ref, g_ref, wt_ref, wnat_ref, cw_ref,
                    qt_ref, kc_ref, vc_ref, ks_ref, kw_ref, vst_ref, vwt_ref, gt_ref,
                    oconv_ref, carry_ref, stage_ref):
    i = pl.program_id(1)
    tm = x_ref.shape[1]

    @pl.when(i == 0)
    def _():
        carry_ref[...] = jnp.zeros_like(carry_ref)

    hb = _rms(x_ref[0], g_ref[...]).astype(jnp.bfloat16)

    tt = lax.dot_general(wt_ref[...], hb, _NT, preferred_element_type=jnp.float32)
    qt_ref[0] = tt[0:D_NSA].astype(jnp.bfloat16)
    vt = tt[D_NSA:D_NSA + 2 * KV_W]
    ones = jnp.ones((V_ROWS - HEAD_DIM, TK), jnp.bfloat16)
    for t in range(tm // TK):
        for g in range(N_KV):
            for out_ref, r0 in ((vst_ref, g * HEAD_DIM), (vwt_ref, KV_W + g * HEAD_DIM)):
                out_ref[0, t, g, 0:HEAD_DIM, :] = vt[r0:r0 + HEAD_DIM,
                                                      t * TK:(t + 1) * TK].astype(jnp.bfloat16)
                out_ref[0, t, g, HEAD_DIM:V_ROWS, :] = ones
    gt_ref[0] = jax.nn.sigmoid(tt[D_NSA + 2 * KV_W:D_NSA + 2 * KV_W + GATE_ROWS])

    kv_c = jnp.dot(hb, wnat_ref[:, 0:2 * KV_W], preferred_element_type=jnp.float32)
    for a, out_ref in enumerate((kc_ref, vc_ref)):
        stage_ref[a] = kv_c[:, a * KV_W:(a + 1) * KV_W]
        for l in range(S_CMP):
            rows = stage_ref[a, pl.ds(l, tm // S_CMP, stride=S_CMP), :]
            for g in range(N_KV):
                out_ref[0, g, :, l * HEAD_DIM:(l + 1) * HEAD_DIM] = rows[
                    :, g * HEAD_DIM:(g + 1) * HEAD_DIM]
    k_sw = jnp.dot(hb, wnat_ref[:, 2 * KV_W:4 * KV_W],
                   preferred_element_type=jnp.float32)
    blk = (i * tm + lax.broadcasted_iota(jnp.int32, (tm, KV_W), 0)) // L_SLC
    col = lax.broadcasted_iota(jnp.int32, (tm, KV_W), 1)
    onehot = jnp.where(blk == col, 1.0, 0.0)
    ks_ref[0, :, 0:KV_W] = k_sw[:, 0:KV_W].astype(jnp.bfloat16)
    ks_ref[0, :, KV_W:2 * KV_W] = onehot.astype(jnp.bfloat16)
    kw_ref[0] = k_sw[:, KV_W:2 * KV_W].astype(jnp.bfloat16)

    c0 = 4 * KV_W
    bg = jnp.dot(hb, wnat_ref[:, c0:c0 + D_CONV], preferred_element_type=jnp.float32)
    cg = jnp.dot(hb, wnat_ref[:, c0 + D_CONV:c0 + 2 * D_CONV], preferred_element_type=jnp.float32)
    xt = jnp.dot(hb, wnat_ref[:, c0 + 2 * D_CONV:c0 + 3 * D_CONV],
                 preferred_element_type=jnp.float32)
    z = cg * xt
    prev1 = carry_ref[SUBLANES - 1:SUBLANES, :]
    prev2 = carry_ref[SUBLANES - 2:SUBLANES - 1, :]
    row = lax.broadcasted_iota(jnp.int32, z.shape, 0)
    z1 = jnp.where(row == 0, prev1, pltpu.roll(z, 1, 0))
    z2 = jnp.where(row == 0, prev2, jnp.where(row == 1, prev1, pltpu.roll(z, 2, 0)))
    y = cw_ref[0:1, :] * z2 + cw_ref[1:2, :] * z1 + cw_ref[2:3, :] * z
    oconv_ref[0] = (bg * y).astype(jnp.bfloat16)
    carry_ref[...] = z[tm - SUBLANES:tm, :]


def _in_proj(x, g_pre, w_in, conv_w):
    B, S, D = x.shape
    tm = TM_IN
    widths = dict(q=D_NSA, k_c=KV_W, v_c=KV_W, k_s=KV_W, v_s=KV_W, k_w=KV_W, v_w=KV_W,
                  gate=N_GATES, b=D_CONV, c=D_CONV, x=D_CONV)
    assert w_in.shape[1] == sum(widths.values())
    col, cols = 0, {}
    for name, width in widths.items():
        cols[name] = w_in[:, col:col + width]
        col += width
    wq = (cols['q'].T * (HEAD_DIM ** -0.5 * LOG2E)).astype(jnp.bfloat16)
    wnat = jnp.concatenate([cols[n] for n in ('k_c', 'v_c', 'k_s', 'k_w', 'b', 'c', 'x')],
                           axis=1).astype(jnp.bfloat16)
    wgate = jnp.pad(cols['gate'], ((0, 0), (0, GATE_ROWS - N_GATES)))
    wt = jnp.concatenate([wq, jnp.concatenate([cols['v_s'], cols['v_w'], wgate], axis=1).T
                          .astype(jnp.bfloat16)], axis=0)
    t_rows = D_NSA + 2 * KV_W + GATE_ROWS
    nat_w = 4 * KV_W + 3 * D_CONV
    nt = S // TK
    const = lambda b, i: (0, 0)
    outs = pl.pallas_call(
        _in_proj_kernel,
        grid=(B, S // tm),
        in_specs=[pl.BlockSpec((1, tm, D), lambda b, i: (b, i, 0)),
                  pl.BlockSpec((1, D), const),
                  pl.BlockSpec((t_rows, D), const),
                  pl.BlockSpec((D, nat_w), const),
                  pl.BlockSpec((3, D_CONV), const)],
        out_specs=[pl.BlockSpec((1, D_NSA, tm), lambda b, i: (b, 0, i)),
                   pl.BlockSpec((1, N_KV, tm // S_CMP, S_CMP * HEAD_DIM), lambda b, i: (b, 0, i, 0)),
                   pl.BlockSpec((1, N_KV, tm // S_CMP, S_CMP * HEAD_DIM), lambda b, i: (b, 0, i, 0)),
                   pl.BlockSpec((1, tm, 2 * KV_W), lambda b, i: (b, i, 0)),
                   pl.BlockSpec((1, tm, KV_W), lambda b, i: (b, i, 0)),
                   pl.BlockSpec((1, tm // TK, N_KV, V_ROWS, TK), lambda b, i: (b, i, 0, 0, 0)),
                   pl.BlockSpec((1, tm // TK, N_KV, V_ROWS, TK), lambda b, i: (b, i, 0, 0, 0)),
                   pl.BlockSpec((1, GATE_ROWS, tm), lambda b, i: (b, 0, i)),
                   pl.BlockSpec((1, tm, D_CONV), lambda b, i: (b, i, 0))],
        out_shape=[jax.ShapeDtypeStruct((B, D_NSA, S), jnp.bfloat16),
                   jax.ShapeDtypeStruct((B, N_KV, S // S_CMP, S_CMP * HEAD_DIM), jnp.float32),
                   jax.ShapeDtypeStruct((B, N_KV, S // S_CMP, S_CMP * HEAD_DIM), jnp.float32),
                   jax.ShapeDtypeStruct((B, S, 2 * KV_W), jnp.bfloat16),
                   jax.ShapeDtypeStruct((B, S, KV_W), jnp.bfloat16),
                   jax.ShapeDtypeStruct((B, nt, N_KV, V_ROWS, TK), jnp.bfloat16),
                   jax.ShapeDtypeStruct((B, nt, N_KV, V_ROWS, TK), jnp.bfloat16),
                   jax.ShapeDtypeStruct((B, GATE_ROWS, S), jnp.float32),
                   jax.ShapeDtypeStruct((B, S, D_CONV), jnp.bfloat16)],
        scratch_shapes=[pltpu.VMEM((SUBLANES, D_CONV), jnp.float32),
                        pltpu.VMEM((2, tm, KV_W), jnp.float32)],
        compiler_params=pltpu.CompilerParams(
            dimension_semantics=("arbitrary", "arbitrary"), vmem_limit_bytes=VMEM_LIMIT),
        name="in_proj",
    )(x, g_pre.reshape(1, D), wt, wnat, conv_w)
    return outs


def _compress_kernel(ck_ref, cv_ref, pek_ref, pev_ref, wk1_ref, wk2_ref, wv1_ref, wv2t_ref,
                     ovl_ref, kc_ref, vct_ref):
    def hidden(c_ref, pe_ref, w1_ref):
        c = c_ref[0, 0]
        a = jnp.dot((c + pe_ref[0:1, :]).astype(jnp.bfloat16), w1_ref[0].astype(jnp.bfloat16),
                    preferred_element_type=jnp.float32)
        b = jnp.dot((c + pe_ref[1:2, :]).astype(jnp.bfloat16), w1_ref[1].astype(jnp.bfloat16),
                    preferred_element_type=jnp.float32)
        return _gelu(a + pltpu.roll(b, N_CMP_PAD - 1, 0)).astype(jnp.bfloat16)

    kc_ref[0, 0] = jnp.dot(hidden(ck_ref, pek_ref, wk1_ref), wk2_ref[...].astype(jnp.bfloat16),
                           preferred_element_type=jnp.float32).astype(jnp.bfloat16)
    vct_ref[0, 0, 0:HEAD_DIM, :] = lax.dot_general(
        wv2t_ref[...].astype(jnp.bfloat16), hidden(cv_ref, pev_ref, wv1_ref), _NT,
        preferred_element_type=jnp.float32).astype(jnp.bfloat16)
    vct_ref[0, 0, HEAD_DIM:V_ROWS, :] = jnp.ones((V_ROWS - HEAD_DIM, N_CMP_PAD), jnp.bfloat16)
    vct_ref[0, 0, V_ROWS:C_ROWS, :] = ovl_ref[...]


def _compress(kc_in, vc_in, pe_k, pe_v, wk1, wk2, wv1, wv2):
    B, _, n_str, half = kc_in.shape
    const2 = lambda b, g: (0, 0)
    const3 = lambda b, g: (0, 0, 0)
    return pl.pallas_call(
        _compress_kernel,
        grid=(B, N_KV),
        in_specs=[pl.BlockSpec((1, 1, n_str, half), lambda b, g: (b, g, 0, 0)),
                  pl.BlockSpec((1, 1, n_str, half), lambda b, g: (b, g, 0, 0)),
                  pl.BlockSpec((2, half), const2),
                  pl.BlockSpec((2, half), const2),
                  pl.BlockSpec((2, half, CMP_HIDDEN), const3),
                  pl.BlockSpec((CMP_HIDDEN, HEAD_DIM), const2),
                  pl.BlockSpec((2, half, CMP_HIDDEN), const3),
                  pl.BlockSpec((HEAD_DIM, CMP_HIDDEN), const2),
                  pl.BlockSpec((C_ROWS - V_ROWS, n_str), const2)],
        out_specs=[pl.BlockSpec((1, 1, n_str, HEAD_DIM), lambda b, g: (b, g, 0, 0)),
                   pl.BlockSpec((1, 1, C_ROWS, n_str), lambda b, g: (b, g, 0, 0))],
        out_shape=[jax.ShapeDtypeStruct((B, N_KV, n_str, HEAD_DIM), jnp.bfloat16),
                   jax.ShapeDtypeStruct((B, N_KV, C_ROWS, n_str), jnp.bfloat16)],
        compiler_params=pltpu.CompilerParams(
            dimension_semantics=("arbitrary", "arbitrary"), vmem_limit_bytes=VMEM_LIMIT),
        name="compress",
    )(kc_in, vc_in, pe_k.reshape(2, half), pe_v.reshape(2, half),
      wk1.reshape(2, half, CMP_HIDDEN), wk2, wv1.reshape(2, half, CMP_HIDDEN), wv2.T,
      _overlap_t(n_str * S_CMP))


def _overlap_t(S):
    n_slc = S // L_SLC
    c_start = np.arange(N_CMP_PAD) * S_CMP
    s_start = np.arange(n_slc) * L_SLC
    ov = np.clip(np.minimum(c_start[:, None] + L_CMP, s_start[None, :] + L_SLC)
                 - np.maximum(c_start[:, None], s_start[None, :]), 0, None).astype(np.float32) / L_CMP
    ov[(S - L_CMP) // S_CMP + 1:, :] = 0.0
    return jnp.asarray(ov.T, dtype=jnp.bfloat16)


def _select_top(score, k, tri):
    w = score
    cnt = jnp.zeros((1, TQ), jnp.float32)
    thr = jnp.zeros((1, TQ), jnp.float32)
    n_gt = jnp.zeros((1, TQ), jnp.float32)
    for _ in range(N_FREE):
        mx = jnp.max(w, axis=0, keepdims=True)
        eq = w == mx
        c = jnp.sum(jnp.where(eq, 1.0, 0.0), axis=0, keepdims=True)
        cross = (cnt < k) & (cnt + c >= k)
        thr = jnp.where(cross, mx, thr)
        n_gt = jnp.where(cross, cnt, n_gt)
        cnt = cnt + c
        w = jnp.where(eq, -jnp.inf, w)
    at_thr = score == thr
    earlier = jnp.dot(tri, jnp.where(at_thr, 1.0, 0.0).astype(jnp.bfloat16),
                      preferred_element_type=jnp.float32)
    return (score > thr) | (at_thr & (earlier + n_gt < k))


def _nsa_attn_kernel(qt_ref, kc_ref, vc_ref, tabc_ref, tri_ref, ks_ref, kw_ref, vst_ref, vwt_ref,
                     near_ref, edge_ref, gt_ref, o_ref,
                     m_ref, acc_ref, mw_ref, accw_ref, s_ref, tmax_ref, sel_ref, oc_ref):
    i = pl.program_id(1)
    zeros64 = jnp.zeros((HEAD_DIM, TQ), jnp.bfloat16)
    off1 = jnp.where(i >= 1, 0.0, MASK_NEG)
    off2 = jnp.where(i >= 2, 0.0, MASK_NEG)
    j1 = jnp.maximum(i - 1, 0)
    j2 = jnp.maximum(i - 2, 0)
    dot = functools.partial(jnp.dot, preferred_element_type=jnp.float32)

    def q_window(h):
        qh = qt_ref[0, h * HEAD_DIM:(h + 1) * HEAD_DIM, :]
        return jnp.concatenate([qh, zeros64] if h // REP == 0 else [zeros64, qh], axis=0)

    def q_selected(h):
        return jnp.concatenate([q_window(h), sel_ref[h // REP]], axis=0)

    def stage_logits(slot, k_ref, j, q_of, bias_of=None):
        k = k_ref[0, pl.ds(pl.multiple_of(j * TK, TK), TK), :]
        for h in range(N_HEADS):
            s = dot(k, q_of(h))
            if bias_of is not None:
                s = s + bias_of(h)
            s_ref[slot, h] = s
            tmax_ref[slot, h:h + 1, :] = jnp.max(s, axis=0, keepdims=True)

    def accumulate(slot, v_ref, j, state, first=False, shift=None):
        ms_ref, as_ref = state
        for h in range(N_HEADS):
            t_max = tmax_ref[slot, h:h + 1, :]
            if shift is not None:
                t_max = t_max + shift
            m_new = t_max if first else jnp.maximum(ms_ref[h:h + 1, :], t_max)
            m_sub = m_new if shift is None else m_new - shift
            p = jnp.exp2(s_ref[slot, h] - m_sub).astype(jnp.bfloat16)
            pv = dot(v_ref[0, j, h // REP], p)
            if first:
                as_ref[h] = pv
            else:
                as_ref[h] = jnp.exp2(ms_ref[h:h + 1, :] - m_new) * as_ref[h] + pv
            ms_ref[h:h + 1, :] = m_new

    sel_state = (m_ref, acc_ref)
    win_state = (mw_ref, accw_ref)
    near0 = lambda h: near_ref[h, 0]
    near1 = lambda h: near_ref[h, 1]
    edge = lambda h: edge_ref[...]

    off_c = pl.multiple_of(TAB_C_ORIGIN - (TQ // S_CMP) * i, TQ // S_CMP)
    for h in range(N_HEADS):
        s = (dot(kc_ref[0, h // REP], qt_ref[0, h * HEAD_DIM:(h + 1) * HEAD_DIM, :])
             + tabc_ref[h, pl.ds(off_c, N_CMP_PAD), :])
        s_ref[0, h] = s
        tmax_ref[0, h:h + 1, :] = jnp.max(s, axis=0, keepdims=True)
    stage_logits(1, kw_ref, i, q_window, near0)

    t_row = i * TQ + lax.broadcasted_iota(jnp.int32, (1, TQ), 1)
    any_cmp = jnp.where(t_row >= L_CMP - 1, 1.0, 0.0)
    n_blk = tri_ref.shape[0]
    j_idx = lax.broadcasted_iota(jnp.int32, (n_blk, TQ), 0)
    cur = (i * TQ + lax.broadcasted_iota(jnp.int32, (n_blk, TQ), 1)) // L_SLC
    forced = (j_idx == 0) | (j_idx == cur) | (j_idx == cur - 1)
    candidate = (j_idx < cur - 1) & (j_idx > 0)
    n_free = float(N_SEL - 1) - jnp.minimum(t_row // L_SLC, 2).astype(jnp.float32)
    for g in range(N_KV):
        imp = jnp.zeros((n_blk, TQ), jnp.float32)
        for h in range(g * REP, (g + 1) * REP):
            p = jnp.exp2(s_ref[0, h] - tmax_ref[0, h:h + 1, :]).astype(jnp.bfloat16)
            a = dot(vc_ref[0, g], p)
            inv = any_cmp / a[HEAD_DIM:HEAD_DIM + 1, :]
            oc_ref[h * HEAD_DIM:(h + 1) * HEAD_DIM, :] = a[0:HEAD_DIM, :] * inv
            imp = imp + a[V_ROWS:V_ROWS + n_blk, :] * inv
        free = _select_top(jnp.where(candidate, imp, MASK_NEG), n_free, tri_ref[...])
        sel = forced | (candidate & free)
        sel_ref[g, 0:n_blk, :] = jnp.where(sel, 0.0, MASK_NEG).astype(jnp.bfloat16)
        sel_ref[g, n_blk:2 * n_blk, :] = jnp.zeros((n_blk, TQ), jnp.bfloat16)

    accumulate(1, vwt_ref, i, win_state, first=True)
    stage_logits(0, ks_ref, i, q_selected, near0)
    stage_logits(1, kw_ref, j1, q_window, near1)
    accumulate(0, vst_ref, i, sel_state, first=True)
    stage_logits(0, ks_ref, j1, q_selected, near1)
    accumulate(1, vwt_ref, j1, win_state, shift=off1)
    stage_logits(1, kw_ref, j2, q_window, edge)
    accumulate(0, vst_ref, j1, sel_state, shift=off1)

    n_far = j1
    last = jnp.maximum(n_far - 1, 0)
    stage_logits(0, ks_ref, 0, q_selected)
    accumulate(1, vwt_ref, j2, win_state, shift=off2)

    def far_pair(t):
        stage_logits(1, ks_ref, t + 1, q_selected)
        accumulate(0, vst_ref, t, sel_state)
        stage_logits(0, ks_ref, jnp.minimum(t + 2, last), q_selected)
        accumulate(1, vst_ref, t + 1, sel_state)

    def far_quad(c, carry):
        far_pair(4 * c)
        far_pair(4 * c + 2)
        return carry

    lax.fori_loop(0, n_far // 4, far_quad, 0)

    @pl.when(n_far % 4 >= 2)
    def _():
        far_pair(n_far // 4 * 4)

    @pl.when(n_far % 2 == 1)
    def _():
        accumulate(0, vst_ref, last, sel_state)

    for pair in range(N_HEADS // 2):
        outs = []
        for h in (2 * pair, 2 * pair + 1):
            rows = slice(h * HEAD_DIM, (h + 1) * HEAD_DIM)
            o_s = acc_ref[h, 0:HEAD_DIM, :] / acc_ref[h, HEAD_DIM:HEAD_DIM + 1, :]
            o_w = accw_ref[h, 0:HEAD_DIM, :] / accw_ref[h, HEAD_DIM:HEAD_DIM + 1, :]
            outs.append(gt_ref[0, 3 * h:3 * h + 1, :] * oc_ref[rows, :]
                        + gt_ref[0, 3 * h + 1:3 * h + 2, :] * o_s
                        + gt_ref[0, 3 * h + 2:3 * h + 3, :] * o_w)
        o_pair = jnp.concatenate(outs, axis=0)
        o_ref[0, :, pair * 2 * HEAD_DIM:(pair + 1) * 2 * HEAD_DIM] = o_pair.T.astype(jnp.bfloat16)


def _nsa_attn(qt, kc, vc, tab_c, ks, kw, vst, vwt, near, gt):
    B, _, S = qt.shape
    ni = S // TQ
    nt = S // TK
    n_blk = S // L_SLC
    key = np.arange(TK)[:, None]
    row = np.arange(TQ)[None, :]
    edge = jnp.asarray(np.where(2 * TK + row - key < WINDOW, 0.0, MASK_NEG).astype(np.float32))
    tri = jnp.asarray(np.tril(np.ones((n_blk, n_blk), np.float32), -1), dtype=jnp.bfloat16)
    per_batch = lambda shape: pl.BlockSpec((1,) + shape, lambda b, i: (b,) + (0,) * len(shape))
    const = lambda shape: pl.BlockSpec(shape, lambda b, i: (0,) * len(shape),
                                       pipeline_mode=pl.Buffered(1))
    return pl.pallas_call(
        _nsa_attn_kernel,
        grid=(B, ni),
        in_specs=[pl.BlockSpec((1, D_NSA, TQ), lambda b, i: (b, 0, i)),
                  per_batch((N_KV, N_CMP_PAD, HEAD_DIM)),
                  per_batch((N_KV, C_ROWS, N_CMP_PAD)),
                  const((N_HEADS, TAB_C_ROWS, TQ)),
                  const((n_blk, n_blk)),
                  per_batch((S, 2 * KV_W)),
                  per_batch((S, KV_W)),
                  per_batch((nt, N_KV, V_ROWS, TK)),
                  per_batch((nt, N_KV, V_ROWS, TK)),
                  const((N_HEADS, 2, TK, TQ)),
                  const((TK, TQ)),
                  pl.BlockSpec((1, GATE_ROWS, TQ), lambda b, i: (b, 0, i))],
        out_specs=pl.BlockSpec((1, TQ, D_NSA), lambda b, i: (b, i, 0)),
        out_shape=jax.ShapeDtypeStruct((B, S, D_NSA), jnp.bfloat16),
        scratch_shapes=[pltpu.VMEM((N_HEADS, TQ), jnp.float32),
                        pltpu.VMEM((N_HEADS, V_ROWS, TQ), jnp.float32),
                        pltpu.VMEM((N_HEADS, TQ), jnp.float32),
                        pltpu.VMEM((N_HEADS, V_ROWS, TQ), jnp.float32),
                        pltpu.VMEM((2, N_HEADS, TK, TQ), jnp.float32),
                        pltpu.VMEM((2, N_HEADS, TQ), jnp.float32),
                        pltpu.VMEM((N_KV, 2 * n_blk, TQ), jnp.bfloat16),
                        pltpu.VMEM((D_NSA, TQ), jnp.float32)],
        compiler_params=pltpu.CompilerParams(
            dimension_semantics=("arbitrary", "arbitrary"), vmem_limit_bytes=VMEM_LIMIT),
        name="nsa_attn",
    )(qt, kc, vc, tab_c, tri, ks, kw, vst, vwt, near, edge, gt)


def _out_ffn_kernel(on_ref, oc_ref, x_ref, wo_ref, gpost_ref, gffn_ref, wup_ref, cw_ref, wdn_ref,
                    g_ref, o_ref, carry_ref, act_ref):
    i = pl.program_id(1)
    tm = x_ref.shape[1]

    @pl.when(i == 0)
    def _():
        carry_ref[...] = jnp.zeros_like(carry_ref)

    y = jnp.dot(jnp.concatenate([on_ref[0], oc_ref[0]], axis=1), wo_ref[...],
                preferred_element_type=jnp.float32)
    x1 = x_ref[0] + _rms(y, gpost_ref[...])
    hb = _rms(x1, gffn_ref[...]).astype(jnp.bfloat16)

    row = lax.broadcasted_iota(jnp.int32, (tm, FF_CHUNK), 0)
    is0 = row == 0
    is1 = row == 1

    def conv(z, c0):
        cols = slice(c0, c0 + FF_CHUNK)
        prev1 = carry_ref[SUBLANES - 1:SUBLANES, cols]
        prev2 = carry_ref[SUBLANES - 2:SUBLANES - 1, cols]
        z1 = jnp.where(is0, prev1, pltpu.roll(z, 1, 0))
        z2 = jnp.where(is0, prev2, jnp.where(is1, prev1, pltpu.roll(z, 2, 0)))
        carry_ref[:, cols] = z[tm - SUBLANES:tm, :]
        return cw_ref[0:1, cols] * z2 + cw_ref[1:2, cols] * z1 + cw_ref[2:3, cols] * z

    for c in range(D_FF // FF_CHUNK):
        g0 = c * FF_CHUNK
        u0 = D_FF + c * FF_CHUNK
        zg = jnp.dot(hb, wup_ref[:, g0:g0 + FF_CHUNK], preferred_element_type=jnp.float32)
        zu = jnp.dot(hb, wup_ref[:, u0:u0 + FF_CHUNK], preferred_element_type=jnp.float32)
        act = _gelu(conv(zg, g0)) * conv(zu, u0)
        act_ref[:, g0:g0 + FF_CHUNK] = act.astype(jnp.bfloat16)
    y2 = jnp.dot(act_ref[...], wdn_ref[...], preferred_element_type=jnp.float32)
    o_ref[0] = x1 + _rms(y2, g_ref[...])


def _out_ffn(o_nsa, o_conv, x, w_out, g_post, g_ffn, w_up, conv_w, w_down, g_ffn_post):
    B, S, D = x.shape
    tm = TM_FFN
    const = lambda b, i: (0, 0)
    single = pl.Buffered(1)
    row_tile = lambda width: pl.BlockSpec((1, tm, width), lambda b, i: (b, i, 0))
    weight = lambda shape: pl.BlockSpec(shape, const, pipeline_mode=single)
    return pl.pallas_call(
        _out_ffn_kernel,
        grid=(B, S // tm),
        in_specs=[row_tile(D_NSA), row_tile(D_CONV), row_tile(D),
                  weight((D, D)), weight((1, D)), weight((1, D)),
                  weight((D, 2 * D_FF)), weight((3, 2 * D_FF)), weight((D_FF, D)), weight((1, D))],
        out_specs=row_tile(D),
        out_shape=jax.ShapeDtypeStruct((B, S, D), jnp.float32),
        scratch_shapes=[pltpu.VMEM((SUBLANES, 2 * D_FF), jnp.float32),
                        pltpu.VMEM((tm, D_FF), jnp.bfloat16)],
        compiler_params=pltpu.CompilerParams(
            dimension_semantics=("arbitrary", "arbitrary"), vmem_limit_bytes=VMEM_LIMIT),
        name="out_ffn",
    )(o_nsa, o_conv, x, w_out.astype(jnp.bfloat16), g_post.reshape(1, D), g_ffn.reshape(1, D),
      w_up.astype(jnp.bfloat16), conv_w, w_down.astype(jnp.bfloat16), g_ffn_post.reshape(1, D))


def kernel(x, norm_mix_pre, norm_mix_post, norm_ffn_pre, norm_ffn_post, w_in, pe_cmp_k, pe_cmp_v,
           w_cmp_k1, w_cmp_k2, w_cmp_v1, w_cmp_v2, rel_bias, conv_mix_w, w_out, w_ffn_up,
           ffn_conv_w, w_ffn_down):
    B, S, D = x.shape
    assert (S, D) == (4096, D_MODEL) and norm_mix_pre.shape[0] == 1
    tab_c, near = _bias_tables(rel_bias)
    for l in range(norm_mix_pre.shape[0]):
        qt, kc_in, vc_in, ks, kw, vst, vwt, gt, o_conv = _in_proj(
            x, norm_mix_pre[l], w_in[l], conv_mix_w[l])
        kc, vc = _compress(kc_in, vc_in, pe_cmp_k[l], pe_cmp_v[l], w_cmp_k1[l], w_cmp_k2[l],
                           w_cmp_v1[l], w_cmp_v2[l])
        o_nsa = _nsa_attn(qt, kc, vc, tab_c, ks, kw, vst, vwt, near, gt)
        x = _out_ffn(o_nsa, o_conv, x, w_out[l], norm_mix_post[l], norm_ffn_pre[l],
                     w_ffn_up[l], ffn_conv_w[l], w_ffn_down[l], norm_ffn_post[l])
    return x
```

```python
import functools
import math

import numpy as np
import jax
import jax.numpy as jnp
from jax import lax
from jax.experimental import pallas as pl
from jax.experimental.pallas import tpu as pltpu

D_MODEL = 1024
D_NSA = 512
D_CONV = 512
HEAD_DIM = 64
N_HEADS = 8
N_KV = 2
REP = 4
KV_W = 128
N_GATES = 24
L_CMP = 32
S_CMP = 16
L_SLC = 64
N_SEL = 16
WINDOW = 512
CMP_HIDDEN = 256
D_FF = 2816
N_BUCKETS = 32
MAX_DIST = 128
RMS_EPS = 1e-6
MASK_NEG = -1e30
N_FREE = N_SEL - 3
LOG2E = math.log2(math.e)

LANES = 128
SUBLANES = 8
BF16_ROWS = 2 * SUBLANES
GATE_ROWS = -(-N_GATES // BF16_ROWS) * BF16_ROWS
TQ = 256
TK = 256
V_ROWS = 80
C_ROWS = V_ROWS + 64
N_CMP_PAD = 256
TAB_C_ROWS = 2 * N_CMP_PAD
TAB_C_ORIGIN = N_CMP_PAD - TQ // S_CMP
TM_IN = 1024
TM_FFN = 1024
FF_CHUNK = 256
VMEM_LIMIT = 56 * 1024 * 1024

_NT = (((1,), (1,)), ((), ()))


def _bucket_np(d):
    max_exact = N_BUCKETS // 2
    d = np.maximum(d, 0)
    df = np.maximum(d, 1).astype(np.float32)
    large = max_exact + (np.log(df / max_exact) / math.log(MAX_DIST / max_exact)
                         * (N_BUCKETS - max_exact)).astype(np.int32)
    return np.where(d < max_exact, d, np.minimum(large, N_BUCKETS - 1)).astype(np.int32)


def _bucket_or_mask(d):
    return np.where(d >= 0, _bucket_np(d), -1).astype(np.int32)


def _rms(x, g):
    return x * lax.rsqrt(jnp.mean(x * x, axis=-1, keepdims=True) + RMS_EPS) * g


def _gelu(x):
    return jax.nn.gelu(x, approximate=True)


def _band_box(idx):
    rows, cols = np.nonzero((idx >= 0) & (idx < N_BUCKETS - 1))
    r0, r1, c0, c1 = int(rows.min()), int(rows.max()) + 1, int(cols.min()), int(cols.max()) + 1
    return (r0 // SUBLANES * SUBLANES, -(-r1 // SUBLANES) * SUBLANES,
            c0 // LANES * LANES, -(-c1 // LANES) * LANES)


def _bias_tables_kernel(boxes, rb_ref, idx_c_ref, idx_n_ref, tab_c_ref, tab_n_ref):
    h = pl.program_id(0)
    last = rb_ref[h, N_BUCKETS - 1]

    def fill(idx, out, box):
        out[...] = jnp.where(idx < 0, MASK_NEG, 0.0)
        r0, r1, c0, c1 = box
        sub = idx[r0:r1, c0:c1]
        val = jnp.zeros(sub.shape, jnp.float32)
        for b in range(N_BUCKETS - 1):
            val = jnp.where(sub == b, rb_ref[h, b] - last, val)
        out[r0:r1, c0:c1] = jnp.where(sub < 0, MASK_NEG, val * LOG2E)

    fill(idx_c_ref[...], tab_c_ref.at[0], boxes[0])
    for k in range(idx_n_ref.shape[0]):
        fill(idx_n_ref[k], tab_n_ref.at[0, k], boxes[1 + k])


def _bias_tables(rel_bias):
    u = np.arange(TAB_C_ROWS)[:, None] - TAB_C_ORIGIN
    row = np.arange(TQ)[None, :]
    d_c = row - S_CMP * u - (L_CMP - 1)
    idx_c = _bucket_or_mask(d_c)
    key = np.arange(TK)[:, None]
    idx_n = np.stack([_bucket_or_mask(dd + row - key) for dd in (0, TK)])
    boxes = tuple(_band_box(t) for t in (idx_c, *idx_n))
    return pl.pallas_call(
        functools.partial(_bias_tables_kernel, boxes),
        grid=(N_HEADS,),
        in_specs=[pl.BlockSpec(memory_space=pltpu.SMEM),
                  pl.BlockSpec((TAB_C_ROWS, TQ), lambda h: (0, 0)),
                  pl.BlockSpec((2, TK, TQ), lambda h: (0, 0, 0))],
        out_specs=[pl.BlockSpec((1, TAB_C_ROWS, TQ), lambda h: (h, 0, 0)),
                   pl.BlockSpec((1, 2, TK, TQ), lambda h: (h, 0, 0, 0))],
        out_shape=[jax.ShapeDtypeStruct((N_HEADS, TAB_C_ROWS, TQ), jnp.float32),
                   jax.ShapeDtypeStruct((N_HEADS, 2, TK, TQ), jnp.float32)],
        name="bias_tables",
    )(rel_bias, jnp.asarray(idx_c), jnp.asarray(idx_n))


def _in_proj_kernel(x_ref, g_ref, wt_ref, wnat_ref, cw_ref,
                    qt_ref, kc_ref, vc_ref, ks_ref, kw_ref, vst_ref, vwt_ref, gt_ref,
                    oconv_ref, carry_ref, stage_ref):
    i = pl.program_id(1)
    tm = x_ref.shape[1]

    @pl.when(i == 0)
    def _():
        carry_ref[...] = jnp.zeros_like(carry_ref)

    hb = _rms(x_ref[0], g_ref[...]).astype(jnp.bfloat16)

    tt = lax.dot_general(wt_ref[...], hb, (((0,), (1,)), ((), ())),
                         preferred_element_type=jnp.float32)
    qt_ref[0] = tt[0:D_NSA].astype(jnp.bfloat16)
    vt = tt[D_NSA:D_NSA + 2 * KV_W]
    ones = jnp.ones((V_ROWS - HEAD_DIM, TK), jnp.bfloat16)
    for t in range(tm // TK):
        for g in range(N_KV):
            for out_ref, r0 in ((vst_ref, g * HEAD_DIM), (vwt_ref, KV_W + g * HEAD_DIM)):
                out_ref[0, t, g, 0:HEAD_DIM, :] = vt[r0:r0 + HEAD_DIM,
                                                      t * TK:(t + 1) * TK].astype(jnp.bfloat16)
                out_ref[0, t, g, HEAD_DIM:V_ROWS, :] = ones
    gt_ref[0] = jax.nn.sigmoid(tt[D_NSA + 2 * KV_W:D_NSA + 2 * KV_W + GATE_ROWS])

    kv_c = jnp.dot(hb, wnat_ref[:, 0:2 * KV_W], preferred_element_type=jnp.float32)
    for a, out_ref in enumerate((kc_ref, vc_ref)):
        stage_ref[a] = kv_c[:, a * KV_W:(a + 1) * KV_W]
        for l in range(S_CMP):
            rows = stage_ref[a, pl.ds(l, tm // S_CMP, stride=S_CMP), :]
            for g in range(N_KV):
                out_ref[0, g, :, l * HEAD_DIM:(l + 1) * HEAD_DIM] = rows[
                    :, g * HEAD_DIM:(g + 1) * HEAD_DIM]
    k_sw = jnp.dot(hb, wnat_ref[:, 2 * KV_W:4 * KV_W],
                   preferred_element_type=jnp.float32)
    blk = (i * tm + lax.broadcasted_iota(jnp.int32, (tm, KV_W), 0)) // L_SLC
    col = lax.broadcasted_iota(jnp.int32, (tm, KV_W), 1)
    onehot = jnp.where(blk == col, 1.0, 0.0)
    ks_ref[0, :, 0:KV_W] = k_sw[:, 0:KV_W].astype(jnp.bfloat16)
    ks_ref[0, :, KV_W:2 * KV_W] = onehot.astype(jnp.bfloat16)
    kw_ref[0] = k_sw[:, KV_W:2 * KV_W].astype(jnp.bfloat16)

    c0 = 4 * KV_W
    bg = jnp.dot(hb, wnat_ref[:, c0:c0 + D_CONV], preferred_element_type=jnp.float32)
    cg = jnp.dot(hb, wnat_ref[:, c0 + D_CONV:c0 + 2 * D_CONV], preferred_element_type=jnp.float32)
    xt = jnp.dot(hb, wnat_ref[:, c0 + 2 * D_CONV:c0 + 3 * D_CONV],
                 preferred_element_type=jnp.float32)
    z = cg * xt
    prev1 = carry_ref[SUBLANES - 1:SUBLANES, :]
    prev2 = carry_ref[SUBLANES - 2:SUBLANES - 1, :]
    row = lax.broadcasted_iota(jnp.int32, z.shape, 0)
    z1 = jnp.where(row == 0, prev1, pltpu.roll(z, 1, 0))
    z2 = jnp.where(row == 0, prev2, jnp.where(row == 1, prev1, pltpu.roll(z, 2, 0)))
    y = cw_ref[0:1, :] * z2 + cw_ref[1:2, :] * z1 + cw_ref[2:3, :] * z
    oconv_ref[0] = (bg * y).astype(jnp.bfloat16)
    carry_ref[...] = z[tm - SUBLANES:tm, :]


def _in_proj(x, g_pre, w_in, conv_w):
    B, S, D = x.shape
    tm = TM_IN
    widths = dict(q=D_NSA, k_c=KV_W, v_c=KV_W, k_s=KV_W, v_s=KV_W, k_w=KV_W, v_w=KV_W,
                  gate=N_GATES, b=D_CONV, c=D_CONV, x=D_CONV)
    assert w_in.shape[1] == sum(widths.values())
    col, cols = 0, {}
    for name, width in widths.items():
        cols[name] = w_in[:, col:col + width]
        col += width
    wnat = jnp.concatenate([cols[n] for n in ('k_c', 'v_c', 'k_s', 'k_w', 'b', 'c', 'x')],
                           axis=1).astype(jnp.bfloat16)
    wgate = jnp.pad(cols['gate'], ((0, 0), (0, GATE_ROWS - N_GATES)))
    wt = jnp.concatenate([cols['q'] * (HEAD_DIM ** -0.5 * LOG2E), cols['v_s'], cols['v_w'], wgate],
                         axis=1).astype(jnp.bfloat16)
    t_rows = D_NSA + 2 * KV_W + GATE_ROWS
    nat_w = 4 * KV_W + 3 * D_CONV
    nt = S // TK
    const = lambda b, i: (0, 0)
    outs = pl.pallas_call(
        _in_proj_kernel,
        grid=(B, S // tm),
        in_specs=[pl.BlockSpec((1, tm, D), lambda b, i: (b, i, 0)),
                  pl.BlockSpec((1, D), const),
                  pl.BlockSpec((D, t_rows), const),
                  pl.BlockSpec((D, nat_w), const),
                  pl.BlockSpec((3, D_CONV), const)],
        out_specs=[pl.BlockSpec((1, D_NSA, tm), lambda b, i: (b, 0, i)),
                   pl.BlockSpec((1, N_KV, tm // S_CMP, S_CMP * HEAD_DIM), lambda b, i: (b, 0, i, 0)),
                   pl.BlockSpec((1, N_KV, tm // S_CMP, S_CMP * HEAD_DIM), lambda b, i: (b, 0, i, 0)),
                   pl.BlockSpec((1, tm, 2 * KV_W), lambda b, i: (b, i, 0)),
                   pl.BlockSpec((1, tm, KV_W), lambda b, i: (b, i, 0)),
                   pl.BlockSpec((1, tm // TK, N_KV, V_ROWS, TK), lambda b, i: (b, i, 0, 0, 0)),
                   pl.BlockSpec((1, tm // TK, N_KV, V_ROWS, TK), lambda b, i: (b, i, 0, 0, 0)),
                   pl.BlockSpec((1, GATE_ROWS, tm), lambda b, i: (b, 0, i)),
                   pl.BlockSpec((1, tm, D_CONV), lambda b, i: (b, i, 0))],
        out_shape=[jax.ShapeDtypeStruct((B, D_NSA, S), jnp.bfloat16),
                   jax.ShapeDtypeStruct((B, N_KV, S // S_CMP, S_CMP * HEAD_DIM), jnp.float32),
                   jax.ShapeDtypeStruct((B, N_KV, S // S_CMP, S_CMP * HEAD_DIM), jnp.float32),
                   jax.ShapeDtypeStruct((B, S, 2 * KV_W), jnp.bfloat16),
                   jax.ShapeDtypeStruct((B, S, KV_W), jnp.bfloat16),
                   jax.ShapeDtypeStruct((B, nt, N_KV, V_ROWS, TK), jnp.bfloat16),
                   jax.ShapeDtypeStruct((B, nt, N_KV, V_ROWS, TK), jnp.bfloat16),
                   jax.ShapeDtypeStruct((B, GATE_ROWS, S), jnp.float32),
                   jax.ShapeDtypeStruct((B, S, D_CONV), jnp.bfloat16)],
        scratch_shapes=[pltpu.VMEM((SUBLANES, D_CONV), jnp.float32),
                        pltpu.VMEM((2, tm, KV_W), jnp.float32)],
        compiler_params=pltpu.CompilerParams(
            dimension_semantics=("arbitrary", "arbitrary"), vmem_limit_bytes=VMEM_LIMIT),
        name="in_proj",
    )(x, g_pre.reshape(1, D), wt, wnat, conv_w)
    return outs


def _compress_kernel(ck_ref, cv_ref, pek_ref, pev_ref, wk1_ref, wk2_ref, wv1_ref, wv2t_ref,
                     ovl_ref, kc_ref, vct_ref):
    def hidden(c_ref, pe_ref, w1_ref):
        c = c_ref[0, 0]
        a = jnp.dot((c + pe_ref[0:1, :]).astype(jnp.bfloat16), w1_ref[0].astype(jnp.bfloat16),
                    preferred_element_type=jnp.float32)
        b = jnp.dot((c + pe_ref[1:2, :]).astype(jnp.bfloat16), w1_ref[1].astype(jnp.bfloat16),
                    preferred_element_type=jnp.float32)
        return _gelu(a + pltpu.roll(b, N_CMP_PAD - 1, 0)).astype(jnp.bfloat16)

    kc_ref[0, 0] = jnp.dot(hidden(ck_ref, pek_ref, wk1_ref), wk2_ref[...].astype(jnp.bfloat16),
                           preferred_element_type=jnp.float32).astype(jnp.bfloat16)
    vct_ref[0, 0, 0:HEAD_DIM, :] = lax.dot_general(
        wv2t_ref[...].astype(jnp.bfloat16), hidden(cv_ref, pev_ref, wv1_ref), _NT,
        preferred_element_type=jnp.float32).astype(jnp.bfloat16)
    vct_ref[0, 0, HEAD_DIM:V_ROWS, :] = jnp.ones((V_ROWS - HEAD_DIM, N_CMP_PAD), jnp.bfloat16)
    vct_ref[0, 0, V_ROWS:C_ROWS, :] = ovl_ref[...]


def _compress(kc_in, vc_in, pe_k, pe_v, wk1, wk2, wv1, wv2):
    B, _, n_str, half = kc_in.shape
    const2 = lambda b, g: (0, 0)
    const3 = lambda b, g: (0, 0, 0)
    return pl.pallas_call(
        _compress_kernel,
        grid=(B, N_KV),
        in_specs=[pl.BlockSpec((1, 1, n_str, half), lambda b, g: (b, g, 0, 0)),
                  pl.BlockSpec((1, 1, n_str, half), lambda b, g: (b, g, 0, 0)),
                  pl.BlockSpec((2, half), const2),
                  pl.BlockSpec((2, half), const2),
                  pl.BlockSpec((2, half, CMP_HIDDEN), const3),
                  pl.BlockSpec((CMP_HIDDEN, HEAD_DIM), const2),
                  pl.BlockSpec((2, half, CMP_HIDDEN), const3),
                  pl.BlockSpec((HEAD_DIM, CMP_HIDDEN), const2),
                  pl.BlockSpec((C_ROWS - V_ROWS, n_str), const2)],
        out_specs=[pl.BlockSpec((1, 1, n_str, HEAD_DIM), lambda b, g: (b, g, 0, 0)),
                   pl.BlockSpec((1, 1, C_ROWS, n_str), lambda b, g: (b, g, 0, 0))],
        out_shape=[jax.ShapeDtypeStruct((B, N_KV, n_str, HEAD_DIM), jnp.bfloat16),
                   jax.ShapeDtypeStruct((B, N_KV, C_ROWS, n_str), jnp.bfloat16)],
        compiler_params=pltpu.CompilerParams(
            dimension_semantics=("arbitrary", "arbitrary"), vmem_limit_bytes=VMEM_LIMIT),
        name="compress",
    )(kc_in, vc_in, pe_k.reshape(2, half), pe_v.reshape(2, half),
      wk1.reshape(2, half, CMP_HIDDEN), wk2, wv1.reshape(2, half, CMP_HIDDEN), wv2.T,
      _overlap_t(n_str * S_CMP))


def _overlap_t(S):
    n_slc = S // L_SLC
    c_start = np.arange(N_CMP_PAD) * S_CMP
    s_start = np.arange(n_slc) * L_SLC
    ov = np.clip(np.minimum(c_start[:, None] + L_CMP, s_start[None, :] + L_SLC)
                 - np.maximum(c_start[:, None], s_start[None, :]), 0, None).astype(np.float32) / L_CMP
    ov[(S - L_CMP) // S_CMP + 1:, :] = 0.0
    return jnp.asarray(ov.T, dtype=jnp.bfloat16)


def _select_top(score, k, tri):
    w = score
    cnt = jnp.zeros((1, TQ), jnp.float32)
    thr = jnp.zeros((1, TQ), jnp.float32)
    n_gt = jnp.zeros((1, TQ), jnp.float32)
    for _ in range(N_FREE):
        mx = jnp.max(w, axis=0, keepdims=True)
        eq = w == mx
        c = jnp.sum(jnp.where(eq, 1.0, 0.0), axis=0, keepdims=True)
        cross = (cnt < k) & (cnt + c >= k)
        thr = jnp.where(cross, mx, thr)
        n_gt = jnp.where(cross, cnt, n_gt)
        cnt = cnt + c
        w = jnp.where(eq, -jnp.inf, w)
    at_thr = score == thr
    earlier = jnp.dot(tri, jnp.where(at_thr, 1.0, 0.0).astype(jnp.bfloat16),
                      preferred_element_type=jnp.float32)
    return (score > thr) | (at_thr & (earlier + n_gt < k))


def _nsa_attn_kernel(qt_ref, kc_ref, vc_ref, tabc_ref, tri_ref, ks_ref, kw_ref, vst_ref, vwt_ref,
                     near_ref, edge_ref, gt_ref, o_ref,
                     m_ref, acc_ref, mw_ref, accw_ref, s_ref, tmax_ref, sel_ref, oc_ref):
    i = pl.program_id(1)
    zeros64 = jnp.zeros((HEAD_DIM, TQ), jnp.bfloat16)
    off1 = jnp.where(i >= 1, 0.0, MASK_NEG)
    off2 = jnp.where(i >= 2, 0.0, MASK_NEG)
    j1 = jnp.maximum(i - 1, 0)
    j2 = jnp.maximum(i - 2, 0)
    dot = functools.partial(jnp.dot, preferred_element_type=jnp.float32)

    def q_window(h):
        qh = qt_ref[0, h * HEAD_DIM:(h + 1) * HEAD_DIM, :]
        return jnp.concatenate([qh, zeros64] if h // REP == 0 else [zeros64, qh], axis=0)

    def q_selected(h):
        return jnp.concatenate([q_window(h), sel_ref[h // REP]], axis=0)

    def stage_logits(slot, k_ref, j, q_of, bias_of=None):
        k = k_ref[0, pl.ds(pl.multiple_of(j * TK, TK), TK), :]
        for h in range(N_HEADS):
            s = dot(k, q_of(h))
            if bias_of is not None:
                s = s + bias_of(h)
            s_ref[slot, h] = s
            tmax_ref[slot, h:h + 1, :] = jnp.max(s, axis=0, keepdims=True)

    def accumulate(slot, v_ref, j, state, first=False, shift=None):
        ms_ref, as_ref = state
        for h in range(N_HEADS):
            t_max = tmax_ref[slot, h:h + 1, :]
            if shift is not None:
                t_max = t_max + shift
            m_new = t_max if first else jnp.maximum(ms_ref[h:h + 1, :], t_max)
            m_sub = m_new if shift is None else m_new - shift
            p = jnp.exp2(s_ref[slot, h] - m_sub).astype(jnp.bfloat16)
            pv = dot(v_ref[0, j, h // REP], p)
            if first:
                as_ref[h] = pv
            else:
                as_ref[h] = jnp.exp2(ms_ref[h:h + 1, :] - m_new) * as_ref[h] + pv
            ms_ref[h:h + 1, :] = m_new

    sel_state = (m_ref, acc_ref)
    win_state = (mw_ref, accw_ref)
    near0 = lambda h: near_ref[h, 0]
    near1 = lambda h: near_ref[h, 1]
    edge = lambda h: edge_ref[...]

    off_c = pl.multiple_of(TAB_C_ORIGIN - (TQ // S_CMP) * i, TQ // S_CMP)
    for h in range(N_HEADS):
        s = (dot(kc_ref[0, h // REP], qt_ref[0, h * HEAD_DIM:(h + 1) * HEAD_DIM, :])
             + tabc_ref[h, pl.ds(off_c, N_CMP_PAD), :])
        s_ref[0, h] = s
        tmax_ref[0, h:h + 1, :] = jnp.max(s, axis=0, keepdims=True)
    stage_logits(1, kw_ref, i, q_window, near0)

    t_row = i * TQ + lax.broadcasted_iota(jnp.int32, (1, TQ), 1)
    any_cmp = jnp.where(t_row >= L_CMP - 1, 1.0, 0.0)
    n_blk = tri_ref.shape[0]
    j_idx = lax.broadcasted_iota(jnp.int32, (n_blk, TQ), 0)
    cur = (i * TQ + lax.broadcasted_iota(jnp.int32, (n_blk, TQ), 1)) // L_SLC
    forced = (j_idx == 0) | (j_idx == cur) | (j_idx == cur - 1)
    candidate = (j_idx < cur - 1) & (j_idx > 0)
    n_free = float(N_SEL - 1) - jnp.minimum(t_row // L_SLC, 2).astype(jnp.float32)
    for g in range(N_KV):
        imp = jnp.zeros((n_blk, TQ), jnp.float32)
        for h in range(g * REP, (g + 1) * REP):
            p = jnp.exp2(s_ref[0, h] - tmax_ref[0, h:h + 1, :]).astype(jnp.bfloat16)
            a = dot(vc_ref[0, g], p)
            inv = any_cmp / a[HEAD_DIM:HEAD_DIM + 1, :]
            oc_ref[h * HEAD_DIM:(h + 1) * HEAD_DIM, :] = a[0:HEAD_DIM, :] * inv
            imp = imp + a[V_ROWS:V_ROWS + n_blk, :] * inv
        free = _select_top(jnp.where(candidate, imp, MASK_NEG), n_free, tri_ref[...])
        sel = forced | (candidate & free)
        sel_ref[g, 0:n_blk, :] = jnp.where(sel, 0.0, MASK_NEG).astype(jnp.bfloat16)
        sel_ref[g, n_blk:2 * n_blk, :] = jnp.zeros((n_blk, TQ), jnp.bfloat16)

    accumulate(1, vwt_ref, i, win_state, first=True)
    stage_logits(0, ks_ref, i, q_selected, near0)
    stage_logits(1, kw_ref, j1, q_window, near1)
    accumulate(0, vst_ref, i, sel_state, first=True)
    stage_logits(0, ks_ref, j1, q_selected, near1)
    accumulate(1, vwt_ref, j1, win_state, shift=off1)
    stage_logits(1, kw_ref, j2, q_window, edge)
    accumulate(0, vst_ref, j1, sel_state, shift=off1)

    n_far = j1
    last = jnp.maximum(n_far - 1, 0)
    stage_logits(0, ks_ref, 0, q_selected)
    accumulate(1, vwt_ref, j2, win_state, shift=off2)

    def far_pair(t):
        stage_logits(1, ks_ref, t + 1, q_selected)
        accumulate(0, vst_ref, t, sel_state)
        stage_logits(0, ks_ref, jnp.minimum(t + 2, last), q_selected)
        accumulate(1, vst_ref, t + 1, sel_state)

    def far_quad(c, carry):
        far_pair(4 * c)
        far_pair(4 * c + 2)
        return carry

    lax.fori_loop(0, n_far // 4, far_quad, 0)

    @pl.when(n_far % 4 >= 2)
    def _():
        far_pair(n_far // 4 * 4)

    @pl.when(n_far % 2 == 1)
    def _():
        accumulate(0, vst_ref, last, sel_state)

    for pair in range(N_HEADS // 2):
        outs = []
        for h in (2 * pair, 2 * pair + 1):
            rows = slice(h * HEAD_DIM, (h + 1) * HEAD_DIM)
            o_s = acc_ref[h, 0:HEAD_DIM, :] / acc_ref[h, HEAD_DIM:HEAD_DIM + 1, :]
            o_w = accw_ref[h, 0:HEAD_DIM, :] / accw_ref[h, HEAD_DIM:HEAD_DIM + 1, :]
            outs.append(gt_ref[0, 3 * h:3 * h + 1, :] * oc_ref[rows, :]
                        + gt_ref[0, 3 * h + 1:3 * h + 2, :] * o_s
                        + gt_ref[0, 3 * h + 2:3 * h + 3, :] * o_w)
        o_pair = jnp.concatenate(outs, axis=0)
        o_ref[0, :, pair * 2 * HEAD_DIM:(pair + 1) * 2 * HEAD_DIM] = o_pair.T.astype(jnp.bfloat16)


def _nsa_attn(qt, kc, vc, tab_c, ks, kw, vst, vwt, near, gt):
    B, _, S = qt.shape
    ni = S // TQ
    nt = S // TK
    n_blk = S // L_SLC
    key = np.arange(TK)[:, None]
    row = np.arange(TQ)[None, :]
    edge = jnp.asarray(np.where(2 * TK + row - key < WINDOW, 0.0, MASK_NEG).astype(np.float32))
    tri = jnp.asarray(np.tril(np.ones((n_blk, n_blk), np.float32), -1), dtype=jnp.bfloat16)
    per_batch = lambda shape: pl.BlockSpec((1,) + shape, lambda b, i: (b,) + (0,) * len(shape))
    const = lambda shape: pl.BlockSpec(shape, lambda b, i: (0,) * len(shape),
                                       pipeline_mode=pl.Buffered(1))
    return pl.pallas_call(
        _nsa_attn_kernel,
        grid=(B, ni),
        in_specs=[pl.BlockSpec((1, D_NSA, TQ), lambda b, i: (b, 0, i)),
                  per_batch((N_KV, N_CMP_PAD, HEAD_DIM)),
                  per_batch((N_KV, C_ROWS, N_CMP_PAD)),
                  const((N_HEADS, TAB_C_ROWS, TQ)),
                  const((n_blk, n_blk)),
                  per_batch((S, 2 * KV_W)),
                  per_batch((S, KV_W)),
                  per_batch((nt, N_KV, V_ROWS, TK)),
                  per_batch((nt, N_KV, V_ROWS, TK)),
                  const((N_HEADS, 2, TK, TQ)),
                  const((TK, TQ)),
                  pl.BlockSpec((1, GATE_ROWS, TQ), lambda b, i: (b, 0, i))],
        out_specs=pl.BlockSpec((1, TQ, D_NSA), lambda b, i: (b, i, 0)),
        out_shape=jax.ShapeDtypeStruct((B, S, D_NSA), jnp.bfloat16),
        scratch_shapes=[pltpu.VMEM((N_HEADS, TQ), jnp.float32),
                        pltpu.VMEM((N_HEADS, V_ROWS, TQ), jnp.float32),
                        pltpu.VMEM((N_HEADS, TQ), jnp.float32),
                        pltpu.VMEM((N_HEADS, V_ROWS, TQ), jnp.float32),
                        pltpu.VMEM((2, N_HEADS, TK, TQ), jnp.float32),
                        pltpu.VMEM((2, N_HEADS, TQ), jnp.float32),
                        pltpu.VMEM((N_KV, 2 * n_blk, TQ), jnp.bfloat16),
                        pltpu.VMEM((D_NSA, TQ), jnp.float32)],
        compiler_params=pltpu.CompilerParams(
            dimension_semantics=("arbitrary", "arbitrary"), vmem_limit_bytes=VMEM_LIMIT),
        name="nsa_attn",
    )(qt, kc, vc, tab_c, tri, ks, kw, vst, vwt, near, edge, gt)


def _out_ffn_kernel(on_ref, oc_ref, x_ref, wo_ref, gpost_ref, gffn_ref, wup_ref, cw_ref, wdn_ref,
                    g_ref, o_ref, carry_ref, act_ref):
    i = pl.program_id(1)
    tm = x_ref.shape[1]

    @pl.when(i == 0)
    def _():
        carry_ref[...] = jnp.zeros_like(carry_ref)

    y = jnp.dot(jnp.concatenate([on_ref[0], oc_ref[0]], axis=1), wo_ref[...],
                preferred_element_type=jnp.float32)
    x1 = x_ref[0] + _rms(y, gpost_ref[...])
    hb = _rms(x1, gffn_ref[...]).astype(jnp.bfloat16)

    row = lax.broadcasted_iota(jnp.int32, (tm, FF_CHUNK), 0)
    is0 = row == 0
    is1 = row == 1

    def conv(z, c0):
        cols = slice(c0, c0 + FF_CHUNK)
        prev1 = carry_ref[SUBLANES - 1:SUBLANES, cols]
        prev2 = carry_ref[SUBLANES - 2:SUBLANES - 1, cols]
        z1 = jnp.where(is0, prev1, pltpu.roll(z, 1, 0))
        z2 = jnp.where(is0, prev2, jnp.where(is1, prev1, pltpu.roll(z, 2, 0)))
        carry_ref[:, cols] = z[tm - SUBLANES:tm, :]
        return cw_ref[0:1, cols] * z2 + cw_ref[1:2, cols] * z1 + cw_ref[2:3, cols] * z

    for c in range(D_FF // FF_CHUNK):
        g0 = c * FF_CHUNK
        u0 = D_FF + c * FF_CHUNK
        zg = jnp.dot(hb, wup_ref[:, g0:g0 + FF_CHUNK], preferred_element_type=jnp.float32)
        zu = jnp.dot(hb, wup_ref[:, u0:u0 + FF_CHUNK], preferred_element_type=jnp.float32)
        act = _gelu(conv(zg, g0)) * conv(zu, u0)
        act_ref[:, g0:g0 + FF_CHUNK] = act.astype(jnp.bfloat16)
    y2 = jnp.dot(act_ref[...], wdn_ref[...], preferred_element_type=jnp.float32)
    o_ref[0] = x1 + _rms(y2, g_ref[...])


def _out_ffn(o_nsa, o_conv, x, w_out, g_post, g_ffn, w_up, conv_w, w_down, g_ffn_post):
    B, S, D = x.shape
    tm = TM_FFN
    const = lambda b, i: (0, 0)
    single = pl.Buffered(1)
    row_tile = lambda width: pl.BlockSpec((1, tm, width), lambda b, i: (b, i, 0))
    weight = lambda shape: pl.BlockSpec(shape, const, pipeline_mode=single)
    return pl.pallas_call(
        _out_ffn_kernel,
        grid=(B, S // tm),
        in_specs=[row_tile(D_NSA), row_tile(D_CONV), row_tile(D),
                  weight((D, D)), weight((1, D)), weight((1, D)),
                  weight((D, 2 * D_FF)), weight((3, 2 * D_FF)), weight((D_FF, D)), weight((1, D))],
        out_specs=row_tile(D),
        out_shape=jax.ShapeDtypeStruct((B, S, D), jnp.float32),
        scratch_shapes=[pltpu.VMEM((SUBLANES, 2 * D_FF), jnp.float32),
                        pltpu.VMEM((tm, D_FF), jnp.bfloat16)],
        compiler_params=pltpu.CompilerParams(
            dimension_semantics=("arbitrary", "arbitrary"), vmem_limit_bytes=VMEM_LIMIT),
        name="out_ffn",
    )(o_nsa, o_conv, x, w_out.astype(jnp.bfloat16), g_post.reshape(1, D), g_ffn.reshape(1, D),
      w_up.astype(jnp.bfloat16), conv_w, w_down.astype(jnp.bfloat16), g_ffn_post.reshape(1, D))


def kernel(x, norm_mix_pre, norm_mix_post, norm_ffn_pre, norm_ffn_post, w_in, pe_cmp_k, pe_cmp_v,
           w_cmp_k1, w_cmp_k2, w_cmp_v1, w_cmp_v2, rel_bias, conv_mix_w, w_out, w_ffn_up,
           ffn_conv_w, w_ffn_down):
    B, S, D = x.shape
    assert (S, D) == (4096, D_MODEL) and norm_mix_pre.shape[0] == 1
    tab_c, near = _bias_tables(rel_bias)
    for l in range(norm_mix_pre.shape[0]):
        qt, kc_in, vc_in, ks, kw, vst, vwt, gt, o_conv = _in_proj(
            x, norm_mix_pre[l], w_in[l], conv_mix_w[l])
        kc, vc = _compress(kc_in, vc_in, pe_cmp_k[l], pe_cmp_v[l], w_cmp_k1[l], w_cmp_k2[l],
                           w_cmp_v1[l], w_cmp_v2[l])
        o_nsa = _nsa_attn(qt, kc, vc, tab_c, ks, kw, vst, vwt, near, gt)
        x = _out_ffn(o_nsa, o_conv, x, w_out[l], norm_mix_post[l], norm_ffn_pre[l],
                     w_ffn_up[l], ffn_conv_w[l], w_ffn_down[l], norm_ffn_post[l])
    return x
```

```python
import functools
import math

import numpy as np
import jax
import jax.numpy as jnp
from jax import lax
from jax.experimental import pallas as pl
from jax.experimental.pallas import tpu as pltpu

D_MODEL = 1024
D_NSA = 512
D_CONV = 512
HEAD_DIM = 64
N_HEADS = 8
N_KV = 2
REP = 4
KV_W = 128
N_GATES = 24
L_CMP = 32
S_CMP = 16
L_SLC = 64
N_SEL = 16
WINDOW = 512
CMP_HIDDEN = 256
D_FF = 2816
N_BUCKETS = 32
MAX_DIST = 128
RMS_EPS = 1e-6
MASK_NEG = -1e30
N_FREE = N_SEL - 3
LOG2E = math.log2(math.e)

LANES = 128
SUBLANES = 8
BF16_ROWS = 2 * SUBLANES
GATE_ROWS = -(-N_GATES // BF16_ROWS) * BF16_ROWS
T_COLS = -(-(D_NSA + 2 * KV_W + GATE_ROWS) // LANES) * LANES
TQ = 256
TK = 256
V_ROWS = 80
C_ROWS = V_ROWS + 64
N_CMP_PAD = 256
TAB_C_ROWS = 2 * N_CMP_PAD
TAB_C_ORIGIN = N_CMP_PAD - TQ // S_CMP
TM_IN = 1024
TM_FFN = 1024
FF_CHUNK = 256
VMEM_LIMIT = 56 * 1024 * 1024

_NT = (((1,), (1,)), ((), ()))


def _bucket_np(d):
    max_exact = N_BUCKETS // 2
    d = np.maximum(d, 0)
    df = np.maximum(d, 1).astype(np.float32)
    large = max_exact + (np.log(df / max_exact) / math.log(MAX_DIST / max_exact)
                         * (N_BUCKETS - max_exact)).astype(np.int32)
    return np.where(d < max_exact, d, np.minimum(large, N_BUCKETS - 1)).astype(np.int32)


def _bucket_or_mask(d):
    return np.where(d >= 0, _bucket_np(d), -1).astype(np.int32)


def _rms(x, g):
    return x * lax.rsqrt(jnp.mean(x * x, axis=-1, keepdims=True) + RMS_EPS) * g


def _gelu(x):
    return jax.nn.gelu(x, approximate=True)


def _band_box(idx):
    rows, cols = np.nonzero((idx >= 0) & (idx < N_BUCKETS - 1))
    r0, r1, c0, c1 = int(rows.min()), int(rows.max()) + 1, int(cols.min()), int(cols.max()) + 1
    return (r0 // SUBLANES * SUBLANES, -(-r1 // SUBLANES) * SUBLANES,
            c0 // LANES * LANES, -(-c1 // LANES) * LANES)


def _bias_tables_kernel(boxes, rb_ref, idx_c_ref, idx_n_ref, tab_c_ref, tab_n_ref):
    h = pl.program_id(0)
    last = rb_ref[h, N_BUCKETS - 1]

    def fill(idx, out, box):
        out[...] = jnp.where(idx < 0, MASK_NEG, 0.0)
        r0, r1, c0, c1 = box
        sub = idx[r0:r1, c0:c1]
        val = jnp.zeros(sub.shape, jnp.float32)
        for b in range(N_BUCKETS - 1):
            val = jnp.where(sub == b, rb_ref[h, b] - last, val)
        out[r0:r1, c0:c1] = jnp.where(sub < 0, MASK_NEG, val * LOG2E)

    fill(idx_c_ref[...], tab_c_ref.at[0], boxes[0])
    for k in range(idx_n_ref.shape[0]):
        fill(idx_n_ref[k], tab_n_ref.at[0, k], boxes[1 + k])


def _bias_tables(rel_bias):
    u = np.arange(TAB_C_ROWS)[:, None] - TAB_C_ORIGIN
    row = np.arange(TQ)[None, :]
    d_c = row - S_CMP * u - (L_CMP - 1)
    idx_c = _bucket_or_mask(d_c)
    key = np.arange(TK)[:, None]
    idx_n = np.stack([_bucket_or_mask(dd + row - key) for dd in (0, TK)])
    boxes = tuple(_band_box(t) for t in (idx_c, *idx_n))
    return pl.pallas_call(
        functools.partial(_bias_tables_kernel, boxes),
        grid=(N_HEADS,),
        in_specs=[pl.BlockSpec(memory_space=pltpu.SMEM),
                  pl.BlockSpec((TAB_C_ROWS, TQ), lambda h: (0, 0)),
                  pl.BlockSpec((2, TK, TQ), lambda h: (0, 0, 0))],
        out_specs=[pl.BlockSpec((1, TAB_C_ROWS, TQ), lambda h: (h, 0, 0)),
                   pl.BlockSpec((1, 2, TK, TQ), lambda h: (h, 0, 0, 0))],
        out_shape=[jax.ShapeDtypeStruct((N_HEADS, TAB_C_ROWS, TQ), jnp.float32),
                   jax.ShapeDtypeStruct((N_HEADS, 2, TK, TQ), jnp.float32)],
        name="bias_tables",
    )(rel_bias, jnp.asarray(idx_c), jnp.asarray(idx_n))


def _in_proj_kernel(x_ref, g_ref, w_ref, cw_ref,
                    qt_ref, kc_ref, vc_ref, ks_ref, kw_ref, vst_ref, vwt_ref, gt_ref,
                    oconv_ref, carry_ref, stage_ref):
    i = pl.program_id(1)
    tm = x_ref.shape[1]

    @pl.when(i == 0)
    def _():
        carry_ref[...] = jnp.zeros_like(carry_ref)

    hb = _rms(x_ref[0], g_ref[...]).astype(jnp.bfloat16)

    tt = lax.dot_general(w_ref[:, 0:T_COLS], hb, (((0,), (1,)), ((), ())),
                         preferred_element_type=jnp.float32)
    qt_ref[0] = tt[0:D_NSA].astype(jnp.bfloat16)
    vt = tt[D_NSA:D_NSA + 2 * KV_W]
    ones = jnp.ones((V_ROWS - HEAD_DIM, TK), jnp.bfloat16)
    for t in range(tm // TK):
        for g in range(N_KV):
            for out_ref, r0 in ((vst_ref, g * HEAD_DIM), (vwt_ref, KV_W + g * HEAD_DIM)):
                out_ref[0, t, g, 0:HEAD_DIM, :] = vt[r0:r0 + HEAD_DIM,
                                                      t * TK:(t + 1) * TK].astype(jnp.bfloat16)
                out_ref[0, t, g, HEAD_DIM:V_ROWS, :] = ones
    gt_ref[0] = jax.nn.sigmoid(tt[D_NSA + 2 * KV_W:D_NSA + 2 * KV_W + GATE_ROWS])

    n0 = T_COLS
    kv_c = jnp.dot(hb, w_ref[:, n0:n0 + 2 * KV_W], preferred_element_type=jnp.float32)
    for a, out_ref in enumerate((kc_ref, vc_ref)):
        stage_ref[a] = kv_c[:, a * KV_W:(a + 1) * KV_W]
        for l in range(S_CMP):
            rows = stage_ref[a, pl.ds(l, tm // S_CMP, stride=S_CMP), :]
            for g in range(N_KV):
                out_ref[0, g, :, l * HEAD_DIM:(l + 1) * HEAD_DIM] = rows[
                    :, g * HEAD_DIM:(g + 1) * HEAD_DIM]
    k_sw = jnp.dot(hb, w_ref[:, n0 + 2 * KV_W:n0 + 4 * KV_W],
                   preferred_element_type=jnp.float32)
    blk = (i * tm + lax.broadcasted_iota(jnp.int32, (tm, KV_W), 0)) // L_SLC
    col = lax.broadcasted_iota(jnp.int32, (tm, KV_W), 1)
    onehot = jnp.where(blk == col, 1.0, 0.0)
    ks_ref[0, :, 0:KV_W] = k_sw[:, 0:KV_W].astype(jnp.bfloat16)
    ks_ref[0, :, KV_W:2 * KV_W] = onehot.astype(jnp.bfloat16)
    kw_ref[0] = k_sw[:, KV_W:2 * KV_W].astype(jnp.bfloat16)

    c0 = n0 + 4 * KV_W
    bg = jnp.dot(hb, w_ref[:, c0:c0 + D_CONV], preferred_element_type=jnp.float32)
    cg = jnp.dot(hb, w_ref[:, c0 + D_CONV:c0 + 2 * D_CONV], preferred_element_type=jnp.float32)
    xt = jnp.dot(hb, w_ref[:, c0 + 2 * D_CONV:c0 + 3 * D_CONV],
                 preferred_element_type=jnp.float32)
    z = cg * xt
    prev1 = carry_ref[SUBLANES - 1:SUBLANES, :]
    prev2 = carry_ref[SUBLANES - 2:SUBLANES - 1, :]
    row = lax.broadcasted_iota(jnp.int32, z.shape, 0)
    z1 = jnp.where(row == 0, prev1, pltpu.roll(z, 1, 0))
    z2 = jnp.where(row == 0, prev2, jnp.where(row == 1, prev1, pltpu.roll(z, 2, 0)))
    y = cw_ref[0:1, :] * z2 + cw_ref[1:2, :] * z1 + cw_ref[2:3, :] * z
    oconv_ref[0] = (bg * y).astype(jnp.bfloat16)
    carry_ref[...] = z[tm - SUBLANES:tm, :]


def _in_proj(x, g_pre, w_in, conv_w):
    B, S, D = x.shape
    tm = TM_IN
    widths = dict(q=D_NSA, k_c=KV_W, v_c=KV_W, k_s=KV_W, v_s=KV_W, k_w=KV_W, v_w=KV_W,
                  gate=N_GATES, b=D_CONV, c=D_CONV, x=D_CONV)
    assert w_in.shape[1] == sum(widths.values())
    col, cols = 0, {}
    for name, width in widths.items():
        cols[name] = w_in[:, col:col + width]
        col += width
    t_used = D_NSA + 2 * KV_W + N_GATES
    w_all = jnp.concatenate(
        [cols['q'] * (HEAD_DIM ** -0.5 * LOG2E), cols['v_s'], cols['v_w'], cols['gate'],
         jnp.zeros((D, T_COLS - t_used), w_in.dtype)]
        + [cols[n] for n in ('k_c', 'v_c', 'k_s', 'k_w', 'b', 'c', 'x')],
        axis=1).astype(jnp.bfloat16)
    all_w = T_COLS + 4 * KV_W + 3 * D_CONV
    nt = S // TK
    const = lambda b, i: (0, 0)
    outs = pl.pallas_call(
        _in_proj_kernel,
        grid=(B, S // tm),
        in_specs=[pl.BlockSpec((1, tm, D), lambda b, i: (b, i, 0)),
                  pl.BlockSpec((1, D), const),
                  pl.BlockSpec((D, all_w), const),
                  pl.BlockSpec((3, D_CONV), const)],
        out_specs=[pl.BlockSpec((1, D_NSA, tm), lambda b, i: (b, 0, i)),
                   pl.BlockSpec((1, N_KV, tm // S_CMP, S_CMP * HEAD_DIM), lambda b, i: (b, 0, i, 0)),
                   pl.BlockSpec((1, N_KV, tm // S_CMP, S_CMP * HEAD_DIM), lambda b, i: (b, 0, i, 0)),
                   pl.BlockSpec((1, tm, 2 * KV_W), lambda b, i: (b, i, 0)),
                   pl.BlockSpec((1, tm, KV_W), lambda b, i: (b, i, 0)),
                   pl.BlockSpec((1, tm // TK, N_KV, V_ROWS, TK), lambda b, i: (b, i, 0, 0, 0)),
                   pl.BlockSpec((1, tm // TK, N_KV, V_ROWS, TK), lambda b, i: (b, i, 0, 0, 0)),
                   pl.BlockSpec((1, GATE_ROWS, tm), lambda b, i: (b, 0, i)),
                   pl.BlockSpec((1, tm, D_CONV), lambda b, i: (b, i, 0))],
        out_shape=[jax.ShapeDtypeStruct((B, D_NSA, S), jnp.bfloat16),
                   jax.ShapeDtypeStruct((B, N_KV, S // S_CMP, S_CMP * HEAD_DIM), jnp.float32),
                   jax.ShapeDtypeStruct((B, N_KV, S // S_CMP, S_CMP * HEAD_DIM), jnp.float32),
                   jax.ShapeDtypeStruct((B, S, 2 * KV_W), jnp.bfloat16),
                   jax.ShapeDtypeStruct((B, S, KV_W), jnp.bfloat16),
                   jax.ShapeDtypeStruct((B, nt, N_KV, V_ROWS, TK), jnp.bfloat16),
                   jax.ShapeDtypeStruct((B, nt, N_KV, V_ROWS, TK), jnp.bfloat16),
                   jax.ShapeDtypeStruct((B, GATE_ROWS, S), jnp.float32),
                   jax.ShapeDtypeStruct((B, S, D_CONV), jnp.bfloat16)],
        scratch_shapes=[pltpu.VMEM((SUBLANES, D_CONV), jnp.float32),
                        pltpu.VMEM((2, tm, KV_W), jnp.float32)],
        compiler_params=pltpu.CompilerParams(
            dimension_semantics=("arbitrary", "arbitrary"), vmem_limit_bytes=VMEM_LIMIT),
        name="in_proj",
    )(x, g_pre.reshape(1, D), w_all, conv_w)
    return outs


def _compress_kernel(ck_ref, cv_ref, pek_ref, pev_ref, wk1_ref, wk2_ref, wv1_ref, wv2t_ref,
                     ovl_ref, kc_ref, vct_ref):
    def hidden(c_ref, pe_ref, w1_ref):
        c = c_ref[0, 0]
        a = jnp.dot((c + pe_ref[0:1, :]).astype(jnp.bfloat16), w1_ref[0].astype(jnp.bfloat16),
                    preferred_element_type=jnp.float32)
        b = jnp.dot((c + pe_ref[1:2, :]).astype(jnp.bfloat16), w1_ref[1].astype(jnp.bfloat16),
                    preferred_element_type=jnp.float32)
        return _gelu(a + pltpu.roll(b, N_CMP_PAD - 1, 0)).astype(jnp.bfloat16)

    kc_ref[0, 0] = jnp.dot(hidden(ck_ref, pek_ref, wk1_ref), wk2_ref[...].astype(jnp.bfloat16),
                           preferred_element_type=jnp.float32).astype(jnp.bfloat16)
    vct_ref[0, 0, 0:HEAD_DIM, :] = lax.dot_general(
        wv2t_ref[...].astype(jnp.bfloat16), hidden(cv_ref, pev_ref, wv1_ref), _NT,
        preferred_element_type=jnp.float32).astype(jnp.bfloat16)
    vct_ref[0, 0, HEAD_DIM:V_ROWS, :] = jnp.ones((V_ROWS - HEAD_DIM, N_CMP_PAD), jnp.bfloat16)
    vct_ref[0, 0, V_ROWS:C_ROWS, :] = ovl_ref[...]


def _compress(kc_in, vc_in, pe_k, pe_v, wk1, wk2, wv1, wv2):
    B, _, n_str, half = kc_in.shape
    const2 = lambda b, g: (0, 0)
    const3 = lambda b, g: (0, 0, 0)
    return pl.pallas_call(
        _compress_kernel,
        grid=(B, N_KV),
        in_specs=[pl.BlockSpec((1, 1, n_str, half), lambda b, g: (b, g, 0, 0)),
                  pl.BlockSpec((1, 1, n_str, half), lambda b, g: (b, g, 0, 0)),
                  pl.BlockSpec((2, half), const2),
                  pl.BlockSpec((2, half), const2),
                  pl.BlockSpec((2, half, CMP_HIDDEN), const3),
                  pl.BlockSpec((CMP_HIDDEN, HEAD_DIM), const2),
                  pl.BlockSpec((2, half, CMP_HIDDEN), const3),
                  pl.BlockSpec((HEAD_DIM, CMP_HIDDEN), const2),
                  pl.BlockSpec((C_ROWS - V_ROWS, n_str), const2)],
        out_specs=[pl.BlockSpec((1, 1, n_str, HEAD_DIM), lambda b, g: (b, g, 0, 0)),
                   pl.BlockSpec((1, 1, C_ROWS, n_str), lambda b, g: (b, g, 0, 0))],
        out_shape=[jax.ShapeDtypeStruct((B, N_KV, n_str, HEAD_DIM), jnp.bfloat16),
                   jax.ShapeDtypeStruct((B, N_KV, C_ROWS, n_str), jnp.bfloat16)],
        compiler_params=pltpu.CompilerParams(
            dimension_semantics=("arbitrary", "arbitrary"), vmem_limit_bytes=VMEM_LIMIT),
        name="compress",
    )(kc_in, vc_in, pe_k.reshape(2, half), pe_v.reshape(2, half),
      wk1.reshape(2, half, CMP_HIDDEN), wk2, wv1.reshape(2, half, CMP_HIDDEN), wv2.T,
      _overlap_t(n_str * S_CMP))


def _overlap_t(S):
    n_slc = S // L_SLC
    c_start = np.arange(N_CMP_PAD) * S_CMP
    s_start = np.arange(n_slc) * L_SLC
    ov = np.clip(np.minimum(c_start[:, None] + L_CMP, s_start[None, :] + L_SLC)
                 - np.maximum(c_start[:, None], s_start[None, :]), 0, None).astype(np.float32) / L_CMP
    ov[(S - L_CMP) // S_CMP + 1:, :] = 0.0
    return jnp.asarray(ov.T, dtype=jnp.bfloat16)


def _select_top(score, k, tri):
    w = score
    cnt = jnp.zeros((1, TQ), jnp.float32)
    thr = jnp.zeros((1, TQ), jnp.float32)
    n_gt = jnp.zeros((1, TQ), jnp.float32)
    for _ in range(N_FREE):
        mx = jnp.max(w, axis=0, keepdims=True)
        eq = w == mx
        c = jnp.sum(jnp.where(eq, 1.0, 0.0), axis=0, keepdims=True)
        cross = (cnt < k) & (cnt + c >= k)
        thr = jnp.where(cross, mx, thr)
        n_gt = jnp.where(cross, cnt, n_gt)
        cnt = cnt + c
        w = jnp.where(eq, -jnp.inf, w)
    at_thr = score == thr
    earlier = jnp.dot(tri, jnp.where(at_thr, 1.0, 0.0).astype(jnp.bfloat16),
                      preferred_element_type=jnp.float32)
    return (score > thr) | (at_thr & (earlier + n_gt < k))


def _nsa_attn_kernel(qt_ref, kc_ref, vc_ref, tabc_ref, tri_ref, ks_ref, kw_ref, vst_ref, vwt_ref,
                     near_ref, edge_ref, gt_ref, o_ref,
                     m_ref, acc_ref, mw_ref, accw_ref, s_ref, tmax_ref, sel_ref, oc_ref):
    i = pl.program_id(1)
    zeros64 = jnp.zeros((HEAD_DIM, TQ), jnp.bfloat16)
    off1 = jnp.where(i >= 1, 0.0, MASK_NEG)
    off2 = jnp.where(i >= 2, 0.0, MASK_NEG)
    j1 = jnp.maximum(i - 1, 0)
    j2 = jnp.maximum(i - 2, 0)
    dot = functools.partial(jnp.dot, preferred_element_type=jnp.float32)

    def q_window(h):
        qh = qt_ref[0, h * HEAD_DIM:(h + 1) * HEAD_DIM, :]
        return jnp.concatenate([qh, zeros64] if h // REP == 0 else [zeros64, qh], axis=0)

    def q_selected(h):
        return jnp.concatenate([q_window(h), sel_ref[h // REP]], axis=0)

    def stage_logits(slot, k_ref, j, q_of, bias_of=None):
        k = k_ref[0, pl.ds(pl.multiple_of(j * TK, TK), TK), :]
        for h in range(N_HEADS):
            s = dot(k, q_of(h))
            if bias_of is not None:
                s = s + bias_of(h)
            s_ref[slot, h] = s
            tmax_ref[slot, h:h + 1, :] = jnp.max(s, axis=0, keepdims=True)

    def accumulate(slot, v_ref, j, state, first=False, shift=None):
        ms_ref, as_ref = state
        for h in range(N_HEADS):
            t_max = tmax_ref[slot, h:h + 1, :]
            if shift is not None:
                t_max = t_max + shift
            m_new = t_max if first else jnp.maximum(ms_ref[h:h + 1, :], t_max)
            m_sub = m_new if shift is None else m_new - shift
            p = jnp.exp2(s_ref[slot, h] - m_sub).astype(jnp.bfloat16)
            pv = dot(v_ref[0, j, h // REP], p)
            if first:
                as_ref[h] = pv
            else:
                as_ref[h] = jnp.exp2(ms_ref[h:h + 1, :] - m_new) * as_ref[h] + pv
            ms_ref[h:h + 1, :] = m_new

    sel_state = (m_ref, acc_ref)
    win_state = (mw_ref, accw_ref)
    near0 = lambda h: near_ref[h, 0]
    near1 = lambda h: near_ref[h, 1]
    edge = lambda h: edge_ref[...]

    off_c = pl.multiple_of(TAB_C_ORIGIN - (TQ // S_CMP) * i, TQ // S_CMP)
    for h in range(N_HEADS):
        s = (dot(kc_ref[0, h // REP], qt_ref[0, h * HEAD_DIM:(h + 1) * HEAD_DIM, :])
             + tabc_ref[h, pl.ds(off_c, N_CMP_PAD), :])
        s_ref[0, h] = s
        tmax_ref[0, h:h + 1, :] = jnp.max(s, axis=0, keepdims=True)
    stage_logits(1, kw_ref, i, q_window, near0)

    t_row = i * TQ + lax.broadcasted_iota(jnp.int32, (1, TQ), 1)
    any_cmp = jnp.where(t_row >= L_CMP - 1, 1.0, 0.0)
    n_blk = tri_ref.shape[0]
    j_idx = lax.broadcasted_iota(jnp.int32, (n_blk, TQ), 0)
    cur = (i * TQ + lax.broadcasted_iota(jnp.int32, (n_blk, TQ), 1)) // L_SLC
    forced = (j_idx == 0) | (j_idx == cur) | (j_idx == cur - 1)
    candidate = (j_idx < cur - 1) & (j_idx > 0)
    n_free = float(N_SEL - 1) - jnp.minimum(t_row // L_SLC, 2).astype(jnp.float32)
    for g in range(N_KV):
        imp = jnp.zeros((n_blk, TQ), jnp.float32)
        for h in range(g * REP, (g + 1) * REP):
            p = jnp.exp2(s_ref[0, h] - tmax_ref[0, h:h + 1, :]).astype(jnp.bfloat16)
            a = dot(vc_ref[0, g], p)
            inv = any_cmp / a[HEAD_DIM:HEAD_DIM + 1, :]
            oc_ref[h * HEAD_DIM:(h + 1) * HEAD_DIM, :] = a[0:HEAD_DIM, :] * inv
            imp = imp + a[V_ROWS:V_ROWS + n_blk, :] * inv
        free = _select_top(jnp.where(candidate, imp, MASK_NEG), n_free, tri_ref[...])
        sel = forced | (candidate & free)
        sel_ref[g, 0:n_blk, :] = jnp.where(sel, 0.0, MASK_NEG).astype(jnp.bfloat16)
        sel_ref[g, n_blk:2 * n_blk, :] = jnp.zeros((n_blk, TQ), jnp.bfloat16)

    accumulate(1, vwt_ref, i, win_state, first=True)
    stage_logits(0, ks_ref, i, q_selected, near0)
    stage_logits(1, kw_ref, j1, q_window, near1)
    accumulate(0, vst_ref, i, sel_state, first=True)
    stage_logits(0, ks_ref, j1, q_selected, near1)
    accumulate(1, vwt_ref, j1, win_state, shift=off1)
    stage_logits(1, kw_ref, j2, q_window, edge)
    accumulate(0, vst_ref, j1, sel_state, shift=off1)

    n_far = j1
    last = jnp.maximum(n_far - 1, 0)
    stage_logits(0, ks_ref, 0, q_selected)
    accumulate(1, vwt_ref, j2, win_state, shift=off2)

    def far_pair(t):
        stage_logits(1, ks_ref, t + 1, q_selected)
        accumulate(0, vst_ref, t, sel_state)
        stage_logits(0, ks_ref, jnp.minimum(t + 2, last), q_selected)
        accumulate(1, vst_ref, t + 1, sel_state)

    def far_quad(c, carry):
        far_pair(4 * c)
        far_pair(4 * c + 2)
        return carry

    lax.fori_loop(0, n_far // 4, far_quad, 0)

    @pl.when(n_far % 4 >= 2)
    def _():
        far_pair(n_far // 4 * 4)

    @pl.when(n_far % 2 == 1)
    def _():
        accumulate(0, vst_ref, last, sel_state)

    for pair in range(N_HEADS // 2):
        outs = []
        for h in (2 * pair, 2 * pair + 1):
            rows = slice(h * HEAD_DIM, (h + 1) * HEAD_DIM)
            o_s = acc_ref[h, 0:HEAD_DIM, :] / acc_ref[h, HEAD_DIM:HEAD_DIM + 1, :]
            o_w = accw_ref[h, 0:HEAD_DIM, :] / accw_ref[h, HEAD_DIM:HEAD_DIM + 1, :]
            outs.append(gt_ref[0, 3 * h:3 * h + 1, :] * oc_ref[rows, :]
                        + gt_ref[0, 3 * h + 1:3 * h + 2, :] * o_s
                        + gt_ref[0, 3 * h + 2:3 * h + 3, :] * o_w)
        o_pair = jnp.concatenate(outs, axis=0)
        o_ref[0, :, pair * 2 * HEAD_DIM:(pair + 1) * 2 * HEAD_DIM] = o_pair.T.astype(jnp.bfloat16)


def _nsa_attn(qt, kc, vc, tab_c, ks, kw, vst, vwt, near, gt):
    B, _, S = qt.shape
    ni = S // TQ
    nt = S // TK
    n_blk = S // L_SLC
    key = np.arange(TK)[:, None]
    row = np.arange(TQ)[None, :]
    edge = jnp.asarray(np.where(2 * TK + row - key < WINDOW, 0.0, MASK_NEG).astype(np.float32))
    tri = jnp.asarray(np.tril(np.ones((n_blk, n_blk), np.float32), -1), dtype=jnp.bfloat16)
    per_batch = lambda shape: pl.BlockSpec((1,) + shape, lambda b, i: (b,) + (0,) * len(shape))
    const = lambda shape: pl.BlockSpec(shape, lambda b, i: (0,) * len(shape),
                                       pipeline_mode=pl.Buffered(1))
    return pl.pallas_call(
        _nsa_attn_kernel,
        grid=(B, ni),
        in_specs=[pl.BlockSpec((1, D_NSA, TQ), lambda b, i: (b, 0, i)),
                  per_batch((N_KV, N_CMP_PAD, HEAD_DIM)),
                  per_batch((N_KV, C_ROWS, N_CMP_PAD)),
                  const((N_HEADS, TAB_C_ROWS, TQ)),
                  const((n_blk, n_blk)),
                  per_batch((S, 2 * KV_W)),
                  per_batch((S, KV_W)),
                  per_batch((nt, N_KV, V_ROWS, TK)),
                  per_batch((nt, N_KV, V_ROWS, TK)),
                  const((N_HEADS, 2, TK, TQ)),
                  const((TK, TQ)),
                  pl.BlockSpec((1, GATE_ROWS, TQ), lambda b, i: (b, 0, i))],
        out_specs=pl.BlockSpec((1, TQ, D_NSA), lambda b, i: (b, i, 0)),
        out_shape=jax.ShapeDtypeStruct((B, S, D_NSA), jnp.bfloat16),
        scratch_shapes=[pltpu.VMEM((N_HEADS, TQ), jnp.float32),
                        pltpu.VMEM((N_HEADS, V_ROWS, TQ), jnp.float32),
                        pltpu.VMEM((N_HEADS, TQ), jnp.float32),
                        pltpu.VMEM((N_HEADS, V_ROWS, TQ), jnp.float32),
                        pltpu.VMEM((2, N_HEADS, TK, TQ), jnp.float32),
                        pltpu.VMEM((2, N_HEADS, TQ), jnp.float32),
                        pltpu.VMEM((N_KV, 2 * n_blk, TQ), jnp.bfloat16),
                        pltpu.VMEM((D_NSA, TQ), jnp.float32)],
        compiler_params=pltpu.CompilerParams(
            dimension_semantics=("arbitrary", "arbitrary"), vmem_limit_bytes=VMEM_LIMIT),
        name="nsa_attn",
    )(qt, kc, vc, tab_c, tri, ks, kw, vst, vwt, near, edge, gt)


def _out_ffn_kernel(on_ref, oc_ref, x_ref, wo_ref, gpost_ref, gffn_ref, wup_ref, cw_ref, wdn_ref,
                    g_ref, o_ref, carry_ref, act_ref):
    i = pl.program_id(1)
    tm = x_ref.shape[1]

    @pl.when(i == 0)
    def _():
        carry_ref[...] = jnp.zeros_like(carry_ref)

    y = jnp.dot(jnp.concatenate([on_ref[0], oc_ref[0]], axis=1), wo_ref[...],
                preferred_element_type=jnp.float32)
    x1 = x_ref[0] + _rms(y, gpost_ref[...])
    hb = _rms(x1, gffn_ref[...]).astype(jnp.bfloat16)

    row = lax.broadcasted_iota(jnp.int32, (tm, FF_CHUNK), 0)
    is0 = row == 0
    is1 = row == 1

    def conv(z, c0):
        cols = slice(c0, c0 + FF_CHUNK)
        prev1 = carry_ref[SUBLANES - 1:SUBLANES, cols]
        prev2 = carry_ref[SUBLANES - 2:SUBLANES - 1, cols]
        z1 = jnp.where(is0, prev1, pltpu.roll(z, 1, 0))
        z2 = jnp.where(is0, prev2, jnp.where(is1, prev1, pltpu.roll(z, 2, 0)))
        carry_ref[:, cols] = z[tm - SUBLANES:tm, :]
        return cw_ref[0:1, cols] * z2 + cw_ref[1:2, cols] * z1 + cw_ref[2:3, cols] * z

    for c in range(D_FF // FF_CHUNK):
        g0 = c * FF_CHUNK
        u0 = D_FF + c * FF_CHUNK
        zg = jnp.dot(hb, wup_ref[:, g0:g0 + FF_CHUNK], preferred_element_type=jnp.float32)
        zu = jnp.dot(hb, wup_ref[:, u0:u0 + FF_CHUNK], preferred_element_type=jnp.float32)
        act = _gelu(conv(zg, g0)) * conv(zu, u0)
        act_ref[:, g0:g0 + FF_CHUNK] = act.astype(jnp.bfloat16)
    y2 = jnp.dot(act_ref[...], wdn_ref[...], preferred_element_type=jnp.float32)
    o_ref[0] = x1 + _rms(y2, g_ref[...])


def _out_ffn(o_nsa, o_conv, x, w_out, g_post, g_ffn, w_up, conv_w, w_down, g_ffn_post):
    B, S, D = x.shape
    tm = TM_FFN
    const = lambda b, i: (0, 0)
    single = pl.Buffered(1)
    row_tile = lambda width: pl.BlockSpec((1, tm, width), lambda b, i: (b, i, 0))
    weight = lambda shape: pl.BlockSpec(shape, const, pipeline_mode=single)
    return pl.pallas_call(
        _out_ffn_kernel,
        grid=(B, S // tm),
        in_specs=[row_tile(D_NSA), row_tile(D_CONV), row_tile(D),
                  weight((D, D)), weight((1, D)), weight((1, D)),
                  weight((D, 2 * D_FF)), weight((3, 2 * D_FF)), weight((D_FF, D)), weight((1, D))],
        out_specs=row_tile(D),
        out_shape=jax.ShapeDtypeStruct((B, S, D), jnp.float32),
        scratch_shapes=[pltpu.VMEM((SUBLANES, 2 * D_FF), jnp.float32),
                        pltpu.VMEM((tm, D_FF), jnp.bfloat16)],
        compiler_params=pltpu.CompilerParams(
            dimension_semantics=("arbitrary", "arbitrary"), vmem_limit_bytes=VMEM_LIMIT),
        name="out_ffn",
    )(o_nsa, o_conv, x, w_out.astype(jnp.bfloat16), g_post.reshape(1, D), g_ffn.reshape(1, D),
      w_up.astype(jnp.bfloat16), conv_w, w_down.astype(jnp.bfloat16), g_ffn_post.reshape(1, D))


def kernel(x, norm_mix_pre, norm_mix_post, norm_ffn_pre, norm_ffn_post, w_in, pe_cmp_k, pe_cmp_v,
           w_cmp_k1, w_cmp_k2, w_cmp_v1, w_cmp_v2, rel_bias, conv_mix_w, w_out, w_ffn_up,
           ffn_conv_w, w_ffn_down):
    B, S, D = x.shape
    assert (S, D) == (4096, D_MODEL) and norm_mix_pre.shape[0] == 1
    tab_c, near = _bias_tables(rel_bias)
    for l in range(norm_mix_pre.shape[0]):
        qt, kc_in, vc_in, ks, kw, vst, vwt, gt, o_conv = _in_proj(
            x, norm_mix_pre[l], w_in[l], conv_mix_w[l])
        kc, vc = _compress(kc_in, vc_in, pe_cmp_k[l], pe_cmp_v[l], w_cmp_k1[l], w_cmp_k2[l],
                           w_cmp_v1[l], w_cmp_v2[l])
        o_nsa = _nsa_attn(qt, kc, vc, tab_c, ks, kw, vst, vwt, near, gt)
        x = _out_ffn(o_nsa, o_conv, x, w_out[l], norm_mix_post[l], norm_ffn_pre[l],
                     w_ffn_up[l], ffn_conv_w[l], w_ffn_down[l], norm_ffn_post[l])
    return x
```

```python
import functools
import math

import numpy as np
import jax
import jax.numpy as jnp
from jax import lax
from jax.experimental import pallas as pl
from jax.experimental.pallas import tpu as pltpu

D_MODEL = 1024
D_NSA = 512
D_CONV = 512
HEAD_DIM = 64
N_HEADS = 8
N_KV = 2
REP = 4
KV_W = 128
N_GATES = 24
L_CMP = 32
S_CMP = 16
L_SLC = 64
N_SEL = 16
WINDOW = 512
CMP_HIDDEN = 256
D_FF = 2816
N_BUCKETS = 32
MAX_DIST = 128
RMS_EPS = 1e-6
MASK_NEG = -1e30
N_FREE = N_SEL - 3
LOG2E = math.log2(math.e)

LANES = 128
SUBLANES = 8
BF16_ROWS = 2 * SUBLANES
GATE_ROWS = -(-N_GATES // BF16_ROWS) * BF16_ROWS
TQ = 256
TK = 256
V_ROWS = 80
C_ROWS = V_ROWS + 64
N_CMP_PAD = 256
TAB_C_ROWS = 2 * N_CMP_PAD
TAB_C_ORIGIN = N_CMP_PAD - TQ // S_CMP
TM_IN = 1024
TM_FFN = 1024
FF_CHUNK = 256
VMEM_LIMIT = 56 * 1024 * 1024

_NT = (((1,), (1,)), ((), ()))


def _bucket_np(d):
    max_exact = N_BUCKETS // 2
    d = np.maximum(d, 0)
    df = np.maximum(d, 1).astype(np.float32)
    large = max_exact + (np.log(df / max_exact) / math.log(MAX_DIST / max_exact)
                         * (N_BUCKETS - max_exact)).astype(np.int32)
    return np.where(d < max_exact, d, np.minimum(large, N_BUCKETS - 1)).astype(np.int32)


def _bucket_or_mask(d):
    return np.where(d >= 0, _bucket_np(d), -1).astype(np.int32)


def _rms(x, g):
    return x * lax.rsqrt(jnp.mean(x * x, axis=-1, keepdims=True) + RMS_EPS) * g


def _gelu(x):
    return jax.nn.gelu(x, approximate=True)


def _band_box(idx):
    rows, cols = np.nonzero((idx >= 0) & (idx < N_BUCKETS - 1))
    r0, r1, c0, c1 = int(rows.min()), int(rows.max()) + 1, int(cols.min()), int(cols.max()) + 1
    return (r0 // SUBLANES * SUBLANES, -(-r1 // SUBLANES) * SUBLANES,
            c0 // LANES * LANES, -(-c1 // LANES) * LANES)


def _bias_tables_kernel(boxes, rb_ref, idx_c_ref, idx_n_ref, tab_c_ref, tab_n_ref):
    h = pl.program_id(0)
    last = rb_ref[h, N_BUCKETS - 1]

    def fill(idx, out, box):
        out[...] = jnp.where(idx < 0, MASK_NEG, 0.0)
        r0, r1, c0, c1 = box
        sub = idx[r0:r1, c0:c1]
        val = jnp.zeros(sub.shape, jnp.float32)
        for b in range(N_BUCKETS - 1):
            val = jnp.where(sub == b, rb_ref[h, b] - last, val)
        out[r0:r1, c0:c1] = jnp.where(sub < 0, MASK_NEG, val * LOG2E)

    fill(idx_c_ref[...], tab_c_ref.at[0], boxes[0])
    for k in range(idx_n_ref.shape[0]):
        fill(idx_n_ref[k], tab_n_ref.at[0, k], boxes[1 + k])


def _bias_tables(rel_bias):
    u = np.arange(TAB_C_ROWS)[:, None] - TAB_C_ORIGIN
    row = np.arange(TQ)[None, :]
    d_c = row - S_CMP * u - (L_CMP - 1)
    idx_c = _bucket_or_mask(d_c)
    key = np.arange(TK)[:, None]
    idx_n = np.stack([_bucket_or_mask(dd + row - key) for dd in (0, TK)])
    boxes = tuple(_band_box(t) for t in (idx_c, *idx_n))
    return pl.pallas_call(
        functools.partial(_bias_tables_kernel, boxes),
        grid=(N_HEADS,),
        in_specs=[pl.BlockSpec(memory_space=pltpu.SMEM),
                  pl.BlockSpec((TAB_C_ROWS, TQ), lambda h: (0, 0)),
                  pl.BlockSpec((2, TK, TQ), lambda h: (0, 0, 0))],
        out_specs=[pl.BlockSpec((1, TAB_C_ROWS, TQ), lambda h: (h, 0, 0)),
                   pl.BlockSpec((1, 2, TK, TQ), lambda h: (h, 0, 0, 0))],
        out_shape=[jax.ShapeDtypeStruct((N_HEADS, TAB_C_ROWS, TQ), jnp.float32),
                   jax.ShapeDtypeStruct((N_HEADS, 2, TK, TQ), jnp.float32)],
        name="bias_tables",
    )(rel_bias, jnp.asarray(idx_c), jnp.asarray(idx_n))


def _in_proj_kernel(x_ref, g_ref, wt_ref, wnat_ref, cw_ref,
                    qt_ref, kc_ref, vc_ref, ks_ref, kw_ref, vst_ref, vwt_ref, gt_ref,
                    oconv_ref, carry_ref, stage_ref):
    i = pl.program_id(1)
    tm = x_ref.shape[1]

    @pl.when(i == 0)
    def _():
        carry_ref[...] = jnp.zeros_like(carry_ref)

    hb = _rms(x_ref[0], g_ref[...]).astype(jnp.bfloat16)

    tt = lax.dot_general(wt_ref[...], hb, (((0,), (1,)), ((), ())),
                         preferred_element_type=jnp.float32)
    qt_ref[0] = tt[0:D_NSA].astype(jnp.bfloat16)
    vt = tt[D_NSA:D_NSA + 2 * KV_W]
    ones = jnp.ones((V_ROWS - HEAD_DIM, TK), jnp.bfloat16)
    for t in range(tm // TK):
        for g in range(N_KV):
            for out_ref, r0 in ((vst_ref, g * HEAD_DIM), (vwt_ref, KV_W + g * HEAD_DIM)):
                out_ref[0, t, g, 0:HEAD_DIM, :] = vt[r0:r0 + HEAD_DIM,
                                                      t * TK:(t + 1) * TK].astype(jnp.bfloat16)
                out_ref[0, t, g, HEAD_DIM:V_ROWS, :] = ones
    gt_ref[0] = jax.nn.sigmoid(tt[D_NSA + 2 * KV_W:D_NSA + 2 * KV_W + GATE_ROWS])

    kv_c = jnp.dot(hb, wnat_ref[:, 0:2 * KV_W], preferred_element_type=jnp.float32)
    for a, out_ref in enumerate((kc_ref, vc_ref)):
        stage_ref[a] = kv_c[:, a * KV_W:(a + 1) * KV_W]
        for l in range(S_CMP):
            rows = stage_ref[a, pl.ds(l, tm // S_CMP, stride=S_CMP), :]
            for g in range(N_KV):
                out_ref[0, g, :, l * HEAD_DIM:(l + 1) * HEAD_DIM] = rows[
                    :, g * HEAD_DIM:(g + 1) * HEAD_DIM]
    k_sw = jnp.dot(hb, wnat_ref[:, 2 * KV_W:4 * KV_W],
                   preferred_element_type=jnp.float32)
    blk = (i * tm + lax.broadcasted_iota(jnp.int32, (tm, KV_W), 0)) // L_SLC
    col = lax.broadcasted_iota(jnp.int32, (tm, KV_W), 1)
    onehot = jnp.where(blk == col, 1.0, 0.0)
    ks_ref[0, :, 0:KV_W] = k_sw[:, 0:KV_W].astype(jnp.bfloat16)
    ks_ref[0, :, KV_W:2 * KV_W] = onehot.astype(jnp.bfloat16)
    kw_ref[0] = k_sw[:, KV_W:2 * KV_W].astype(jnp.bfloat16)

    c0 = 4 * KV_W
    bg = jnp.dot(hb, wnat_ref[:, c0:c0 + D_CONV], preferred_element_type=jnp.float32)
    cg = jnp.dot(hb, wnat_ref[:, c0 + D_CONV:c0 + 2 * D_CONV], preferred_element_type=jnp.float32)
    xt = jnp.dot(hb, wnat_ref[:, c0 + 2 * D_CONV:c0 + 3 * D_CONV],
                 preferred_element_type=jnp.float32)
    z = cg * xt
    prev1 = carry_ref[SUBLANES - 1:SUBLANES, :]
    prev2 = carry_ref[SUBLANES - 2:SUBLANES - 1, :]
    row = lax.broadcasted_iota(jnp.int32, z.shape, 0)
    z1 = jnp.where(row == 0, prev1, pltpu.roll(z, 1, 0))
    z2 = jnp.where(row == 0, prev2, jnp.where(row == 1, prev1, pltpu.roll(z, 2, 0)))
    y = cw_ref[0:1, :] * z2 + cw_ref[1:2, :] * z1 + cw_ref[2:3, :] * z
    oconv_ref[0] = (bg * y).astype(jnp.bfloat16)
    carry_ref[...] = z[tm - SUBLANES:tm, :]


def _in_proj(x, g_pre, w_in, conv_w):
    B, S, D = x.shape
    tm = TM_IN
    widths = dict(q=D_NSA, k_c=KV_W, v_c=KV_W, k_s=KV_W, v_s=KV_W, k_w=KV_W, v_w=KV_W,
                  gate=N_GATES, b=D_CONV, c=D_CONV, x=D_CONV)
    assert w_in.shape[1] == sum(widths.values())
    col, cols = 0, {}
    for name, width in widths.items():
        cols[name] = w_in[:, col:col + width]
        col += width
    wnat = jnp.concatenate([cols[n] for n in ('k_c', 'v_c', 'k_s', 'k_w', 'b', 'c', 'x')],
                           axis=1).astype(jnp.bfloat16)
    wgate = jnp.pad(cols['gate'], ((0, 0), (0, GATE_ROWS - N_GATES)))
    wt = jnp.concatenate([(cols['q'] * (HEAD_DIM ** -0.5 * LOG2E)).astype(jnp.bfloat16)]
                         + [w.astype(jnp.bfloat16) for w in (cols['v_s'], cols['v_w'], wgate)],
                         axis=1)
    t_rows = D_NSA + 2 * KV_W + GATE_ROWS
    nat_w = 4 * KV_W + 3 * D_CONV
    nt = S // TK
    const = lambda b, i: (0, 0)
    outs = pl.pallas_call(
        _in_proj_kernel,
        grid=(B, S // tm),
        in_specs=[pl.BlockSpec((1, tm, D), lambda b, i: (b, i, 0)),
                  pl.BlockSpec((1, D), const),
                  pl.BlockSpec((D, t_rows), const),
                  pl.BlockSpec((D, nat_w), const),
                  pl.BlockSpec((3, D_CONV), const)],
        out_specs=[pl.BlockSpec((1, D_NSA, tm), lambda b, i: (b, 0, i)),
                   pl.BlockSpec((1, N_KV, tm // S_CMP, S_CMP * HEAD_DIM), lambda b, i: (b, 0, i, 0)),
                   pl.BlockSpec((1, N_KV, tm // S_CMP, S_CMP * HEAD_DIM), lambda b, i: (b, 0, i, 0)),
                   pl.BlockSpec((1, tm, 2 * KV_W), lambda b, i: (b, i, 0)),
                   pl.BlockSpec((1, tm, KV_W), lambda b, i: (b, i, 0)),
                   pl.BlockSpec((1, tm // TK, N_KV, V_ROWS, TK), lambda b, i: (b, i, 0, 0, 0)),
                   pl.BlockSpec((1, tm // TK, N_KV, V_ROWS, TK), lambda b, i: (b, i, 0, 0, 0)),
                   pl.BlockSpec((1, GATE_ROWS, tm), lambda b, i: (b, 0, i)),
                   pl.BlockSpec((1, tm, D_CONV), lambda b, i: (b, i, 0))],
        out_shape=[jax.ShapeDtypeStruct((B, D_NSA, S), jnp.bfloat16),
                   jax.ShapeDtypeStruct((B, N_KV, S // S_CMP, S_CMP * HEAD_DIM), jnp.float32),
                   jax.ShapeDtypeStruct((B, N_KV, S // S_CMP, S_CMP * HEAD_DIM), jnp.float32),
                   jax.ShapeDtypeStruct((B, S, 2 * KV_W), jnp.bfloat16),
                   jax.ShapeDtypeStruct((B, S, KV_W), jnp.bfloat16),
                   jax.ShapeDtypeStruct((B, nt, N_KV, V_ROWS, TK), jnp.bfloat16),
                   jax.ShapeDtypeStruct((B, nt, N_KV, V_ROWS, TK), jnp.bfloat16),
                   jax.ShapeDtypeStruct((B, GATE_ROWS, S), jnp.float32),
                   jax.ShapeDtypeStruct((B, S, D_CONV), jnp.bfloat16)],
        scratch_shapes=[pltpu.VMEM((SUBLANES, D_CONV), jnp.float32),
                        pltpu.VMEM((2, tm, KV_W), jnp.float32)],
        compiler_params=pltpu.CompilerParams(
            dimension_semantics=("arbitrary", "arbitrary"), vmem_limit_bytes=VMEM_LIMIT),
        name="in_proj",
    )(x, g_pre.reshape(1, D), wt, wnat, conv_w)
    return outs


def _compress_kernel(ck_ref, cv_ref, pek_ref, pev_ref, wk1_ref, wk2_ref, wv1_ref, wv2t_ref,
                     ovl_ref, kc_ref, vct_ref):
    def hidden(c_ref, pe_ref, w1_ref):
        c = c_ref[0, 0]
        a = jnp.dot((c + pe_ref[0:1, :]).astype(jnp.bfloat16), w1_ref[0].astype(jnp.bfloat16),
                    preferred_element_type=jnp.float32)
        b = jnp.dot((c + pe_ref[1:2, :]).astype(jnp.bfloat16), w1_ref[1].astype(jnp.bfloat16),
                    preferred_element_type=jnp.float32)
        return _gelu(a + pltpu.roll(b, N_CMP_PAD - 1, 0)).astype(jnp.bfloat16)

    kc_ref[0, 0] = jnp.dot(hidden(ck_ref, pek_ref, wk1_ref), wk2_ref[...].astype(jnp.bfloat16),
                           preferred_element_type=jnp.float32).astype(jnp.bfloat16)
    vct_ref[0, 0, 0:HEAD_DIM, :] = lax.dot_general(
        wv2t_ref[...].astype(jnp.bfloat16), hidden(cv_ref, pev_ref, wv1_ref), _NT,
        preferred_element_type=jnp.float32).astype(jnp.bfloat16)
    vct_ref[0, 0, HEAD_DIM:V_ROWS, :] = jnp.ones((V_ROWS - HEAD_DIM, N_CMP_PAD), jnp.bfloat16)
    vct_ref[0, 0, V_ROWS:C_ROWS, :] = ovl_ref[...]


def _compress(kc_in, vc_in, pe_k, pe_v, wk1, wk2, wv1, wv2):
    B, _, n_str, half = kc_in.shape
    const2 = lambda b, g: (0, 0)
    const3 = lambda b, g: (0, 0, 0)
    return pl.pallas_call(
        _compress_kernel,
        grid=(B, N_KV),
        in_specs=[pl.BlockSpec((1, 1, n_str, half), lambda b, g: (b, g, 0, 0)),
                  pl.BlockSpec((1, 1, n_str, half), lambda b, g: (b, g, 0, 0)),
                  pl.BlockSpec((2, half), const2),
                  pl.BlockSpec((2, half), const2),
                  pl.BlockSpec((2, half, CMP_HIDDEN), const3),
                  pl.BlockSpec((CMP_HIDDEN, HEAD_DIM), const2),
                  pl.BlockSpec((2, half, CMP_HIDDEN), const3),
                  pl.BlockSpec((HEAD_DIM, CMP_HIDDEN), const2),
                  pl.BlockSpec((C_ROWS - V_ROWS, n_str), const2)],
        out_specs=[pl.BlockSpec((1, 1, n_str, HEAD_DIM), lambda b, g: (b, g, 0, 0)),
                   pl.BlockSpec((1, 1, C_ROWS, n_str), lambda b, g: (b, g, 0, 0))],
        out_shape=[jax.ShapeDtypeStruct((B, N_KV, n_str, HEAD_DIM), jnp.bfloat16),
                   jax.ShapeDtypeStruct((B, N_KV, C_ROWS, n_str), jnp.bfloat16)],
        compiler_params=pltpu.CompilerParams(
            dimension_semantics=("arbitrary", "arbitrary"), vmem_limit_bytes=VMEM_LIMIT),
        name="compress",
    )(kc_in, vc_in, pe_k.reshape(2, half), pe_v.reshape(2, half),
      wk1.reshape(2, half, CMP_HIDDEN), wk2, wv1.reshape(2, half, CMP_HIDDEN), wv2.T,
      _overlap_t(n_str * S_CMP))


def _overlap_t(S):
    n_slc = S // L_SLC
    c_start = np.arange(N_CMP_PAD) * S_CMP
    s_start = np.arange(n_slc) * L_SLC
    ov = np.clip(np.minimum(c_start[:, None] + L_CMP, s_start[None, :] + L_SLC)
                 - np.maximum(c_start[:, None], s_start[None, :]), 0, None).astype(np.float32) / L_CMP
    ov[(S - L_CMP) // S_CMP + 1:, :] = 0.0
    return jnp.asarray(ov.T, dtype=jnp.bfloat16)


def _select_top(score, k, tri):
    w = score
    cnt = jnp.zeros((1, TQ), jnp.float32)
    thr = jnp.zeros((1, TQ), jnp.float32)
    n_gt = jnp.zeros((1, TQ), jnp.float32)
    for _ in range(N_FREE):
        mx = jnp.max(w, axis=0, keepdims=True)
        eq = w == mx
        c = jnp.sum(jnp.where(eq, 1.0, 0.0), axis=0, keepdims=True)
        cross = (cnt < k) & (cnt + c >= k)
        thr = jnp.where(cross, mx, thr)
        n_gt = jnp.where(cross, cnt, n_gt)
        cnt = cnt + c
        w = jnp.where(eq, -jnp.inf, w)
    at_thr = score == thr
    earlier = jnp.dot(tri, jnp.where(at_thr, 1.0, 0.0).astype(jnp.bfloat16),
                      preferred_element_type=jnp.float32)
    return (score > thr) | (at_thr & (earlier + n_gt < k))


def _nsa_attn_kernel(qt_ref, kc_ref, vc_ref, tabc_ref, tri_ref, ks_ref, kw_ref, vst_ref, vwt_ref,
                     near_ref, edge_ref, gt_ref, o_ref,
                     m_ref, acc_ref, mw_ref, accw_ref, s_ref, tmax_ref, sel_ref, oc_ref):
    i = pl.program_id(1)
    zeros64 = jnp.zeros((HEAD_DIM, TQ), jnp.bfloat16)
    off1 = jnp.where(i >= 1, 0.0, MASK_NEG)
    off2 = jnp.where(i >= 2, 0.0, MASK_NEG)
    j1 = jnp.maximum(i - 1, 0)
    j2 = jnp.maximum(i - 2, 0)
    dot = functools.partial(jnp.dot, preferred_element_type=jnp.float32)

    def q_window(h):
        qh = qt_ref[0, h * HEAD_DIM:(h + 1) * HEAD_DIM, :]
        return jnp.concatenate([qh, zeros64] if h // REP == 0 else [zeros64, qh], axis=0)

    def q_selected(h):
        return jnp.concatenate([q_window(h), sel_ref[h // REP]], axis=0)

    def stage_logits(slot, k_ref, j, q_of, bias_of=None):
        k = k_ref[0, pl.ds(pl.multiple_of(j * TK, TK), TK), :]
        for h in range(N_HEADS):
            s = dot(k, q_of(h))
            if bias_of is not None:
                s = s + bias_of(h)
            s_ref[slot, h] = s
            tmax_ref[slot, h:h + 1, :] = jnp.max(s, axis=0, keepdims=True)

    def accumulate(slot, v_ref, j, state, first=False, shift=None):
        ms_ref, as_ref = state
        for h in range(N_HEADS):
            t_max = tmax_ref[slot, h:h + 1, :]
            if shift is not None:
                t_max = t_max + shift
            m_new = t_max if first else jnp.maximum(ms_ref[h:h + 1, :], t_max)
            m_sub = m_new if shift is None else m_new - shift
            p = jnp.exp2(s_ref[slot, h] - m_sub).astype(jnp.bfloat16)
            pv = dot(v_ref[0, j, h // REP], p)
            if first:
                as_ref[h] = pv
            else:
                as_ref[h] = jnp.exp2(ms_ref[h:h + 1, :] - m_new) * as_ref[h] + pv
            ms_ref[h:h + 1, :] = m_new

    sel_state = (m_ref, acc_ref)
    win_state = (mw_ref, accw_ref)
    near0 = lambda h: near_ref[h, 0]
    near1 = lambda h: near_ref[h, 1]
    edge = lambda h: edge_ref[...]

    off_c = pl.multiple_of(TAB_C_ORIGIN - (TQ // S_CMP) * i, TQ // S_CMP)
    for h in range(N_HEADS):
        s = (dot(kc_ref[0, h // REP], qt_ref[0, h * HEAD_DIM:(h + 1) * HEAD_DIM, :])
             + tabc_ref[h, pl.ds(off_c, N_CMP_PAD), :])
        s_ref[0, h] = s
        tmax_ref[0, h:h + 1, :] = jnp.max(s, axis=0, keepdims=True)
    stage_logits(1, kw_ref, i, q_window, near0)

    t_row = i * TQ + lax.broadcasted_iota(jnp.int32, (1, TQ), 1)
    any_cmp = jnp.where(t_row >= L_CMP - 1, 1.0, 0.0)
    n_blk = tri_ref.shape[0]
    j_idx = lax.broadcasted_iota(jnp.int32, (n_blk, TQ), 0)
    cur = (i * TQ + lax.broadcasted_iota(jnp.int32, (n_blk, TQ), 1)) // L_SLC
    forced = (j_idx == 0) | (j_idx == cur) | (j_idx == cur - 1)
    candidate = (j_idx < cur - 1) & (j_idx > 0)
    n_free = float(N_SEL - 1) - jnp.minimum(t_row // L_SLC, 2).astype(jnp.float32)
    for g in range(N_KV):
        imp = jnp.zeros((n_blk, TQ), jnp.float32)
        for h in range(g * REP, (g + 1) * REP):
            p = jnp.exp2(s_ref[0, h] - tmax_ref[0, h:h + 1, :]).astype(jnp.bfloat16)
            a = dot(vc_ref[0, g], p)
            inv = any_cmp / a[HEAD_DIM:HEAD_DIM + 1, :]
            oc_ref[h * HEAD_DIM:(h + 1) * HEAD_DIM, :] = a[0:HEAD_DIM, :] * inv
            imp = imp + a[V_ROWS:V_ROWS + n_blk, :] * inv
        free = _select_top(jnp.where(candidate, imp, MASK_NEG), n_free, tri_ref[...])
        sel = forced | (candidate & free)
        sel_ref[g, 0:n_blk, :] = jnp.where(sel, 0.0, MASK_NEG).astype(jnp.bfloat16)
        sel_ref[g, n_blk:2 * n_blk, :] = jnp.zeros((n_blk, TQ), jnp.bfloat16)

    accumulate(1, vwt_ref, i, win_state, first=True)
    stage_logits(0, ks_ref, i, q_selected, near0)
    stage_logits(1, kw_ref, j1, q_window, near1)
    accumulate(0, vst_ref, i, sel_state, first=True)
    stage_logits(0, ks_ref, j1, q_selected, near1)
    accumulate(1, vwt_ref, j1, win_state, shift=off1)
    stage_logits(1, kw_ref, j2, q_window, edge)
    accumulate(0, vst_ref, j1, sel_state, shift=off1)

    n_far = j1
    last = jnp.maximum(n_far - 1, 0)
    stage_logits(0, ks_ref, 0, q_selected)
    accumulate(1, vwt_ref, j2, win_state, shift=off2)

    def far_pair(t):
        stage_logits(1, ks_ref, t + 1, q_selected)
        accumulate(0, vst_ref, t, sel_state)
        stage_logits(0, ks_ref, jnp.minimum(t + 2, last), q_selected)
        accumulate(1, vst_ref, t + 1, sel_state)

    def far_quad(c, carry):
        far_pair(4 * c)
        far_pair(4 * c + 2)
        return carry

    lax.fori_loop(0, n_far // 4, far_quad, 0)

    @pl.when(n_far % 4 >= 2)
    def _():
        far_pair(n_far // 4 * 4)

    @pl.when(n_far % 2 == 1)
    def _():
        accumulate(0, vst_ref, last, sel_state)

    for pair in range(N_HEADS // 2):
        outs = []
        for h in (2 * pair, 2 * pair + 1):
            rows = slice(h * HEAD_DIM, (h + 1) * HEAD_DIM)
            o_s = acc_ref[h, 0:HEAD_DIM, :] / acc_ref[h, HEAD_DIM:HEAD_DIM + 1, :]
            o_w = accw_ref[h, 0:HEAD_DIM, :] / accw_ref[h, HEAD_DIM:HEAD_DIM + 1, :]
            outs.append(gt_ref[0, 3 * h:3 * h + 1, :] * oc_ref[rows, :]
                        + gt_ref[0, 3 * h + 1:3 * h + 2, :] * o_s
                        + gt_ref[0, 3 * h + 2:3 * h + 3, :] * o_w)
        o_pair = jnp.concatenate(outs, axis=0)
        o_ref[0, :, pair * 2 * HEAD_DIM:(pair + 1) * 2 * HEAD_DIM] = o_pair.T.astype(jnp.bfloat16)


def _nsa_attn(qt, kc, vc, tab_c, ks, kw, vst, vwt, near, gt):
    B, _, S = qt.shape
    ni = S // TQ
    nt = S // TK
    n_blk = S // L_SLC
    key = np.arange(TK)[:, None]
    row = np.arange(TQ)[None, :]
    edge = jnp.asarray(np.where(2 * TK + row - key < WINDOW, 0.0, MASK_NEG).astype(np.float32))
    tri = jnp.asarray(np.tril(np.ones((n_blk, n_blk), np.float32), -1), dtype=jnp.bfloat16)
    per_batch = lambda shape: pl.BlockSpec((1,) + shape, lambda b, i: (b,) + (0,) * len(shape))
    const = lambda shape: pl.BlockSpec(shape, lambda b, i: (0,) * len(shape),
                                       pipeline_mode=pl.Buffered(1))
    return pl.pallas_call(
        _nsa_attn_kernel,
        grid=(B, ni),
        in_specs=[pl.BlockSpec((1, D_NSA, TQ), lambda b, i: (b, 0, i)),
                  per_batch((N_KV, N_CMP_PAD, HEAD_DIM)),
                  per_batch((N_KV, C_ROWS, N_CMP_PAD)),
                  const((N_HEADS, TAB_C_ROWS, TQ)),
                  const((n_blk, n_blk)),
                  per_batch((S, 2 * KV_W)),
                  per_batch((S, KV_W)),
                  per_batch((nt, N_KV, V_ROWS, TK)),
                  per_batch((nt, N_KV, V_ROWS, TK)),
                  const((N_HEADS, 2, TK, TQ)),
                  const((TK, TQ)),
                  pl.BlockSpec((1, GATE_ROWS, TQ), lambda b, i: (b, 0, i))],
        out_specs=pl.BlockSpec((1, TQ, D_NSA), lambda b, i: (b, i, 0)),
        out_shape=jax.ShapeDtypeStruct((B, S, D_NSA), jnp.bfloat16),
        scratch_shapes=[pltpu.VMEM((N_HEADS, TQ), jnp.float32),
                        pltpu.VMEM((N_HEADS, V_ROWS, TQ), jnp.float32),
                        pltpu.VMEM((N_HEADS, TQ), jnp.float32),
                        pltpu.VMEM((N_HEADS, V_ROWS, TQ), jnp.float32),
                        pltpu.VMEM((2, N_HEADS, TK, TQ), jnp.float32),
                        pltpu.VMEM((2, N_HEADS, TQ), jnp.float32),
                        pltpu.VMEM((N_KV, 2 * n_blk, TQ), jnp.bfloat16),
                        pltpu.VMEM((D_NSA, TQ), jnp.float32)],
        compiler_params=pltpu.CompilerParams(
            dimension_semantics=("arbitrary", "arbitrary"), vmem_limit_bytes=VMEM_LIMIT),
        name="nsa_attn",
    )(qt, kc, vc, tab_c, tri, ks, kw, vst, vwt, near, edge, gt)


def _out_ffn_kernel(on_ref, oc_ref, x_ref, wo_ref, gpost_ref, gffn_ref, wup_ref, cw_ref, wdn_ref,
                    g_ref, o_ref, carry_ref, act_ref):
    i = pl.program_id(1)
    tm = x_ref.shape[1]

    @pl.when(i == 0)
    def _():
        carry_ref[...] = jnp.zeros_like(carry_ref)

    y = jnp.dot(jnp.concatenate([on_ref[0], oc_ref[0]], axis=1), wo_ref[...],
                preferred_element_type=jnp.float32)
    x1 = x_ref[0] + _rms(y, gpost_ref[...])
    hb = _rms(x1, gffn_ref[...]).astype(jnp.bfloat16)

    row = lax.broadcasted_iota(jnp.int32, (tm, FF_CHUNK), 0)
    is0 = row == 0
    is1 = row == 1

    def conv(z, c0):
        cols = slice(c0, c0 + FF_CHUNK)
        prev1 = carry_ref[SUBLANES - 1:SUBLANES, cols]
        prev2 = carry_ref[SUBLANES - 2:SUBLANES - 1, cols]
        z1 = jnp.where(is0, prev1, pltpu.roll(z, 1, 0))
        z2 = jnp.where(is0, prev2, jnp.where(is1, prev1, pltpu.roll(z, 2, 0)))
        carry_ref[:, cols] = z[tm - SUBLANES:tm, :]
        return cw_ref[0:1, cols] * z2 + cw_ref[1:2, cols] * z1 + cw_ref[2:3, cols] * z

    for c in range(D_FF // FF_CHUNK):
        g0 = c * FF_CHUNK
        u0 = D_FF + c * FF_CHUNK
        zg = jnp.dot(hb, wup_ref[:, g0:g0 + FF_CHUNK], preferred_element_type=jnp.float32)
        zu = jnp.dot(hb, wup_ref[:, u0:u0 + FF_CHUNK], preferred_element_type=jnp.float32)
        act = _gelu(conv(zg, g0)) * conv(zu, u0)
        act_ref[:, g0:g0 + FF_CHUNK] = act.astype(jnp.bfloat16)
    y2 = jnp.dot(act_ref[...], wdn_ref[...], preferred_element_type=jnp.float32)
    o_ref[0] = x1 + _rms(y2, g_ref[...])


def _out_ffn(o_nsa, o_conv, x, w_out, g_post, g_ffn, w_up, conv_w, w_down, g_ffn_post):
    B, S, D = x.shape
    tm = TM_FFN
    const = lambda b, i: (0, 0)
    single = pl.Buffered(1)
    row_tile = lambda width: pl.BlockSpec((1, tm, width), lambda b, i: (b, i, 0))
    weight = lambda shape: pl.BlockSpec(shape, const, pipeline_mode=single)
    return pl.pallas_call(
        _out_ffn_kernel,
        grid=(B, S // tm),
        in_specs=[row_tile(D_NSA), row_tile(D_CONV), row_tile(D),
                  weight((D, D)), weight((1, D)), weight((1, D)),
                  weight((D, 2 * D_FF)), weight((3, 2 * D_FF)), weight((D_FF, D)), weight((1, D))],
        out_specs=row_tile(D),
        out_shape=jax.ShapeDtypeStruct((B, S, D), jnp.float32),
        scratch_shapes=[pltpu.VMEM((SUBLANES, 2 * D_FF), jnp.float32),
                        pltpu.VMEM((tm, D_FF), jnp.bfloat16)],
        compiler_params=pltpu.CompilerParams(
            dimension_semantics=("arbitrary", "arbitrary"), vmem_limit_bytes=VMEM_LIMIT),
        name="out_ffn",
    )(o_nsa, o_conv, x, w_out.astype(jnp.bfloat16), g_post.reshape(1, D), g_ffn.reshape(1, D),
      w_up.astype(jnp.bfloat16), conv_w, w_down.astype(jnp.bfloat16), g_ffn_post.reshape(1, D))


def kernel(x, norm_mix_pre, norm_mix_post, norm_ffn_pre, norm_ffn_post, w_in, pe_cmp_k, pe_cmp_v,
           w_cmp_k1, w_cmp_k2, w_cmp_v1, w_cmp_v2, rel_bias, conv_mix_w, w_out, w_ffn_up,
           ffn_conv_w, w_ffn_down):
    B, S, D = x.shape
    assert (S, D) == (4096, D_MODEL) and norm_mix_pre.shape[0] == 1
    tab_c, near = _bias_tables(rel_bias)
    for l in range(norm_mix_pre.shape[0]):
        qt, kc_in, vc_in, ks, kw, vst, vwt, gt, o_conv = _in_proj(
            x, norm_mix_pre[l], w_in[l], conv_mix_w[l])
        kc, vc = _compress(kc_in, vc_in, pe_cmp_k[l], pe_cmp_v[l], w_cmp_k1[l], w_cmp_k2[l],
                           w_cmp_v1[l], w_cmp_v2[l])
        o_nsa = _nsa_attn(qt, kc, vc, tab_c, ks, kw, vst, vwt, near, gt)
        x = _out_ffn(o_nsa, o_conv, x, w_out[l], norm_mix_post[l], norm_ffn_pre[l],
                     w_ffn_up[l], ffn_conv_w[l], w_ffn_down[l], norm_ffn_post[l])
    return x
```
